```python
import math
import jax, jax.numpy as jnp
from jax import lax
import numpy as np

D_MODEL = 2048
BATCH = 2
SEQ = 4096
DEPTH = 1

MIX_WIDTH = D_MODEL
ATTN_WIDTH = MIX_WIDTH // 2
LRU_WIDTH = MIX_WIDTH - ATTN_WIDTH
HEAD_DIM = 128
N_ATTN_HEADS = ATTN_WIDTH // HEAD_DIM
N_LRU_BLOCKS = 8
LRU_BLOCK = LRU_WIDTH // N_LRU_BLOCKS
CONV_WIDTH = 4
LRU_C = 8.0
Q_BLOCK = 128
D_FF = -(-8 * D_MODEL // (3 * 256)) * 256
IN_WIDTH = 3 * ATTN_WIDTH + 2 * LRU_WIDTH
RMS_EPS = 1e-6

kernel_name = "hymba_stickbreak_rglru_swiglu_sandwich"


def rmsnorm(x, gain):
    xf = x.astype(jnp.float32)
    var = jnp.mean(xf * xf, axis=-1, keepdims=True)
    return (xf * lax.rsqrt(var + RMS_EPS)).astype(x.dtype) * gain


def stick_breaking_attention(q, k, v):
    b, h, s, dh = q.shape
    n_blk = s // Q_BLOCK
    scale = 1.0 / math.sqrt(dh)
    qf = q.astype(jnp.float32)
    kf = k.astype(jnp.float32)
    vf = v.astype(jnp.float32)
    q_blocks = qf.reshape(b, h, n_blk, Q_BLOCK, dh).transpose(2, 0, 1, 3, 4)
    key_pos = jnp.arange(s)

    def one_block(args):
        qb, blk = args
        z = jnp.einsum('bhqd,bhkd->bhqk', qb, kf) * scale
        q_pos = blk * Q_BLOCK + jnp.arange(Q_BLOCK)
        causal = key_pos[None, :] < q_pos[:, None]
        log_beta = jax.nn.log_sigmoid(z)
        log_1m = jnp.where(causal, jax.nn.log_sigmoid(-z), 0.0)
        suffix = lax.cumsum(log_1m, axis=3, reverse=True) - log_1m
        weights = jnp.where(causal, jnp.exp(log_beta + suffix), 0.0)
        return jnp.einsum('bhqk,bhkd->bhqd', weights, vf)

    out = lax.map(one_block, (q_blocks, jnp.arange(n_blk)))
    out = out.transpose(1, 0, 3, 2, 4).reshape(b, s, h * dh)
    return out.astype(q.dtype)


def causal_depthwise_conv(x, w, bias):
    s = x.shape[1]
    xp = jnp.pad(x, ((0, 0), (CONV_WIDTH - 1, 0), (0, 0)))
    out = sum(xp[:, k:k + s, :] * w[k] for k in range(CONV_WIDTH))
    return out + bias


def block_diag_linear(x, w, bias):
    b, s, c = x.shape
    xr = x.reshape(b, s, N_LRU_BLOCKS, LRU_BLOCK)
    return jnp.einsum('bsnc,ncd->bsnd', xr, w).reshape(b, s, c) + bias


def rg_lru(x, w_rgate, b_rgate, w_igate, b_igate, lru_lambda):
    xf = x.astype(jnp.float32)
    r = jax.nn.sigmoid(block_diag_linear(x, w_rgate, b_rgate).astype(jnp.float32))
    i = jax.nn.sigmoid(block_diag_linear(x, w_igate, b_igate).astype(jnp.float32))
    log_a = LRU_C * r * jax.nn.log_sigmoid(lru_lambda.astype(jnp.float32))
    a = jnp.exp(log_a)
    mult = jnp.sqrt(-jnp.expm1(2.0 * log_a))
    bterm = mult * (i * xf)

    def combine(left, right):
        a_l, b_l = left
        a_r, b_r = right
        return a_l * a_r, a_r * b_l + b_r

    _, h = lax.associative_scan(combine, (a, bterm), axis=1)
    return h.astype(x.dtype)


def hybrid_mixer(hn, w_in, conv_w, conv_b, w_rgate, b_rgate, w_igate, b_igate, lru_lambda,
                 attn_out_norm, lru_out_norm, w_out):
    b, s, _ = hn.shape
    proj = hn @ w_in
    q, k, v, x_lru, g_lru = jnp.split(
        proj, [ATTN_WIDTH, 2 * ATTN_WIDTH, 3 * ATTN_WIDTH, 3 * ATTN_WIDTH + LRU_WIDTH], axis=-1)

    def heads(t):
        return t.reshape(b, s, N_ATTN_HEADS, HEAD_DIM).transpose(0, 2, 1, 3)

    y_attn = stick_breaking_attention(heads(q), heads(k), heads(v))

    xc = causal_depthwise_conv(x_lru, conv_w, conv_b)
    h = rg_lru(xc, w_rgate, b_rgate, w_igate, b_igate, lru_lambda)
    y_lru = h * jax.nn.gelu(g_lru)

    y = jnp.concatenate([rmsnorm(y_attn, attn_out_norm), rmsnorm(y_lru, lru_out_norm)], axis=-1)
    return y @ w_out


def swiglu_ffn(hn, w_gate, w_up, w_down):
    return (jax.nn.silu(hn @ w_gate) * (hn @ w_up)) @ w_down


def setup_inputs(seed: int = 0) -> dict:
    key = jax.random.key(seed)
    ks = jax.random.split(key, 20)
    f32 = jnp.float32

    def normal(k, shape, fan_in):
        return jax.random.normal(k, shape, f32) * (fan_in ** -0.5)

    def gain(k, shape):
        return 1.0 + 0.02 * jax.random.normal(k, shape, f32)

    def small(k, shape):
        return 0.01 * jax.random.normal(k, shape, f32)

    x = jax.random.normal(ks[0], (BATCH, SEQ, D_MODEL), f32)
    u = jax.random.uniform(ks[10], (DEPTH, LRU_WIDTH), f32, 0.9, 0.999)
    a0 = u ** (1.0 / LRU_C)
    lru_lambda = jnp.log(a0) - jnp.log1p(-a0)
    return {
        "x": x,
        "pre_mix_norm": gain(ks[1], (DEPTH, D_MODEL)),
        "post_mix_norm": gain(ks[2], (DEPTH, D_MODEL)),
        "pre_ffn_norm": gain(ks[3], (DEPTH, D_MODEL)),
        "post_ffn_norm": gain(ks[4], (DEPTH, D_MODEL)),
        "w_in": normal(ks[5], (DEPTH, D_MODEL, IN_WIDTH), D_MODEL),
        "conv_w": normal(ks[6], (DEPTH, CONV_WIDTH, LRU_WIDTH), CONV_WIDTH),
        "conv_b": small(ks[7], (DEPTH, LRU_WIDTH)),
        "w_rgate": normal(ks[8], (DEPTH, N_LRU_BLOCKS, LRU_BLOCK, LRU_BLOCK), LRU_BLOCK),
        "b_rgate": small(ks[9], (DEPTH, LRU_WIDTH)),
        "w_igate": normal(ks[11], (DEPTH, N_LRU_BLOCKS, LRU_BLOCK, LRU_BLOCK), LRU_BLOCK),
        "b_igate": small(ks[12], (DEPTH, LRU_WIDTH)),
        "lru_lambda": lru_lambda,
        "attn_out_norm": gain(ks[13], (DEPTH, ATTN_WIDTH)),
        "lru_out_norm": gain(ks[14], (DEPTH, LRU_WIDTH)),
        "w_out": normal(ks[15], (DEPTH, MIX_WIDTH, D_MODEL), MIX_WIDTH),
        "w_ffn_gate": normal(ks[16], (DEPTH, D_MODEL, D_FF), D_MODEL),
        "w_ffn_up": normal(ks[17], (DEPTH, D_MODEL, D_FF), D_MODEL),
        "w_ffn_down": normal(ks[18], (DEPTH, D_FF, D_MODEL), D_FF),
    }


def reference(x, pre_mix_norm, post_mix_norm, pre_ffn_norm, post_ffn_norm, w_in, conv_w, conv_b,
              w_rgate, b_rgate, w_igate, b_igate, lru_lambda, attn_out_norm, lru_out_norm, w_out,
              w_ffn_gate, w_ffn_up, w_ffn_down):
    for l in range(DEPTH):
        hn = rmsnorm(x, pre_mix_norm[l])
        m = hybrid_mixer(hn, w_in[l], conv_w[l], conv_b[l], w_rgate[l], b_rgate[l], w_igate[l],
                         b_igate[l], lru_lambda[l], attn_out_norm[l], lru_out_norm[l], w_out[l])
        x = x + rmsnorm(m, post_mix_norm[l])
        hn = rmsnorm(x, pre_ffn_norm[l])
        f = swiglu_ffn(hn, w_ffn_gate[l], w_ffn_up[l], w_ffn_down[l])
        x = x + rmsnorm(f, post_ffn_norm[l])
    return x
```

```python
import functools
import math

import jax
import jax.numpy as jnp
from jax import lax
from jax.experimental import pallas as pl
from jax.experimental.pallas import tpu as pltpu

F32 = jnp.float32
BF16 = jnp.bfloat16

HEAD_DIM = 128
LRU_BLOCK = 128
CONV_WIDTH = 4
LRU_C = 8.0
RMS_EPS = 1e-6
SUBLANES = 8
VMEM_LIMIT = 56 * 1024 * 1024


def _params(n_axes):
    return pltpu.CompilerParams(dimension_semantics=("arbitrary",) * n_axes,
                                vmem_limit_bytes=VMEM_LIMIT)


def _rms(x, gain):
    var = jnp.mean(x * x, axis=-1, keepdims=True)
    return (x * lax.rsqrt(var + RMS_EPS)) * gain


def _log_sigmoid(z):
    return jnp.minimum(z, 0.0) - jnp.log1p(jnp.exp(-jnp.abs(z)))


def _inproj_kernel(x_ref, g_ref, w_ref, qkv_ref, lru_ref, hn_ref, *, n_qkv_tiles):
    j = pl.program_id(1)

    @pl.when(j == 0)
    def _():
        hn_ref[...] = _rms(x_ref[...], g_ref[...]).astype(BF16)

    acc = jnp.dot(hn_ref[...], w_ref[...], preferred_element_type=F32)

    @pl.when(j < n_qkv_tiles)
    def _():
        qkv_ref[...] = acc.astype(BF16)

    @pl.when(j >= n_qkv_tiles)
    def _():
        lru_ref[...] = acc


def _inproj(x2, gain, w_bf, qkv_width, tm=1024, tn=512):
    m, d = x2.shape
    n = w_bf.shape[1]
    lru_width = n - qkv_width
    n_qkv_tiles = qkv_width // tn
    grid = (m // tm, n // tn)
    return pl.pallas_call(
        functools.partial(_inproj_kernel, n_qkv_tiles=n_qkv_tiles),
        grid=grid,
        in_specs=[
            pl.BlockSpec((tm, d), lambda i, j: (i, 0)),
            pl.BlockSpec((1, d), lambda i, j: (0, 0)),
            pl.BlockSpec((d, tn), lambda i, j: (0, j)),
        ],
        out_specs=[
            pl.BlockSpec((tm, tn), lambda i, j: (i, jnp.minimum(j, n_qkv_tiles - 1))),
            pl.BlockSpec((tm, tn), lambda i, j: (i, jnp.maximum(j - n_qkv_tiles, 0))),
        ],
        out_shape=[
            jax.ShapeDtypeStruct((m, qkv_width), BF16),
            jax.ShapeDtypeStruct((m, lru_width), F32),
        ],
        scratch_shapes=[pltpu.VMEM((tm, d), BF16)],
        compiler_params=_params(2),
        name="inproj",
    )(x2, gain, w_bf)


def _attn_kernel(q_ref, k_ref, v_ref, o_ref, *, tq, scale):
    tk = tq
    qi = pl.program_id(2)
    q = q_ref[...]
    row = lax.broadcasted_iota(jnp.int32, (tk, tk), 0)
    col = lax.broadcasted_iota(jnp.int32, (tk, tk), 1)
    later = (row > col).astype(BF16)
    causal = col < row

    def tile(kj, carry, masked):
        acc, tail = carry
        start = pl.multiple_of(kj * tk, tk)
        k_t = k_ref[pl.ds(start, tk), :]
        v_t = v_ref[pl.ds(start, tk), :]
        z = lax.dot_general(q, k_t, (((1,), (1,)), ((), ())),
                            preferred_element_type=F32) * scale
        log_beta = _log_sigmoid(z)
        log_1m = log_beta - z
        if masked:
            log_1m = jnp.where(causal, log_1m, 0.0)
        suffix = jnp.dot(log_1m.astype(BF16), later, preferred_element_type=F32)
        w = jnp.exp(log_beta + suffix + tail)
        if masked:
            w = jnp.where(causal, w, 0.0)
        acc = acc + jnp.dot(w.astype(BF16), v_t, preferred_element_type=F32)
        tail = tail + (suffix[:, 0:1] + log_1m[:, 0:1])
        return acc, tail

    carry = (jnp.zeros((tq, HEAD_DIM), F32), jnp.zeros((tq, 1), F32))
    carry = tile(qi, carry, True)
    acc, _ = lax.fori_loop(0, qi, lambda it, c: tile(qi - 1 - it, c, False), carry)
    o_ref[...] = acc


def _attention(qkv, batch, seq, n_heads, tq=256):
    m = qkv.shape[0]
    nq = seq // tq
    return pl.pallas_call(
        functools.partial(_attn_kernel, tq=tq, scale=1.0 / math.sqrt(HEAD_DIM)),
        grid=(batch, n_heads, nq),
        in_specs=[
            pl.BlockSpec((tq, HEAD_DIM), lambda b, h, i: (b * nq + i, h)),
            pl.BlockSpec((seq, HEAD_DIM), lambda b, h, i: (b, n_heads + h)),
            pl.BlockSpec((seq, HEAD_DIM), lambda b, h, i: (b, 2 * n_heads + h)),
        ],
        out_specs=pl.BlockSpec((tq, HEAD_DIM), lambda b, h, i: (b * nq + i, h)),
        out_shape=jax.ShapeDtypeStruct((m, n_heads * HEAD_DIM), F32),
        compiler_params=_params(3),
        name="stickbreak_attn",
    )(qkv, qkv, qkv)


def _lru_kernel(x_ref, gate_ref, cw_ref, cb_ref, wr_ref, wi_ref, br_ref, bi_ref, lam_ref,
                gn_ref, o_ref, xbuf, a_buf, b_buf, h_buf, hcarry, *, tt):
    t = pl.program_id(1)
    c = x_ref.shape[1]
    n_blocks = c // LRU_BLOCK

    @pl.when(t == 0)
    def _():
        xbuf[0:SUBLANES, :] = jnp.zeros((SUBLANES, c), F32)
        hcarry[...] = jnp.zeros((SUBLANES, c), F32)

    xbuf[SUBLANES:SUBLANES + tt, :] = x_ref[...]
    xc = cb_ref[...] + cw_ref[0:1, :] * xbuf[pl.ds(SUBLANES - 3, tt), :]
    for k in range(1, CONV_WIDTH):
        xc = xc + cw_ref[k:k + 1, :] * xbuf[pl.ds(SUBLANES - 3 + k, tt), :]
    xbuf[0:SUBLANES, :] = xbuf[tt:tt + SUBLANES, :]

    row_in_vreg = lax.broadcasted_iota(jnp.int32, (tt, LRU_BLOCK), 0) % SUBLANES
    log_sig_lam = _log_sigmoid(lam_ref[...])
    xcb = xc.astype(BF16)
    for n in range(n_blocks):
        sl = slice(n * LRU_BLOCK, (n + 1) * LRU_BLOCK)
        xn = xc[:, sl]
        r = jax.nn.sigmoid(jnp.dot(xcb[:, sl], wr_ref[n], preferred_element_type=F32)
                           + br_ref[:, sl])
        i = jax.nn.sigmoid(jnp.dot(xcb[:, sl], wi_ref[n], preferred_element_type=F32)
                           + bi_ref[:, sl])
        log_a = LRU_C * r * log_sig_lam[:, sl]
        a = jnp.exp(log_a)
        b = jnp.sqrt(-jnp.tanh(log_a) * (1.0 + a * a)) * (i * xn)
        for shift in (1, 2, 4):
            keep = row_in_vreg >= shift
            a_prev = jnp.where(keep, pltpu.roll(a, shift, 0), 1.0)
            b_prev = jnp.where(keep, pltpu.roll(b, shift, 0), 0.0)
            b = a * b_prev + b
            a = a * a_prev
        a_buf[:, sl] = a
        b_buf[:, sl] = b

    def group(g, hb):
        rows = pl.ds(pl.multiple_of(g * SUBLANES, SUBLANES), SUBLANES)
        h = a_buf[rows, :] * hb + b_buf[rows, :]
        h_buf[rows, :] = h
        return jnp.broadcast_to(h[SUBLANES - 1:SUBLANES, :], (SUBLANES, c))

    hcarry[...] = lax.fori_loop(0, tt // SUBLANES, group, hcarry[...])

    y = h_buf[...] * jax.nn.gelu(gate_ref[...], approximate=True)
    o_ref[...] = _rms(y, gn_ref[...]).astype(o_ref.dtype)


def _lru_branch(lru_in, conv_w, conv_b, w_r, w_i, b_r, b_i, lam, gain, batch, seq, tt=256):
    m = lru_in.shape[0]
    c = lru_in.shape[1] // 2
    nt = seq // tt
    n_blocks = c // LRU_BLOCK
    row = lambda b, t: (0, 0)
    return pl.pallas_call(
        functools.partial(_lru_kernel, tt=tt),
        grid=(batch, nt),
        in_specs=[
            pl.BlockSpec((tt, c), lambda b, t: (b * nt + t, 0)),
            pl.BlockSpec((tt, c), lambda b, t: (b * nt + t, 1)),
            pl.BlockSpec((CONV_WIDTH, c), row),
            pl.BlockSpec((1, c), row),
            pl.BlockSpec((n_blocks, LRU_BLOCK, LRU_BLOCK), lambda b, t: (0, 0, 0)),
            pl.BlockSpec((n_blocks, LRU_BLOCK, LRU_BLOCK), lambda b, t: (0, 0, 0)),
            pl.BlockSpec((1, c), row),
            pl.BlockSpec((1, c), row),
            pl.BlockSpec((1, c), row),
            pl.BlockSpec((1, c), row),
        ],
        out_specs=pl.BlockSpec((tt, c), lambda b, t: (b * nt + t, 0)),
        out_shape=jax.ShapeDtypeStruct((m, c), BF16),
        scratch_shapes=[
            pltpu.VMEM((tt + SUBLANES, c), F32),
            pltpu.VMEM((tt, c), F32),
            pltpu.VMEM((tt, c), F32),
            pltpu.VMEM((tt, c), F32),
            pltpu.VMEM((SUBLANES, c), F32),
        ],
        compiler_params=_params(2),
        name="rglru",
    )(lru_in, lru_in, conv_w, conv_b, w_r, w_i, b_r, b_i, lam, gain)


def _outproj_kernel(ya_ref, yl_ref, ga_ref, w_ref, x_ref, gpost_ref, gpre_ref, x1_ref, hn_ref):
    ya = _rms(ya_ref[...], ga_ref[...]).astype(BF16)
    y = jnp.concatenate([ya, yl_ref[...]], axis=1)
    mix = jnp.dot(y, w_ref[...], preferred_element_type=F32)
    x1 = x_ref[...] + _rms(mix, gpost_ref[...])
    x1_ref[...] = x1
    hn_ref[...] = _rms(x1, gpre_ref[...]).astype(BF16)


def _outproj(y_attn, y_lru, attn_gain, w_bf, x2, post_gain, pre_gain, tm=256):
    m, d = x2.shape
    ca = y_attn.shape[1]
    cl = y_lru.shape[1]
    row = lambda i: (0, 0)
    return pl.pallas_call(
        _outproj_kernel,
        grid=(m // tm,),
        in_specs=[
            pl.BlockSpec((tm, ca), lambda i: (i, 0)),
            pl.BlockSpec((tm, cl), lambda i: (i, 0)),
            pl.BlockSpec((1, ca), row),
            pl.BlockSpec((ca + cl, d), row),
            pl.BlockSpec((tm, d), lambda i: (i, 0)),
            pl.BlockSpec((1, d), row),
            pl.BlockSpec((1, d), row),
        ],
        out_specs=[
            pl.BlockSpec((tm, d), lambda i: (i, 0)),
            pl.BlockSpec((tm, d), lambda i: (i, 0)),
        ],
        out_shape=[
            jax.ShapeDtypeStruct((m, d), F32),
            jax.ShapeDtypeStruct((m, d), BF16),
        ],
        compiler_params=_params(1),
        name="outproj",
    )(y_attn, y_lru, attn_gain, w_bf, x2, post_gain, pre_gain)


def _ffn_kernel(hn_ref, wg_ref, wu_ref, wd_ref, x1_ref, gain_ref, o_ref):
    f = pl.program_id(1)
    hn = hn_ref[...]
    gate = jnp.dot(hn, wg_ref[...], preferred_element_type=F32)
    up = jnp.dot(hn, wu_ref[...], preferred_element_type=F32)
    act = (jax.nn.silu(gate) * up).astype(BF16)
    part = jnp.dot(act, wd_ref[...], preferred_element_type=F32)

    @pl.when(f == 0)
    def _():
        o_ref[...] = part

    @pl.when(f > 0)
    def _():
        o_ref[...] += part

    @pl.when(f == pl.num_programs(1) - 1)
    def _():
        o_ref[...] = x1_ref[...] + _rms(o_ref[...], gain_ref[...])


def _ffn(hn, wg, wu, wd, x1, gain, tm=512, tf=512):
    m, d = hn.shape
    dff = wg.shape[1]
    return pl.pallas_call(
        _ffn_kernel,
        grid=(m // tm, dff // tf),
        in_specs=[
            pl.BlockSpec((tm, d), lambda i, f: (i, 0)),
            pl.BlockSpec((d, tf), lambda i, f: (0, f)),
            pl.BlockSpec((d, tf), lambda i, f: (0, f)),
            pl.BlockSpec((tf, d), lambda i, f: (f, 0)),
            pl.BlockSpec((tm, d), lambda i, f: (i, 0)),
            pl.BlockSpec((1, d), lambda i, f: (0, 0)),
        ],
        out_specs=pl.BlockSpec((tm, d), lambda i, f: (i, 0)),
        out_shape=jax.ShapeDtypeStruct((m, d), F32),
        compiler_params=_params(2),
        name="swiglu_ffn",
    )(hn, wg, wu, wd, x1, gain)


def _layer(x2, batch, seq, pre_mix, post_mix, pre_ffn, post_ffn, w_in, conv_w, conv_b, w_r, b_r,
           w_i, b_i, lam, attn_gain, lru_gain, w_out, w_g, w_u, w_d):
    row = lambda v: v.reshape(1, -1)
    lru_width = conv_w.shape[1]
    qkv_width = w_in.shape[1] - 2 * lru_width
    n_heads = qkv_width // (3 * HEAD_DIM)

    qkv, lru_in = _inproj(x2, row(pre_mix), w_in.astype(BF16), qkv_width)
    y_attn = _attention(qkv, batch, seq, n_heads)
    y_lru = _lru_branch(lru_in, conv_w, row(conv_b), w_r.astype(BF16), w_i.astype(BF16),
                        row(b_r), row(b_i), row(lam), row(lru_gain), batch, seq)
    x1, hn2 = _outproj(y_attn, y_lru, row(attn_gain), w_out.astype(BF16), x2, row(post_mix),
                       row(pre_ffn))
    return _ffn(hn2, w_g.astype(BF16), w_u.astype(BF16), w_d.astype(BF16), x1, row(post_ffn))


def kernel(x, pre_mix_norm, post_mix_norm, pre_ffn_norm, post_ffn_norm, w_in, conv_w, conv_b,
           w_rgate, b_rgate, w_igate, b_igate, lru_lambda, attn_out_norm, lru_out_norm, w_out,
           w_ffn_gate, w_ffn_up, w_ffn_down):
    batch, seq, d = x.shape
    x2 = x.reshape(batch * seq, d)
    for l in range(w_in.shape[0]):
        x2 = _layer(x2, batch, seq, pre_mix_norm[l], post_mix_norm[l], pre_ffn_norm[l],
                    post_ffn_norm[l], w_in[l], conv_w[l], conv_b[l], w_rgate[l], b_rgate[l],
                    w_igate[l], b_igate[l], lru_lambda[l], attn_out_norm[l], lru_out_norm[l],
                    w_out[l], w_ffn_gate[l], w_ffn_up[l], w_ffn_down[l])
    return x2.reshape(batch, seq, d)
```

```python
import functools
import math

import jax
import jax.numpy as jnp
from jax import lax
from jax.experimental import pallas as pl
from jax.experimental.pallas import tpu as pltpu

F32 = jnp.float32
BF16 = jnp.bfloat16

HEAD_DIM = 128
LRU_BLOCK = 128
CONV_WIDTH = 4
LRU_C = 8.0
RMS_EPS = 1e-6
LOG2_E = math.log2(math.e)
Q_SCALE = LOG2_E / math.sqrt(HEAD_DIM)
F32_ZERO_LOG2 = -150.0
SUBLANES = 8
VMEM_LIMIT = 56 * 1024 * 1024


def _params(n_axes):
    return pltpu.CompilerParams(dimension_semantics=("arbitrary",) * n_axes,
                                vmem_limit_bytes=VMEM_LIMIT)


def _rms(x, gain):
    var = jnp.mean(x * x, axis=-1, keepdims=True)
    return (x * lax.rsqrt(var + RMS_EPS)) * gain


def _log_sigmoid(z):
    return jnp.minimum(z, 0.0) - jnp.log1p(jnp.exp(-jnp.abs(z)))


def _inproj_kernel(x_ref, g_ref, w_ref, qkv_ref, lru_ref, hn_ref, *, n_q_tiles, n_qkv_tiles):
    j = pl.program_id(1)

    @pl.when(j == 0)
    def _():
        hn_ref[...] = _rms(x_ref[...], g_ref[...]).astype(BF16)

    acc = jnp.dot(hn_ref[...], w_ref[...], preferred_element_type=F32)

    @pl.when(j < n_qkv_tiles)
    def _():
        qkv_ref[...] = (acc * jnp.where(j < n_q_tiles, Q_SCALE, 1.0)).astype(BF16)

    @pl.when(j >= n_qkv_tiles)
    def _():
        lru_ref[...] = acc


def _inproj(x2, gain, w_bf, qkv_width, tm=1024, tn=512):
    m, d = x2.shape
    n = w_bf.shape[1]
    lru_width = n - qkv_width
    n_qkv_tiles = qkv_width // tn
    n_q_tiles = qkv_width // (3 * tn)
    grid = (m // tm, n // tn)
    return pl.pallas_call(
        functools.partial(_inproj_kernel, n_q_tiles=n_q_tiles, n_qkv_tiles=n_qkv_tiles),
        grid=grid,
        in_specs=[
            pl.BlockSpec((tm, d), lambda i, j: (i, 0)),
            pl.BlockSpec((1, d), lambda i, j: (0, 0)),
            pl.BlockSpec((d, tn), lambda i, j: (0, j)),
        ],
        out_specs=[
            pl.BlockSpec((tm, tn), lambda i, j: (i, jnp.minimum(j, n_qkv_tiles - 1))),
            pl.BlockSpec((tm, tn), lambda i, j: (i, jnp.maximum(j - n_qkv_tiles, 0))),
        ],
        out_shape=[
            jax.ShapeDtypeStruct((m, qkv_width), BF16),
            jax.ShapeDtypeStruct((m, lru_width), F32),
        ],
        scratch_shapes=[pltpu.VMEM((tm, d), BF16)],
        compiler_params=_params(2),
        name="inproj",
    )(x2, gain, w_bf)


def _attn_kernel(q_ref, k_ref, v_ref, o_ref, tail_ref, *, tq, heads):
    tk = tq
    qi = pl.program_id(2)
    row = lax.broadcasted_iota(jnp.int32, (tk, tk), 0)
    col = lax.broadcasted_iota(jnp.int32, (tk, tk), 1)
    neg_later = jnp.where(row > col, -1.0, 0.0).astype(BF16)

    def tile(kj, first):
        start = pl.multiple_of(kj * tk, tk)
        log_betas = []
        softplus = []
        for h in range(heads):
            sl = slice(h * HEAD_DIM, (h + 1) * HEAD_DIM)
            k_t = k_ref[pl.ds(start, tk), sl]
            z2 = lax.dot_general(q_ref[:, sl], k_t, (((1,), (1,)), ((), ())),
                                 preferred_element_type=F32)
            pos = jnp.maximum(z2, 0.0)
            neg = z2 - pos
            log_term = jnp.log(1.0 + jnp.exp2(neg - pos)) * LOG2_E
            sp = pos + log_term
            if first:
                sp = jnp.where(col < row, sp, 0.0)
            softplus.append(sp)
            log_betas.append(neg - log_term)
        sp_all = jnp.concatenate(softplus, axis=0).astype(BF16)
        suffix_all = jnp.dot(sp_all, neg_later, preferred_element_type=F32)
        for h in range(heads):
            sl = slice(h * HEAD_DIM, (h + 1) * HEAD_DIM)
            v_t = v_ref[pl.ds(start, tk), sl]
            suffix = suffix_all[h * tq:(h + 1) * tq]
            expo = log_betas[h] + suffix
            if not first:
                expo = expo + jnp.concatenate([tail_ref[h]] * (tk // HEAD_DIM), axis=1)
            w = jnp.exp2(expo)
            if first:
                w = jnp.where(col < row, w, 0.0)
            pv = jnp.dot(w.astype(BF16), v_t, preferred_element_type=F32)
            total = jnp.broadcast_to(suffix[:, 0:1] - softplus[h][:, 0:1], (tq, HEAD_DIM))
            if first:
                o_ref[:, sl] = pv
                tail_ref[h] = total
            else:
                o_ref[:, sl] += pv
                tail_ref[h] += total

    def stick_left():
        return jnp.max(tail_ref[...]) > F32_ZERO_LOG2

    tile(qi, True)

    def body(carry):
        it, _ = carry
        tile(qi - 1 - it, False)
        return it + 1, stick_left()

    lax.while_loop(lambda c: jnp.logical_and(c[0] < qi, c[1]), body,
                   (jnp.int32(0), stick_left()))


def _attention(qkv, batch, seq, n_heads, tq=256, heads=4):
    m = qkv.shape[0]
    nq = seq // tq
    ng = n_heads // heads
    width = heads * HEAD_DIM
    return pl.pallas_call(
        functools.partial(_attn_kernel, tq=tq, heads=heads),
        grid=(batch, ng, nq),
        in_specs=[
            pl.BlockSpec((tq, width), lambda b, g, i: (b * nq + i, g)),
            pl.BlockSpec((seq, width), lambda b, g, i: (b, ng + g)),
            pl.BlockSpec((seq, width), lambda b, g, i: (b, 2 * ng + g)),
        ],
        out_specs=pl.BlockSpec((tq, width), lambda b, g, i: (b * nq + i, g)),
        out_shape=jax.ShapeDtypeStruct((m, n_heads * HEAD_DIM), F32),
        scratch_shapes=[pltpu.VMEM((heads, tq, HEAD_DIM), F32)],
        compiler_params=_params(3),
        name="stickbreak_attn",
    )(qkv, qkv, qkv)


def _lru_kernel(x_ref, gate_ref, cw_ref, cb_ref, wr_ref, wi_ref, br_ref, bi_ref, lam_ref,
                gn_ref, o_ref, xbuf, a_buf, b_buf, h_buf, hcarry, *, tt):
    t = pl.program_id(1)
    c = x_ref.shape[1]
    n_blocks = c // LRU_BLOCK

    @pl.when(t == 0)
    def _():
        xbuf[0:SUBLANES, :] = jnp.zeros((SUBLANES, c), F32)
        hcarry[...] = jnp.zeros((SUBLANES, c), F32)

    xbuf[SUBLANES:SUBLANES + tt, :] = x_ref[...]
    xc = cb_ref[...] + cw_ref[0:1, :] * xbuf[pl.ds(SUBLANES - 3, tt), :]
    for k in range(1, CONV_WIDTH):
        xc = xc + cw_ref[k:k + 1, :] * xbuf[pl.ds(SUBLANES - 3 + k, tt), :]
    xbuf[0:SUBLANES, :] = xbuf[tt:tt + SUBLANES, :]

    row_in_vreg = lax.broadcasted_iota(jnp.int32, (tt, LRU_BLOCK), 0) % SUBLANES
    log_sig_lam = _log_sigmoid(lam_ref[...])
    xcb = xc.astype(BF16)
    for n in range(n_blocks):
        sl = slice(n * LRU_BLOCK, (n + 1) * LRU_BLOCK)
        xn = xc[:, sl]
        r = jax.nn.sigmoid(jnp.dot(xcb[:, sl], wr_ref[n], preferred_element_type=F32)
                           + br_ref[:, sl])
        i = jax.nn.sigmoid(jnp.dot(xcb[:, sl], wi_ref[n], preferred_element_type=F32)
                           + bi_ref[:, sl])
        log_a = LRU_C * r * log_sig_lam[:, sl]
        a = jnp.exp(log_a)
        b = jnp.sqrt(-jnp.tanh(log_a) * (1.0 + a * a)) * (i * xn)
        for shift in (1, 2, 4):
            keep = row_in_vreg >= shift
            a_prev = jnp.where(keep, pltpu.roll(a, shift, 0), 1.0)
            b_prev = jnp.where(keep, pltpu.roll(b, shift, 0), 0.0)
            b = a * b_prev + b
            a = a * a_prev
        a_buf[:, sl] = a
        b_buf[:, sl] = b

    def group(g, hb):
        rows = pl.ds(pl.multiple_of(g * SUBLANES, SUBLANES), SUBLANES)
        h = a_buf[rows, :] * hb + b_buf[rows, :]
        h_buf[rows, :] = h
        return jnp.broadcast_to(h[SUBLANES - 1:SUBLANES, :], (SUBLANES, c))

    hcarry[...] = lax.fori_loop(0, tt // SUBLANES, group, hcarry[...])

    y = h_buf[...] * jax.nn.gelu(gate_ref[...], approximate=True)
    o_ref[...] = _rms(y, gn_ref[...]).astype(o_ref.dtype)


def _lru_branch(lru_in, conv_w, conv_b, w_r, w_i, b_r, b_i, lam, gain, batch, seq, tt=256):
    m = lru_in.shape[0]
    c = lru_in.shape[1] // 2
    nt = seq // tt
    n_blocks = c // LRU_BLOCK
    row = lambda b, t: (0, 0)
    return pl.pallas_call(
        functools.partial(_lru_kernel, tt=tt),
        grid=(batch, nt),
        in_specs=[
            pl.BlockSpec((tt, c), lambda b, t: (b * nt + t, 0)),
            pl.BlockSpec((tt, c), lambda b, t: (b * nt + t, 1)),
            pl.BlockSpec((CONV_WIDTH, c), row),
            pl.BlockSpec((1, c), row),
            pl.BlockSpec((n_blocks, LRU_BLOCK, LRU_BLOCK), lambda b, t: (0, 0, 0)),
            pl.BlockSpec((n_blocks, LRU_BLOCK, LRU_BLOCK), lambda b, t: (0, 0, 0)),
            pl.BlockSpec((1, c), row),
            pl.BlockSpec((1, c), row),
            pl.BlockSpec((1, c), row),
            pl.BlockSpec((1, c), row),
        ],
        out_specs=pl.BlockSpec((tt, c), lambda b, t: (b * nt + t, 0)),
        out_shape=jax.ShapeDtypeStruct((m, c), BF16),
        scratch_shapes=[
            pltpu.VMEM((tt + SUBLANES, c), F32),
            pltpu.VMEM((tt, c), F32),
            pltpu.VMEM((tt, c), F32),
            pltpu.VMEM((tt, c), F32),
            pltpu.VMEM((SUBLANES, c), F32),
        ],
        compiler_params=_params(2),
        name="rglru",
    )(lru_in, lru_in, conv_w, conv_b, w_r, w_i, b_r, b_i, lam, gain)


def _outproj_kernel(ya_ref, yl_ref, ga_ref, w_ref, x_ref, gpost_ref, gpre_ref, x1_ref, hn_ref):
    ya = _rms(ya_ref[...], ga_ref[...]).astype(BF16)
    y = jnp.concatenate([ya, yl_ref[...]], axis=1)
    mix = jnp.dot(y, w_ref[...], preferred_element_type=F32)
    x1 = x_ref[...] + _rms(mix, gpost_ref[...])
    x1_ref[...] = x1
    hn_ref[...] = _rms(x1, gpre_ref[...]).astype(BF16)


def _outproj(y_attn, y_lru, attn_gain, w_bf, x2, post_gain, pre_gain, tm=256):
    m, d = x2.shape
    ca = y_attn.shape[1]
    cl = y_lru.shape[1]
    row = lambda i: (0, 0)
    return pl.pallas_call(
        _outproj_kernel,
        grid=(m // tm,),
        in_specs=[
            pl.BlockSpec((tm, ca), lambda i: (i, 0)),
            pl.BlockSpec((tm, cl), lambda i: (i, 0)),
            pl.BlockSpec((1, ca), row),
            pl.BlockSpec((ca + cl, d), row),
            pl.BlockSpec((tm, d), lambda i: (i, 0)),
            pl.BlockSpec((1, d), row),
            pl.BlockSpec((1, d), row),
        ],
        out_specs=[
            pl.BlockSpec((tm, d), lambda i: (i, 0)),
            pl.BlockSpec((tm, d), lambda i: (i, 0)),
        ],
        out_shape=[
            jax.ShapeDtypeStruct((m, d), F32),
            jax.ShapeDtypeStruct((m, d), BF16),
        ],
        compiler_params=_params(1),
        name="outproj",
    )(y_attn, y_lru, attn_gain, w_bf, x2, post_gain, pre_gain)


def _ffn_kernel(hn_ref, wg_ref, wu_ref, wd_ref, x1_ref, gain_ref, o_ref):
    f = pl.program_id(1)
    hn = hn_ref[...]
    gate = jnp.dot(hn, wg_ref[...], preferred_element_type=F32)
    up = jnp.dot(hn, wu_ref[...], preferred_element_type=F32)
    act = (jax.nn.silu(gate) * up).astype(BF16)
    part = jnp.dot(act, wd_ref[...], preferred_element_type=F32)

    @pl.when(f == 0)
    def _():
        o_ref[...] = part

    @pl.when(f > 0)
    def _():
        o_ref[...] += part

    @pl.when(f == pl.num_programs(1) - 1)
    def _():
        o_ref[...] = x1_ref[...] + _rms(o_ref[...], gain_ref[...])


def _ffn(hn, wg, wu, wd, x1, gain, tm=512, tf=512):
    m, d = hn.shape
    dff = wg.shape[1]
    return pl.pallas_call(
        _ffn_kernel,
        grid=(m // tm, dff // tf),
        in_specs=[
            pl.BlockSpec((tm, d), lambda i, f: (i, 0)),
            pl.BlockSpec((d, tf), lambda i, f: (0, f)),
            pl.BlockSpec((d, tf), lambda i, f: (0, f)),
            pl.BlockSpec((tf, d), lambda i, f: (f, 0)),
            pl.BlockSpec((tm, d), lambda i, f: (i, 0)),
            pl.BlockSpec((1, d), lambda i, f: (0, 0)),
        ],
        out_specs=pl.BlockSpec((tm, d), lambda i, f: (i, 0)),
        out_shape=jax.ShapeDtypeStruct((m, d), F32),
        compiler_params=_params(2),
        name="swiglu_ffn",
    )(hn, wg, wu, wd, x1, gain)


def _layer(x2, batch, seq, pre_mix, post_mix, pre_ffn, post_ffn, w_in, conv_w, conv_b, w_r, b_r,
           w_i, b_i, lam, attn_gain, lru_gain, w_out, w_g, w_u, w_d):
    row = lambda v: v.reshape(1, -1)
    lru_width = conv_w.shape[1]
    qkv_width = w_in.shape[1] - 2 * lru_width
    n_heads = qkv_width // (3 * HEAD_DIM)

    qkv, lru_in = _inproj(x2, row(pre_mix), w_in.astype(BF16), qkv_width)
    y_attn = _attention(qkv, batch, seq, n_heads)
    y_lru = _lru_branch(lru_in, conv_w, row(conv_b), w_r.astype(BF16), w_i.astype(BF16),
                        row(b_r), row(b_i), row(lam), row(lru_gain), batch, seq)
    x1, hn2 = _outproj(y_attn, y_lru, row(attn_gain), w_out.astype(BF16), x2, row(post_mix),
                       row(pre_ffn))
    return _ffn(hn2, w_g.astype(BF16), w_u.astype(BF16), w_d.astype(BF16), x1, row(post_ffn))


def kernel(x, pre_mix_norm, post_mix_norm, pre_ffn_norm, post_ffn_norm, w_in, conv_w, conv_b,
           w_rgate, b_rgate, w_igate, b_igate, lru_lambda, attn_out_norm, lru_out_norm, w_out,
           w_ffn_gate, w_ffn_up, w_ffn_down):
    batch, seq, d = x.shape
    x2 = x.reshape(batch * seq, d)
    for l in range(w_in.shape[0]):
        x2 = _layer(x2, batch, seq, pre_mix_norm[l], post_mix_norm[l], pre_ffn_norm[l],
                    post_ffn_norm[l], w_in[l], conv_w[l], conv_b[l], w_rgate[l], b_rgate[l],
                    w_igate[l], b_igate[l], lru_lambda[l], attn_out_norm[l], lru_out_norm[l],
                    w_out[l], w_ffn_gate[l], w_ffn_up[l], w_ffn_down[l])
    return x2.reshape(batch, seq, d)
```

```python
import functools
import math

import jax
import jax.numpy as jnp
from jax import lax
from jax.experimental import pallas as pl
from jax.experimental.pallas import tpu as pltpu

F32 = jnp.float32
BF16 = jnp.bfloat16

HEAD_DIM = 128
LRU_BLOCK = 128
CONV_WIDTH = 4
LRU_C = 8.0
RMS_EPS = 1e-6
LOG2_E = math.log2(math.e)
Q_SCALE = LOG2_E / math.sqrt(HEAD_DIM)
F32_ZERO_LOG2 = -150.0
SUBLANES = 8
VMEM_LIMIT = 56 * 1024 * 1024


def _params(n_axes):
    return pltpu.CompilerParams(dimension_semantics=("arbitrary",) * n_axes,
                                vmem_limit_bytes=VMEM_LIMIT)


def _rms(x, gain):
    var = jnp.mean(x * x, axis=-1, keepdims=True)
    return (x * lax.rsqrt(var + RMS_EPS)) * gain


def _log_sigmoid(z):
    return jnp.minimum(z, 0.0) - jnp.log1p(jnp.exp(-jnp.abs(z)))


def _inproj_kernel(x_ref, g_ref, w_ref, qkv_ref, lru_ref, hn_ref, *, n_q_tiles, n_qkv_tiles):
    j = pl.program_id(1)

    @pl.when(j == 0)
    def _():
        hn_ref[...] = _rms(x_ref[...], g_ref[...]).astype(BF16)

    acc = jnp.dot(hn_ref[...], w_ref[...], preferred_element_type=F32)

    @pl.when(j < n_qkv_tiles)
    def _():
        qkv_ref[...] = (acc * jnp.where(j < n_q_tiles, Q_SCALE, 1.0)).astype(BF16)

    @pl.when(j >= n_qkv_tiles)
    def _():
        lru_ref[...] = acc


def _inproj(x2, gain, w_bf, qkv_width, tm=1024, tn=1024):
    m, d = x2.shape
    n = w_bf.shape[1]
    lru_width = n - qkv_width
    n_qkv_tiles = qkv_width // tn
    n_q_tiles = qkv_width // (3 * tn)
    grid = (m // tm, n // tn)
    return pl.pallas_call(
        functools.partial(_inproj_kernel, n_q_tiles=n_q_tiles, n_qkv_tiles=n_qkv_tiles),
        grid=grid,
        in_specs=[
            pl.BlockSpec((tm, d), lambda i, j: (i, 0)),
            pl.BlockSpec((1, d), lambda i, j: (0, 0)),
            pl.BlockSpec((d, tn), lambda i, j: (0, j)),
        ],
        out_specs=[
            pl.BlockSpec((tm, tn), lambda i, j: (i, jnp.minimum(j, n_qkv_tiles - 1))),
            pl.BlockSpec((tm, tn), lambda i, j: (i, jnp.maximum(j - n_qkv_tiles, 0))),
        ],
        out_shape=[
            jax.ShapeDtypeStruct((m, qkv_width), BF16),
            jax.ShapeDtypeStruct((m, lru_width), F32),
        ],
        scratch_shapes=[pltpu.VMEM((tm, d), BF16)],
        compiler_params=_params(2),
        name="inproj",
    )(x2, gain, w_bf)


def _attn_kernel(q_ref, k_ref, v_ref, o_ref, tail_ref, *, tq, heads):
    tk = tq
    qi = pl.program_id(2)
    row = lax.broadcasted_iota(jnp.int32, (tk, tk), 0)
    col = lax.broadcasted_iota(jnp.int32, (tk, tk), 1)
    neg_later = jnp.where(row > col, -1.0, 0.0).astype(BF16)

    def tile(kj, first):
        start = pl.multiple_of(kj * tk, tk)
        log_betas = []
        softplus = []
        for h in range(heads):
            sl = slice(h * HEAD_DIM, (h + 1) * HEAD_DIM)
            k_t = k_ref[pl.ds(start, tk), sl]
            z2 = lax.dot_general(q_ref[:, sl], k_t, (((1,), (1,)), ((), ())),
                                 preferred_element_type=F32)
            pos = jnp.maximum(z2, 0.0)
            neg = z2 - pos
            log_term = jnp.log(1.0 + jnp.exp2(neg - pos)) * LOG2_E
            sp = pos + log_term
            if first:
                sp = jnp.where(col < row, sp, 0.0)
            softplus.append(sp)
            log_betas.append(neg - log_term)
        sp_all = jnp.concatenate(softplus, axis=0).astype(BF16)
        suffix_all = jnp.dot(sp_all, neg_later, preferred_element_type=F32)
        for h in range(heads):
            sl = slice(h * HEAD_DIM, (h + 1) * HEAD_DIM)
            v_t = v_ref[pl.ds(start, tk), sl]
            suffix = suffix_all[h * tq:(h + 1) * tq]
            expo = log_betas[h] + suffix
            if not first:
                expo = expo + jnp.concatenate([tail_ref[h]] * (tk // HEAD_DIM), axis=1)
            w = jnp.exp2(expo)
            if first:
                w = jnp.where(col < row, w, 0.0)
            pv = jnp.dot(w.astype(BF16), v_t, preferred_element_type=F32)
            total = jnp.broadcast_to(suffix[:, 0:1] - softplus[h][:, 0:1], (tq, HEAD_DIM))
            if first:
                o_ref[:, sl] = pv
                tail_ref[h] = total
            else:
                o_ref[:, sl] += pv
                tail_ref[h] += total

    def stick_left():
        return jnp.max(tail_ref[...]) > F32_ZERO_LOG2

    tile(qi, True)

    def body(carry):
        it, _ = carry
        tile(qi - 1 - it, False)
        return it + 1, stick_left()

    lax.while_loop(lambda c: jnp.logical_and(c[0] < qi, c[1]), body,
                   (jnp.int32(0), stick_left()))


def _attention(qkv, batch, seq, n_heads, tq=256, heads=4):
    m = qkv.shape[0]
    nq = seq // tq
    ng = n_heads // heads
    width = heads * HEAD_DIM
    return pl.pallas_call(
        functools.partial(_attn_kernel, tq=tq, heads=heads),
        grid=(batch, ng, nq),
        in_specs=[
            pl.BlockSpec((tq, width), lambda b, g, i: (b * nq + i, g)),
            pl.BlockSpec((seq, width), lambda b, g, i: (b, ng + g)),
            pl.BlockSpec((seq, width), lambda b, g, i: (b, 2 * ng + g)),
        ],
        out_specs=pl.BlockSpec((tq, width), lambda b, g, i: (b * nq + i, g)),
        out_shape=jax.ShapeDtypeStruct((m, n_heads * HEAD_DIM), F32),
        scratch_shapes=[pltpu.VMEM((heads, tq, HEAD_DIM), F32)],
        compiler_params=_params(3),
        name="stickbreak_attn",
    )(qkv, qkv, qkv)


def _lru_kernel(x_ref, gate_ref, cw_ref, cb_ref, wr_ref, wi_ref, br_ref, bi_ref, lam_ref,
                gn_ref, o_ref, xbuf, a_buf, b_buf, h_buf, hcarry, *, tt):
    t = pl.program_id(1)
    c = x_ref.shape[1]
    n_blocks = c // LRU_BLOCK

    @pl.when(t == 0)
    def _():
        xbuf[0:SUBLANES, :] = jnp.zeros((SUBLANES, c), F32)
        hcarry[...] = jnp.zeros((SUBLANES, c), F32)

    xbuf[SUBLANES:SUBLANES + tt, :] = x_ref[...]
    xc = cb_ref[...] + cw_ref[0:1, :] * xbuf[pl.ds(SUBLANES - 3, tt), :]
    for k in range(1, CONV_WIDTH):
        xc = xc + cw_ref[k:k + 1, :] * xbuf[pl.ds(SUBLANES - 3 + k, tt), :]
    xbuf[0:SUBLANES, :] = xbuf[tt:tt + SUBLANES, :]

    row_in_vreg = lax.broadcasted_iota(jnp.int32, (tt // SUBLANES, SUBLANES, LRU_BLOCK), 1)
    log_sig_lam = _log_sigmoid(lam_ref[...])
    xcb = xc.astype(BF16)
    for n in range(n_blocks):
        sl = slice(n * LRU_BLOCK, (n + 1) * LRU_BLOCK)
        xn = xc[:, sl]
        r = jax.nn.sigmoid(jnp.dot(xcb[:, sl], wr_ref[n], preferred_element_type=F32)
                           + br_ref[:, sl])
        i = jax.nn.sigmoid(jnp.dot(xcb[:, sl], wi_ref[n], preferred_element_type=F32)
                           + bi_ref[:, sl])
        log_a = LRU_C * r * log_sig_lam[:, sl]
        a = jnp.exp(log_a)
        b = jnp.sqrt(-jnp.tanh(log_a) * (1.0 + a * a)) * (i * xn)
        a = a.reshape(tt // SUBLANES, SUBLANES, LRU_BLOCK)
        b = b.reshape(tt // SUBLANES, SUBLANES, LRU_BLOCK)
        for shift in (1, 2, 4):
            keep = row_in_vreg >= shift
            a_prev = jnp.where(keep, pltpu.roll(a, shift, 1), 1.0)
            b_prev = jnp.where(keep, pltpu.roll(b, shift, 1), 0.0)
            b = a * b_prev + b
            a = a * a_prev
        a_buf[:, sl] = a.reshape(tt, LRU_BLOCK)
        b_buf[:, sl] = b.reshape(tt, LRU_BLOCK)

    def group(g, hb):
        rows = pl.ds(pl.multiple_of(g * SUBLANES, SUBLANES), SUBLANES)
        h = a_buf[rows, :] * hb + b_buf[rows, :]
        h_buf[rows, :] = h
        return jnp.broadcast_to(h[SUBLANES - 1:SUBLANES, :], (SUBLANES, c))

    hcarry[...] = lax.fori_loop(0, tt // SUBLANES, group, hcarry[...])

    y = h_buf[...] * jax.nn.gelu(gate_ref[...], approximate=True)
    o_ref[...] = _rms(y, gn_ref[...]).astype(o_ref.dtype)


def _lru_branch(lru_in, conv_w, conv_b, w_r, w_i, b_r, b_i, lam, gain, batch, seq, tt=256):
    m = lru_in.shape[0]
    c = lru_in.shape[1] // 2
    nt = seq // tt
    n_blocks = c // LRU_BLOCK
    row = lambda b, t: (0, 0)
    return pl.pallas_call(
        functools.partial(_lru_kernel, tt=tt),
        grid=(batch, nt),
        in_specs=[
            pl.BlockSpec((tt, c), lambda b, t: (b * nt + t, 0)),
            pl.BlockSpec((tt, c), lambda b, t: (b * nt + t, 1)),
            pl.BlockSpec((CONV_WIDTH, c), row),
            pl.BlockSpec((1, c), row),
            pl.BlockSpec((n_blocks, LRU_BLOCK, LRU_BLOCK), lambda b, t: (0, 0, 0)),
            pl.BlockSpec((n_blocks, LRU_BLOCK, LRU_BLOCK), lambda b, t: (0, 0, 0)),
            pl.BlockSpec((1, c), row),
            pl.BlockSpec((1, c), row),
            pl.BlockSpec((1, c), row),
            pl.BlockSpec((1, c), row),
        ],
        out_specs=pl.BlockSpec((tt, c), lambda b, t: (b * nt + t, 0)),
        out_shape=jax.ShapeDtypeStruct((m, c), BF16),
        scratch_shapes=[
            pltpu.VMEM((tt + SUBLANES, c), F32),
            pltpu.VMEM((tt, c), F32),
            pltpu.VMEM((tt, c), F32),
            pltpu.VMEM((tt, c), F32),
            pltpu.VMEM((SUBLANES, c), F32),
        ],
        compiler_params=_params(2),
        name="rglru",
    )(lru_in, lru_in, conv_w, conv_b, w_r, w_i, b_r, b_i, lam, gain)


def _outproj_kernel(ya_ref, yl_ref, ga_ref, w_ref, x_ref, gpost_ref, gpre_ref, x1_ref, hn_ref):
    ya = _rms(ya_ref[...], ga_ref[...]).astype(BF16)
    y = jnp.concatenate([ya, yl_ref[...]], axis=1)
    mix = jnp.dot(y, w_ref[...], preferred_element_type=F32)
    x1 = x_ref[...] + _rms(mix, gpost_ref[...])
    x1_ref[...] = x1
    hn_ref[...] = _rms(x1, gpre_ref[...]).astype(BF16)


def _outproj(y_attn, y_lru, attn_gain, w_bf, x2, post_gain, pre_gain, tm=256):
    m, d = x2.shape
    ca = y_attn.shape[1]
    cl = y_lru.shape[1]
    row = lambda i: (0, 0)
    return pl.pallas_call(
        _outproj_kernel,
        grid=(m // tm,),
        in_specs=[
            pl.BlockSpec((tm, ca), lambda i: (i, 0)),
            pl.BlockSpec((tm, cl), lambda i: (i, 0)),
            pl.BlockSpec((1, ca), row),
            pl.BlockSpec((ca + cl, d), row),
            pl.BlockSpec((tm, d), lambda i: (i, 0)),
            pl.BlockSpec((1, d), row),
            pl.BlockSpec((1, d), row),
        ],
        out_specs=[
            pl.BlockSpec((tm, d), lambda i: (i, 0)),
            pl.BlockSpec((tm, d), lambda i: (i, 0)),
        ],
        out_shape=[
            jax.ShapeDtypeStruct((m, d), F32),
            jax.ShapeDtypeStruct((m, d), BF16),
        ],
        compiler_params=_params(1),
        name="outproj",
    )(y_attn, y_lru, attn_gain, w_bf, x2, post_gain, pre_gain)


def _ffn_kernel(hn_ref, wg_ref, wu_ref, wd_ref, x1_ref, gain_ref, o_ref):
    f = pl.program_id(1)

    @pl.when(f == 0)
    def _():
        o_ref[...] = jnp.zeros(o_ref.shape, F32)

    hn = hn_ref[...]
    gate = jnp.dot(hn, wg_ref[...], preferred_element_type=F32)
    up = jnp.dot(hn, wu_ref[...], preferred_element_type=F32)
    act = (jax.nn.silu(gate) * up).astype(BF16)
    o_ref[...] += jnp.dot(act, wd_ref[...], preferred_element_type=F32)

    @pl.when(f == pl.num_programs(1) - 1)
    def _():
        o_ref[...] = x1_ref[...] + _rms(o_ref[...], gain_ref[...])


def _ffn(hn, wg, wu, wd, x1, gain, tm=512, tf=512):
    m, d = hn.shape
    dff = wg.shape[1]
    return pl.pallas_call(
        _ffn_kernel,
        grid=(m // tm, dff // tf),
        in_specs=[
            pl.BlockSpec((tm, d), lambda i, f: (i, 0)),
            pl.BlockSpec((d, tf), lambda i, f: (0, f)),
            pl.BlockSpec((d, tf), lambda i, f: (0, f)),
            pl.BlockSpec((tf, d), lambda i, f: (f, 0)),
            pl.BlockSpec((tm, d), lambda i, f: (i, 0)),
            pl.BlockSpec((1, d), lambda i, f: (0, 0)),
        ],
        out_specs=pl.BlockSpec((tm, d), lambda i, f: (i, 0)),
        out_shape=jax.ShapeDtypeStruct((m, d), F32),
        compiler_params=_params(2),
        name="swiglu_ffn",
    )(hn, wg, wu, wd, x1, gain)


def _layer(x2, batch, seq, pre_mix, post_mix, pre_ffn, post_ffn, w_in, conv_w, conv_b, w_r, b_r,
           w_i, b_i, lam, attn_gain, lru_gain, w_out, w_g, w_u, w_d):
    row = lambda v: v.reshape(1, -1)
    lru_width = conv_w.shape[1]
    qkv_width = w_in.shape[1] - 2 * lru_width
    n_heads = qkv_width // (3 * HEAD_DIM)

    qkv, lru_in = _inproj(x2, row(pre_mix), w_in.astype(BF16), qkv_width)
    y_attn = _attention(qkv, batch, seq, n_heads)
    y_lru = _lru_branch(lru_in, conv_w, row(conv_b), w_r.astype(BF16), w_i.astype(BF16),
                        row(b_r), row(b_i), row(lam), row(lru_gain), batch, seq)
    x1, hn2 = _outproj(y_attn, y_lru, row(attn_gain), w_out.astype(BF16), x2, row(post_mix),
                       row(pre_ffn))
    return _ffn(hn2, w_g.astype(BF16), w_u.astype(BF16), w_d.astype(BF16), x1, row(post_ffn))


def kernel(x, pre_mix_norm, post_mix_norm, pre_ffn_norm, post_ffn_norm, w_in, conv_w, conv_b,
           w_rgate, b_rgate, w_igate, b_igate, lru_lambda, attn_out_norm, lru_out_norm, w_out,
           w_ffn_gate, w_ffn_up, w_ffn_down):
    batch, seq, d = x.shape
    x2 = x.reshape(batch * seq, d)
    for l in range(w_in.shape[0]):
        x2 = _layer(x2, batch, seq, pre_mix_norm[l], post_mix_norm[l], pre_ffn_norm[l],
                    post_ffn_norm[l], w_in[l], conv_w[l], conv_b[l], w_rgate[l], b_rgate[l],
                    w_igate[l], b_igate[l], lru_lambda[l], attn_out_norm[l], lru_out_norm[l],
                    w_out[l], w_ffn_gate[l], w_ffn_up[l], w_ffn_down[l])
    return x2.reshape(batch, seq, d)
```

```python
import functools
import math

import jax
import jax.numpy as jnp
from jax import lax
from jax.experimental import pallas as pl
from jax.experimental.pallas import tpu as pltpu

F32 = jnp.float32
BF16 = jnp.bfloat16

HEAD_DIM = 128
LRU_BLOCK = 128
CONV_WIDTH = 4
LRU_C = 8.0
RMS_EPS = 1e-6
LOG2_E = math.log2(math.e)
Q_SCALE = LOG2_E / math.sqrt(HEAD_DIM)
F32_ZERO_LOG2 = -150.0
SUBLANES = 8
VMEM_LIMIT = 60 * 1024 * 1024


def _params(n_axes):
    return pltpu.CompilerParams(dimension_semantics=("arbitrary",) * n_axes,
                                vmem_limit_bytes=VMEM_LIMIT)


def _rms(x, gain):
    var = jnp.mean(x * x, axis=-1, keepdims=True)
    return (x * lax.rsqrt(var + RMS_EPS)) * gain


def _log_sigmoid(z):
    return jnp.minimum(z, 0.0) - jnp.log1p(jnp.exp(-jnp.abs(z)))


def _inproj_kernel(x_ref, g_ref, w_ref, qkv_ref, lru_ref, hn_ref, *, n_q_tiles, n_qkv_tiles):
    j = pl.program_id(1)

    @pl.when(j == 0)
    def _():
        hn_ref[...] = _rms(x_ref[...], g_ref[...]).astype(BF16)

    acc = jnp.dot(hn_ref[...], w_ref[...].astype(BF16), preferred_element_type=F32)

    @pl.when(j < n_qkv_tiles)
    def _():
        qkv_ref[...] = (acc * jnp.where(j < n_q_tiles, Q_SCALE, 1.0)).astype(BF16)

    @pl.when(j >= n_qkv_tiles)
    def _():
        lru_ref[...] = acc


def _inproj(x2, gain, w, qkv_width, tm=1024, tn=512):
    m, d = x2.shape
    n = w.shape[1]
    lru_width = n - qkv_width
    n_qkv_tiles = qkv_width // tn
    n_q_tiles = qkv_width // (3 * tn)
    grid = (m // tm, n // tn)
    return pl.pallas_call(
        functools.partial(_inproj_kernel, n_q_tiles=n_q_tiles, n_qkv_tiles=n_qkv_tiles),
        grid=grid,
        in_specs=[
            pl.BlockSpec((tm, d), lambda i, j: (i, 0)),
            pl.BlockSpec((1, d), lambda i, j: (0, 0)),
            pl.BlockSpec((d, tn), lambda i, j: (0, j)),
        ],
        out_specs=[
            pl.BlockSpec((tm, tn), lambda i, j: (i, jnp.minimum(j, n_qkv_tiles - 1))),
            pl.BlockSpec((tm, tn), lambda i, j: (i, jnp.maximum(j - n_qkv_tiles, 0))),
        ],
        out_shape=[
            jax.ShapeDtypeStruct((m, qkv_width), BF16),
            jax.ShapeDtypeStruct((m, lru_width), F32),
        ],
        scratch_shapes=[pltpu.VMEM((tm, d), BF16)],
        compiler_params=_params(2),
        name="inproj",
    )(x2, gain, w)


def _attn_kernel(q_ref, k_ref, v_ref, o_ref, tail_ref, *, tq, heads):
    tk = tq
    qi = pl.program_id(2)
    row = lax.broadcasted_iota(jnp.int32, (tk, tk), 0)
    col = lax.broadcasted_iota(jnp.int32, (tk, tk), 1)
    neg_later = jnp.where(row > col, -1.0, 0.0).astype(BF16)

    def tile(kj, first):
        start = pl.multiple_of(kj * tk, tk)
        log_betas = []
        softplus = []
        for h in range(heads):
            sl = slice(h * HEAD_DIM, (h + 1) * HEAD_DIM)
            k_t = k_ref[pl.ds(start, tk), sl]
            z2 = lax.dot_general(q_ref[:, sl], k_t, (((1,), (1,)), ((), ())),
                                 preferred_element_type=F32)
            pos = jnp.maximum(z2, 0.0)
            neg = z2 - pos
            log_term = jnp.log(1.0 + jnp.exp2(neg - pos)) * LOG2_E
            sp = pos + log_term
            if first:
                sp = jnp.where(col < row, sp, 0.0)
            softplus.append(sp)
            log_betas.append(neg - log_term)
        sp_all = jnp.concatenate(softplus, axis=0).astype(BF16)
        suffix_all = jnp.dot(sp_all, neg_later, preferred_element_type=F32)
        for h in range(heads):
            sl = slice(h * HEAD_DIM, (h + 1) * HEAD_DIM)
            v_t = v_ref[pl.ds(start, tk), sl]
            suffix = suffix_all[h * tq:(h + 1) * tq]
            expo = log_betas[h] + suffix
            if not first:
                expo = expo + jnp.concatenate([tail_ref[h]] * (tk // HEAD_DIM), axis=1)
            w = jnp.exp2(expo)
            if first:
                w = jnp.where(col < row, w, 0.0)
            pv = jnp.dot(w.astype(BF16), v_t, preferred_element_type=F32)
            total = jnp.broadcast_to(suffix[:, 0:1] - softplus[h][:, 0:1], (tq, HEAD_DIM))
            if first:
                o_ref[:, sl] = pv
                tail_ref[h] = total
            else:
                o_ref[:, sl] += pv
                tail_ref[h] += total

    def stick_left():
        return jnp.max(tail_ref[...]) > F32_ZERO_LOG2

    tile(qi, True)

    def body(carry):
        it, _ = carry
        tile(qi - 1 - it, False)
        return it + 1, stick_left()

    lax.while_loop(lambda c: jnp.logical_and(c[0] < qi, c[1]), body,
                   (jnp.int32(0), stick_left()))


def _attention(qkv, batch, seq, n_heads, tq=256, heads=4):
    m = qkv.shape[0]
    nq = seq // tq
    ng = n_heads // heads
    width = heads * HEAD_DIM
    return pl.pallas_call(
        functools.partial(_attn_kernel, tq=tq, heads=heads),
        grid=(batch, ng, nq),
        in_specs=[
            pl.BlockSpec((tq, width), lambda b, g, i: (b * nq + i, g)),
            pl.BlockSpec((seq, width), lambda b, g, i: (b, ng + g)),
            pl.BlockSpec((seq, width), lambda b, g, i: (b, 2 * ng + g)),
        ],
        out_specs=pl.BlockSpec((tq, width), lambda b, g, i: (b * nq + i, g)),
        out_shape=jax.ShapeDtypeStruct((m, n_heads * HEAD_DIM), F32),
        scratch_shapes=[pltpu.VMEM((heads, tq, HEAD_DIM), F32)],
        compiler_params=_params(3),
        name="stickbreak_attn",
    )(qkv, qkv, qkv)


def _lru_kernel(x_ref, gate_ref, cw_ref, cb_ref, wr_ref, wi_ref, br_ref, bi_ref, lam_ref,
                gn_ref, o_ref, xbuf, a_buf, b_buf, h_buf, hcarry, *, tt):
    t = pl.program_id(1)
    c = x_ref.shape[1]
    n_blocks = c // LRU_BLOCK

    @pl.when(t == 0)
    def _():
        xbuf[0:SUBLANES, :] = jnp.zeros((SUBLANES, c), F32)
        hcarry[...] = jnp.zeros((SUBLANES, c), F32)

    xbuf[SUBLANES:SUBLANES + tt, :] = x_ref[...]
    xc = cb_ref[...] + cw_ref[0:1, :] * xbuf[pl.ds(SUBLANES - 3, tt), :]
    for k in range(1, CONV_WIDTH):
        xc = xc + cw_ref[k:k + 1, :] * xbuf[pl.ds(SUBLANES - 3 + k, tt), :]
    xbuf[0:SUBLANES, :] = xbuf[tt:tt + SUBLANES, :]

    row_in_vreg = lax.broadcasted_iota(jnp.int32, (tt // SUBLANES, SUBLANES, LRU_BLOCK), 1)
    log_sig_lam = _log_sigmoid(lam_ref[...])
    xcb = xc.astype(BF16)
    for n in range(n_blocks):
        sl = slice(n * LRU_BLOCK, (n + 1) * LRU_BLOCK)
        xn = xc[:, sl]
        r = jax.nn.sigmoid(jnp.dot(xcb[:, sl], wr_ref[n], preferred_element_type=F32)
                           + br_ref[:, sl])
        i = jax.nn.sigmoid(jnp.dot(xcb[:, sl], wi_ref[n], preferred_element_type=F32)
                           + bi_ref[:, sl])
        log_a = LRU_C * r * log_sig_lam[:, sl]
        a = jnp.exp(log_a)
        b = jnp.sqrt(-jnp.tanh(log_a) * (1.0 + a * a)) * (i * xn)
        a = a.reshape(tt // SUBLANES, SUBLANES, LRU_BLOCK)
        b = b.reshape(tt // SUBLANES, SUBLANES, LRU_BLOCK)
        for shift in (1, 2, 4):
            keep = row_in_vreg >= shift
            a_prev = jnp.where(keep, pltpu.roll(a, shift, 1), 1.0)
            b_prev = jnp.where(keep, pltpu.roll(b, shift, 1), 0.0)
            b = a * b_prev + b
            a = a * a_prev
        a_buf[:, sl] = a.reshape(tt, LRU_BLOCK)
        b_buf[:, sl] = b.reshape(tt, LRU_BLOCK)

    def group(g, hb):
        rows = pl.ds(pl.multiple_of(g * SUBLANES, SUBLANES), SUBLANES)
        h = a_buf[rows, :] * hb + b_buf[rows, :]
        h_buf[rows, :] = h
        return jnp.broadcast_to(h[SUBLANES - 1:SUBLANES, :], (SUBLANES, c))

    hcarry[...] = lax.fori_loop(0, tt // SUBLANES, group, hcarry[...])

    y = h_buf[...] * jax.nn.gelu(gate_ref[...], approximate=True)
    o_ref[...] = _rms(y, gn_ref[...]).astype(o_ref.dtype)


def _lru_branch(lru_in, conv_w, conv_b, w_r, w_i, b_r, b_i, lam, gain, batch, seq, tt=256):
    m = lru_in.shape[0]
    c = lru_in.shape[1] // 2
    nt = seq // tt
    n_blocks = c // LRU_BLOCK
    row = lambda b, t: (0, 0)
    return pl.pallas_call(
        functools.partial(_lru_kernel, tt=tt),
        grid=(batch, nt),
        in_specs=[
            pl.BlockSpec((tt, c), lambda b, t: (b * nt + t, 0)),
            pl.BlockSpec((tt, c), lambda b, t: (b * nt + t, 1)),
            pl.BlockSpec((CONV_WIDTH, c), row),
            pl.BlockSpec((1, c), row),
            pl.BlockSpec((n_blocks, LRU_BLOCK, LRU_BLOCK), lambda b, t: (0, 0, 0)),
            pl.BlockSpec((n_blocks, LRU_BLOCK, LRU_BLOCK), lambda b, t: (0, 0, 0)),
            pl.BlockSpec((1, c), row),
            pl.BlockSpec((1, c), row),
            pl.BlockSpec((1, c), row),
            pl.BlockSpec((1, c), row),
        ],
        out_specs=pl.BlockSpec((tt, c), lambda b, t: (b * nt + t, 0)),
        out_shape=jax.ShapeDtypeStruct((m, c), BF16),
        scratch_shapes=[
            pltpu.VMEM((tt + SUBLANES, c), F32),
            pltpu.VMEM((tt, c), F32),
            pltpu.VMEM((tt, c), F32),
            pltpu.VMEM((tt, c), F32),
            pltpu.VMEM((SUBLANES, c), F32),
        ],
        compiler_params=_params(2),
        name="rglru",
    )(lru_in, lru_in, conv_w, conv_b, w_r, w_i, b_r, b_i, lam, gain)


def _outproj_kernel(ya_ref, yl_ref, ga_ref, w_ref, x_ref, gpost_ref, x1_ref):
    ya = _rms(ya_ref[...], ga_ref[...]).astype(BF16)
    y = jnp.concatenate([ya, yl_ref[...]], axis=1)
    mix = jnp.dot(y, w_ref[...], preferred_element_type=F32)
    x1_ref[...] = x_ref[...] + _rms(mix, gpost_ref[...])


def _outproj(y_attn, y_lru, attn_gain, w_bf, x2, post_gain, tm=256):
    m, d = x2.shape
    ca = y_attn.shape[1]
    cl = y_lru.shape[1]
    row = lambda i: (0, 0)
    return pl.pallas_call(
        _outproj_kernel,
        grid=(m // tm,),
        in_specs=[
            pl.BlockSpec((tm, ca), lambda i: (i, 0)),
            pl.BlockSpec((tm, cl), lambda i: (i, 0)),
            pl.BlockSpec((1, ca), row),
            pl.BlockSpec((ca + cl, d), row),
            pl.BlockSpec((tm, d), lambda i: (i, 0)),
            pl.BlockSpec((1, d), row),
        ],
        out_specs=pl.BlockSpec((tm, d), lambda i: (i, 0)),
        out_shape=jax.ShapeDtypeStruct((m, d), F32),
        compiler_params=_params(1),
        name="outproj",
    )(y_attn, y_lru, attn_gain, w_bf, x2, post_gain)


def _ffn_kernel(x1_ref, gpre_ref, wg_ref, wu_ref, wd_ref, gpost_ref, o_ref, hn_ref):
    f = pl.program_id(1)

    @pl.when(f == 0)
    def _():
        hn_ref[...] = _rms(x1_ref[...], gpre_ref[...]).astype(BF16)
        o_ref[...] = jnp.zeros(o_ref.shape, F32)

    hn = hn_ref[...]
    gate = jnp.dot(hn, wg_ref[...].astype(BF16), preferred_element_type=F32)
    up = jnp.dot(hn, wu_ref[...].astype(BF16), preferred_element_type=F32)
    act = (jax.nn.silu(gate) * up).astype(BF16)
    o_ref[...] += jnp.dot(act, wd_ref[...].astype(BF16), preferred_element_type=F32)

    @pl.when(f == pl.num_programs(1) - 1)
    def _():
        o_ref[...] = x1_ref[...] + _rms(o_ref[...], gpost_ref[...])


def _ffn(x1, pre_gain, wg, wu, wd, post_gain, tm=1024, tf=256):
    m, d = x1.shape
    dff = wg.shape[1]
    return pl.pallas_call(
        _ffn_kernel,
        grid=(m // tm, dff // tf),
        in_specs=[
            pl.BlockSpec((tm, d), lambda i, f: (i, 0)),
            pl.BlockSpec((1, d), lambda i, f: (0, 0)),
            pl.BlockSpec((d, tf), lambda i, f: (0, f)),
            pl.BlockSpec((d, tf), lambda i, f: (0, f)),
            pl.BlockSpec((tf, d), lambda i, f: (f, 0)),
            pl.BlockSpec((1, d), lambda i, f: (0, 0)),
        ],
        out_specs=pl.BlockSpec((tm, d), lambda i, f: (i, 0)),
        out_shape=jax.ShapeDtypeStruct((m, d), F32),
        scratch_shapes=[pltpu.VMEM((tm, d), BF16)],
        compiler_params=_params(2),
        name="swiglu_ffn",
    )(x1, pre_gain, wg, wu, wd, post_gain)


def _layer(x2, batch, seq, pre_mix, post_mix, pre_ffn, post_ffn, w_in, conv_w, conv_b, w_r, b_r,
           w_i, b_i, lam, attn_gain, lru_gain, w_out, w_g, w_u, w_d):
    row = lambda v: v.reshape(1, -1)
    lru_width = conv_w.shape[1]
    qkv_width = w_in.shape[1] - 2 * lru_width
    n_heads = qkv_width // (3 * HEAD_DIM)

    qkv, lru_in = _inproj(x2, row(pre_mix), w_in, qkv_width)
    y_attn = _attention(qkv, batch, seq, n_heads)
    y_lru = _lru_branch(lru_in, conv_w, row(conv_b), w_r.astype(BF16), w_i.astype(BF16),
                        row(b_r), row(b_i), row(lam), row(lru_gain), batch, seq)
    x1 = _outproj(y_attn, y_lru, row(attn_gain), w_out.astype(BF16), x2, row(post_mix))
    return _ffn(x1, row(pre_ffn), w_g, w_u, w_d, row(post_ffn))


def kernel(x, pre_mix_norm, post_mix_norm, pre_ffn_norm, post_ffn_norm, w_in, conv_w, conv_b,
           w_rgate, b_rgate, w_igate, b_igate, lru_lambda, attn_out_norm, lru_out_norm, w_out,
           w_ffn_gate, w_ffn_up, w_ffn_down):
    batch, seq, d = x.shape
    x2 = x.reshape(batch * seq, d)
    for l in range(w_in.shape[0]):
        x2 = _layer(x2, batch, seq, pre_mix_norm[l], post_mix_norm[l], pre_ffn_norm[l],
                    post_ffn_norm[l], w_in[l], conv_w[l], conv_b[l], w_rgate[l], b_rgate[l],
                    w_igate[l], b_igate[l], lru_lambda[l], attn_out_norm[l], lru_out_norm[l],
                    w_out[l], w_ffn_gate[l], w_ffn_up[l], w_ffn_down[l])
    return x2.reshape(batch, seq, d)
```

```python
import functools
import math

import jax
import jax.numpy as jnp
from jax import lax
from jax.experimental import pallas as pl
from jax.experimental.pallas import tpu as pltpu

F32 = jnp.float32
BF16 = jnp.bfloat16

HEAD_DIM = 128
LRU_BLOCK = 128
CONV_WIDTH = 4
LRU_C = 8.0
RMS_EPS = 1e-6
LOG2_E = math.log2(math.e)
Q_SCALE = LOG2_E / math.sqrt(HEAD_DIM)
F32_TINY = float(jnp.finfo(jnp.float32).tiny)
GELU_C1 = math.sqrt(2.0 / math.pi)
GELU_C3 = GELU_C1 * 0.044715
F32_ZERO_LOG2 = -150.0
SUBLANES = 8
VMEM_LIMIT = 60 * 1024 * 1024


def _params(n_axes):
    return pltpu.CompilerParams(dimension_semantics=("arbitrary",) * n_axes,
                                vmem_limit_bytes=VMEM_LIMIT)


def _rms(x, gain):
    var = jnp.mean(x * x, axis=-1, keepdims=True)
    return (x * lax.rsqrt(var + RMS_EPS)) * gain


def _log_sigmoid(z):
    return jnp.minimum(z, 0.0) - jnp.log1p(jnp.exp(-jnp.abs(z)))


def _inproj_kernel(x_ref, g_ref, w_ref, qkv_ref, lru_ref, hn_ref, *, n_q_tiles, n_qkv_tiles):
    j = pl.program_id(1)

    @pl.when(j == 0)
    def _():
        hn_ref[...] = _rms(x_ref[...], g_ref[...]).astype(BF16)

    acc = jnp.dot(hn_ref[...], w_ref[...].astype(BF16), preferred_element_type=F32)

    @pl.when(j < n_qkv_tiles)
    def _():
        qkv_ref[...] = (acc * jnp.where(j < n_q_tiles, Q_SCALE, 1.0)).astype(BF16)

    @pl.when(j >= n_qkv_tiles)
    def _():
        lru_ref[...] = acc


def _inproj(x2, gain, w, qkv_width, tm=1024, tn=1024):
    m, d = x2.shape
    n = w.shape[1]
    lru_width = n - qkv_width
    n_qkv_tiles = qkv_width // tn
    n_q_tiles = qkv_width // (3 * tn)
    grid = (m // tm, n // tn)
    return pl.pallas_call(
        functools.partial(_inproj_kernel, n_q_tiles=n_q_tiles, n_qkv_tiles=n_qkv_tiles),
        grid=grid,
        in_specs=[
            pl.BlockSpec((tm, d), lambda i, j: (i, 0)),
            pl.BlockSpec((1, d), lambda i, j: (0, 0)),
            pl.BlockSpec((d, tn), lambda i, j: (0, j)),
        ],
        out_specs=[
            pl.BlockSpec((tm, tn), lambda i, j: (i, jnp.minimum(j, n_qkv_tiles - 1))),
            pl.BlockSpec((tm, tn), lambda i, j: (i, jnp.maximum(j - n_qkv_tiles, 0))),
        ],
        out_shape=[
            jax.ShapeDtypeStruct((m, qkv_width), BF16),
            jax.ShapeDtypeStruct((m, lru_width), F32),
        ],
        scratch_shapes=[pltpu.VMEM((tm, d), BF16)],
        compiler_params=_params(2),
        name="inproj",
    )(x2, gain, w)


def _attn_kernel(q_ref, k_ref, v_ref, o_ref, tail_ref, *, tq, heads):
    tk = tq
    qi = pl.program_id(2)
    row = lax.broadcasted_iota(jnp.int32, (tk, tk), 0)
    col = lax.broadcasted_iota(jnp.int32, (tk, tk), 1)
    neg_later = jnp.where(row > col, -1.0, 0.0).astype(BF16)

    def tile(kj, first):
        start = pl.multiple_of(kj * tk, tk)
        log_betas = []
        softplus = []
        for h in range(heads):
            sl = slice(h * HEAD_DIM, (h + 1) * HEAD_DIM)
            k_t = k_ref[pl.ds(start, tk), sl]
            z2 = lax.dot_general(q_ref[:, sl], k_t, (((1,), (1,)), ((), ())),
                                 preferred_element_type=F32)
            pos = jnp.maximum(z2, 0.0)
            neg = z2 - pos
            log_term = jnp.log(1.0 + jnp.exp2(neg - pos)) * LOG2_E
            sp = pos + log_term
            if first:
                sp = jnp.where(col < row, sp, 0.0)
            softplus.append(sp)
            log_betas.append(neg - log_term)
        sp_all = jnp.concatenate(softplus, axis=0).astype(BF16)
        suffix_all = jnp.dot(sp_all, neg_later, preferred_element_type=F32)
        for h in range(heads):
            sl = slice(h * HEAD_DIM, (h + 1) * HEAD_DIM)
            v_t = v_ref[pl.ds(start, tk), sl]
            suffix = suffix_all[h * tq:(h + 1) * tq]
            expo = log_betas[h] + suffix
            if not first:
                expo = expo + jnp.concatenate([tail_ref[h]] * (tk // HEAD_DIM), axis=1)
            w = jnp.exp2(expo)
            if first:
                w = jnp.where(col < row, w, 0.0)
            pv = jnp.dot(w.astype(BF16), v_t, preferred_element_type=F32)
            total = jnp.broadcast_to(suffix[:, 0:1] - softplus[h][:, 0:1], (tq, HEAD_DIM))
            if first:
                o_ref[:, sl] = pv
                tail_ref[h] = total
            else:
                o_ref[:, sl] += pv
                tail_ref[h] += total

    def stick_left():
        return jnp.max(tail_ref[...]) > F32_ZERO_LOG2

    tile(qi, True)

    def body(carry):
        it, _ = carry
        tile(qi - 1 - it, False)
        return it + 1, stick_left()

    lax.while_loop(lambda c: jnp.logical_and(c[0] < qi, c[1]), body,
                   (jnp.int32(0), stick_left()))


def _attention(qkv, batch, seq, n_heads, tq=256, heads=4):
    m = qkv.shape[0]
    nq = seq // tq
    ng = n_heads // heads
    width = heads * HEAD_DIM
    return pl.pallas_call(
        functools.partial(_attn_kernel, tq=tq, heads=heads),
        grid=(batch, ng, nq),
        in_specs=[
            pl.BlockSpec((tq, width), lambda b, g, i: (b * nq + i, g)),
            pl.BlockSpec((seq, width), lambda b, g, i: (b, ng + g)),
            pl.BlockSpec((seq, width), lambda b, g, i: (b, 2 * ng + g)),
        ],
        out_specs=pl.BlockSpec((tq, width), lambda b, g, i: (b * nq + i, g)),
        out_shape=jax.ShapeDtypeStruct((m, n_heads * HEAD_DIM), F32),
        scratch_shapes=[pltpu.VMEM((heads, tq, HEAD_DIM), F32)],
        compiler_params=_params(3),
        name="stickbreak_attn",
    )(qkv, qkv, qkv)


def _lru_kernel(x_ref, gate_ref, cw_ref, cb_ref, wr_ref, wi_ref, br_ref, bi_ref, lam_ref,
                gn_ref, o_ref, xbuf, a_buf, b_buf, h_buf, hcarry, *, tt):
    t = pl.program_id(1)
    c = x_ref.shape[1]
    n_groups = tt // SUBLANES
    grouped = (n_groups, SUBLANES, LRU_BLOCK)

    @pl.when(t == 0)
    def _():
        xbuf[0:SUBLANES, :] = jnp.zeros((SUBLANES, c), F32)
        hcarry[...] = jnp.zeros((SUBLANES, c), F32)

    xbuf[SUBLANES:SUBLANES + tt, :] = x_ref[...]

    row_in_vreg = lax.broadcasted_iota(jnp.int32, grouped, 1)
    log_sig_lam = _log_sigmoid(lam_ref[...])
    log2_a_coef = (0.5 * LRU_C * LOG2_E) * log_sig_lam
    neg_log_a_coef = (-0.5 * LRU_C) * log_sig_lam
    for n in range(c // LRU_BLOCK):
        sl = slice(n * LRU_BLOCK, (n + 1) * LRU_BLOCK)
        lane_row = lambda ref, k=0: ref[k:k + 1, sl].reshape(1, 1, LRU_BLOCK)
        window = xbuf[:, sl].reshape(n_groups + 1, SUBLANES, LRU_BLOCK)
        xc = lane_row(cb_ref) + lane_row(cw_ref, CONV_WIDTH - 1) * window[1:]
        for back in range(1, CONV_WIDTH):
            rolled = pltpu.roll(window, back, 1)
            shifted = jnp.where(row_in_vreg >= back, rolled[1:], rolled[:-1])
            xc = xc + lane_row(cw_ref, CONV_WIDTH - 1 - back) * shifted
        xc = xc.reshape(tt, LRU_BLOCK)
        xcb = xc.astype(BF16)
        tanh_r = jnp.tanh(jnp.dot(xcb, wr_ref[n], preferred_element_type=F32) + br_ref[:, sl])
        tanh_i = jnp.tanh(jnp.dot(xcb, wi_ref[n], preferred_element_type=F32) + bi_ref[:, sl])
        two_r = tanh_r + 1.0
        a = jnp.exp2(two_r * log2_a_coef[:, sl])
        one_m_a2 = jnp.tanh(two_r * neg_log_a_coef[:, sl]) * (1.0 + a * a)
        mult = one_m_a2 * lax.rsqrt(jnp.maximum(one_m_a2, F32_TINY))
        b = mult * ((tanh_i + 1.0) * (0.5 * xc))
        a = a.reshape(grouped)
        b = b.reshape(grouped)
        for shift in (1, 2, 4):
            keep = row_in_vreg >= shift
            a_prev = jnp.where(keep, pltpu.roll(a, shift, 1), 1.0)
            b_prev = jnp.where(keep, pltpu.roll(b, shift, 1), 0.0)
            b = a * b_prev + b
            a = a * a_prev
        a_buf[:, sl] = a.reshape(tt, LRU_BLOCK)
        b_buf[:, sl] = b.reshape(tt, LRU_BLOCK)

    xbuf[0:SUBLANES, :] = xbuf[tt:tt + SUBLANES, :]

    def group(g, hb):
        rows = pl.ds(pl.multiple_of(g * SUBLANES, SUBLANES), SUBLANES)
        h = a_buf[rows, :] * hb + b_buf[rows, :]
        h_buf[rows, :] = h
        return jnp.broadcast_to(h[SUBLANES - 1:SUBLANES, :], (SUBLANES, c))

    hcarry[...] = lax.fori_loop(0, n_groups, group, hcarry[...])

    g = gate_ref[...]
    inner = g * (GELU_C1 + GELU_C3 * (g * g))
    y = (h_buf[...] * (0.5 * g)) * (1.0 + jnp.tanh(inner))
    o_ref[...] = _rms(y, gn_ref[...]).astype(o_ref.dtype)


def _lru_branch(lru_in, conv_w, conv_b, w_r, w_i, b_r, b_i, lam, gain, batch, seq, tt=256):
    m = lru_in.shape[0]
    c = lru_in.shape[1] // 2
    nt = seq // tt
    n_blocks = c // LRU_BLOCK
    row = lambda b, t: (0, 0)
    return pl.pallas_call(
        functools.partial(_lru_kernel, tt=tt),
        grid=(batch, nt),
        in_specs=[
            pl.BlockSpec((tt, c), lambda b, t: (b * nt + t, 0)),
            pl.BlockSpec((tt, c), lambda b, t: (b * nt + t, 1)),
            pl.BlockSpec((CONV_WIDTH, c), row),
            pl.BlockSpec((1, c), row),
            pl.BlockSpec((n_blocks, LRU_BLOCK, LRU_BLOCK), lambda b, t: (0, 0, 0)),
            pl.BlockSpec((n_blocks, LRU_BLOCK, LRU_BLOCK), lambda b, t: (0, 0, 0)),
            pl.BlockSpec((1, c), row),
            pl.BlockSpec((1, c), row),
            pl.BlockSpec((1, c), row),
            pl.BlockSpec((1, c), row),
        ],
        out_specs=pl.BlockSpec((tt, c), lambda b, t: (b * nt + t, 0)),
        out_shape=jax.ShapeDtypeStruct((m, c), BF16),
        scratch_shapes=[
            pltpu.VMEM((tt + SUBLANES, c), F32),
            pltpu.VMEM((tt, c), F32),
            pltpu.VMEM((tt, c), F32),
            pltpu.VMEM((tt, c), F32),
            pltpu.VMEM((SUBLANES, c), F32),
        ],
        compiler_params=_params(2),
        name="rglru",
    )(lru_in, lru_in, conv_w, conv_b, w_r, w_i, b_r, b_i, lam, gain)


def _outproj_kernel(ya_ref, yl_ref, ga_ref, w_ref, x_ref, gpost_ref, x1_ref, wbf_ref):
    @pl.when(pl.program_id(0) == 0)
    def _():
        wbf_ref[...] = w_ref[...].astype(BF16)

    ya = _rms(ya_ref[...], ga_ref[...]).astype(BF16)
    y = jnp.concatenate([ya, yl_ref[...]], axis=1)
    mix = jnp.dot(y, wbf_ref[...], preferred_element_type=F32)
    x1_ref[...] = x_ref[...] + _rms(mix, gpost_ref[...])


def _outproj(y_attn, y_lru, attn_gain, w, x2, post_gain, tm=512):
    m, d = x2.shape
    ca = y_attn.shape[1]
    cl = y_lru.shape[1]
    row = lambda i: (0, 0)
    return pl.pallas_call(
        _outproj_kernel,
        grid=(m // tm,),
        in_specs=[
            pl.BlockSpec((tm, ca), lambda i: (i, 0)),
            pl.BlockSpec((tm, cl), lambda i: (i, 0)),
            pl.BlockSpec((1, ca), row),
            pl.BlockSpec((ca + cl, d), row, pipeline_mode=pl.Buffered(1)),
            pl.BlockSpec((tm, d), lambda i: (i, 0)),
            pl.BlockSpec((1, d), row),
        ],
        out_specs=pl.BlockSpec((tm, d), lambda i: (i, 0)),
        out_shape=jax.ShapeDtypeStruct((m, d), F32),
        scratch_shapes=[pltpu.VMEM((ca + cl, d), BF16)],
        compiler_params=_params(1),
        name="outproj",
    )(y_attn, y_lru, attn_gain, w, x2, post_gain)


def _ffn_kernel(x1_ref, gpre_ref, wg_ref, wu_ref, wd_ref, gpost_ref, o_ref, hn_ref):
    f = pl.program_id(1)

    @pl.when(f == 0)
    def _():
        hn_ref[...] = _rms(x1_ref[...], gpre_ref[...]).astype(BF16)
        o_ref[...] = jnp.zeros(o_ref.shape, F32)

    hn = hn_ref[...]
    gate = jnp.dot(hn, wg_ref[...].astype(BF16), preferred_element_type=F32)
    up = jnp.dot(hn, wu_ref[...].astype(BF16), preferred_element_type=F32)
    act = (jax.nn.silu(gate) * up).astype(BF16)
    o_ref[...] += jnp.dot(act, wd_ref[...].astype(BF16), preferred_element_type=F32)

    @pl.when(f == pl.num_programs(1) - 1)
    def _():
        o_ref[...] = x1_ref[...] + _rms(o_ref[...], gpost_ref[...])


def _ffn(x1, pre_gain, wg, wu, wd, post_gain, tm=1024, tf=256):
    m, d = x1.shape
    dff = wg.shape[1]
    return pl.pallas_call(
        _ffn_kernel,
        grid=(m // tm, dff // tf),
        in_specs=[
            pl.BlockSpec((tm, d), lambda i, f: (i, 0)),
            pl.BlockSpec((1, d), lambda i, f: (0, 0)),
            pl.BlockSpec((d, tf), lambda i, f: (0, f)),
            pl.BlockSpec((d, tf), lambda i, f: (0, f)),
            pl.BlockSpec((tf, d), lambda i, f: (f, 0)),
            pl.BlockSpec((1, d), lambda i, f: (0, 0)),
        ],
        out_specs=pl.BlockSpec((tm, d), lambda i, f: (i, 0)),
        out_shape=jax.ShapeDtypeStruct((m, d), F32),
        scratch_shapes=[pltpu.VMEM((tm, d), BF16)],
        compiler_params=_params(2),
        name="swiglu_ffn",
    )(x1, pre_gain, wg, wu, wd, post_gain)


def _layer(x2, batch, seq, pre_mix, post_mix, pre_ffn, post_ffn, w_in, conv_w, conv_b, w_r, b_r,
           w_i, b_i, lam, attn_gain, lru_gain, w_out, w_g, w_u, w_d):
    row = lambda v: v.reshape(1, -1)
    lru_width = conv_w.shape[1]
    qkv_width = w_in.shape[1] - 2 * lru_width
    n_heads = qkv_width // (3 * HEAD_DIM)

    qkv, lru_in = _inproj(x2, row(pre_mix), w_in, qkv_width)
    y_attn = _attention(qkv, batch, seq, n_heads)
    y_lru = _lru_branch(lru_in, conv_w, row(conv_b), (0.5 * w_r).astype(BF16),
                        (0.5 * w_i).astype(BF16), row(0.5 * b_r), row(0.5 * b_i), row(lam),
                        row(lru_gain), batch, seq)
    x1 = _outproj(y_attn, y_lru, row(attn_gain), w_out, x2, row(post_mix))
    return _ffn(x1, row(pre_ffn), w_g, w_u, w_d, row(post_ffn))


def kernel(x, pre_mix_norm, post_mix_norm, pre_ffn_norm, post_ffn_norm, w_in, conv_w, conv_b,
           w_rgate, b_rgate, w_igate, b_igate, lru_lambda, attn_out_norm, lru_out_norm, w_out,
           w_ffn_gate, w_ffn_up, w_ffn_down):
    batch, seq, d = x.shape
    x2 = x.reshape(batch * seq, d)
    for l in range(w_in.shape[0]):
        x2 = _layer(x2, batch, seq, pre_mix_norm[l], post_mix_norm[l], pre_ffn_norm[l],
                    post_ffn_norm[l], w_in[l], conv_w[l], conv_b[l], w_rgate[l], b_rgate[l],
                    w_igate[l], b_igate[l], lru_lambda[l], attn_out_norm[l], lru_out_norm[l],
                    w_out[l], w_ffn_gate[l], w_ffn_up[l], w_ffn_down[l])
    return x2.reshape(batch, seq, d)
```

```python
import functools
import math

import jax
import jax.numpy as jnp
from jax import lax
from jax.experimental import pallas as pl
from jax.experimental.pallas import tpu as pltpu

F32 = jnp.float32
BF16 = jnp.bfloat16

HEAD_DIM = 128
LRU_BLOCK = 128
CONV_WIDTH = 4
LRU_C = 8.0
RMS_EPS = 1e-6
LOG2_E = math.log2(math.e)
Q_SCALE = LOG2_E / math.sqrt(HEAD_DIM)
F32_TINY = float(jnp.finfo(jnp.float32).tiny)
GELU_C1 = math.sqrt(2.0 / math.pi)
GELU_C3 = GELU_C1 * 0.044715
F32_ZERO_LOG2 = -150.0
SUBLANES = 8
VMEM_LIMIT = 60 * 1024 * 1024


def _params(n_axes):
    return pltpu.CompilerParams(dimension_semantics=("arbitrary",) * n_axes,
                                vmem_limit_bytes=VMEM_LIMIT)


def _rms(x, gain):
    var = jnp.mean(x * x, axis=-1, keepdims=True)
    return (x * lax.rsqrt(var + RMS_EPS)) * gain


def _log_sigmoid(z):
    return jnp.minimum(z, 0.0) - jnp.log1p(jnp.exp(-jnp.abs(z)))


def _inproj_kernel(x_ref, g_ref, w_ref, qkv_ref, lru_ref, hn_ref, *, n_q_tiles, n_qkv_tiles):
    j = pl.program_id(1)

    @pl.when(j == 0)
    def _():
        hn_ref[...] = _rms(x_ref[...], g_ref[...]).astype(BF16)

    acc = jnp.dot(hn_ref[...], w_ref[...].astype(BF16), preferred_element_type=F32)

    @pl.when(j < n_qkv_tiles)
    def _():
        qkv_ref[...] = (acc * jnp.where(j < n_q_tiles, Q_SCALE, 1.0)).astype(BF16)

    @pl.when(j >= n_qkv_tiles)
    def _():
        lru_ref[...] = acc


def _inproj(x2, gain, w, qkv_width, tm=1024, tn=1024):
    m, d = x2.shape
    n = w.shape[1]
    lru_width = n - qkv_width
    n_qkv_tiles = qkv_width // tn
    n_q_tiles = qkv_width // (3 * tn)
    grid = (m // tm, n // tn)
    return pl.pallas_call(
        functools.partial(_inproj_kernel, n_q_tiles=n_q_tiles, n_qkv_tiles=n_qkv_tiles),
        grid=grid,
        in_specs=[
            pl.BlockSpec((tm, d), lambda i, j: (i, 0)),
            pl.BlockSpec((1, d), lambda i, j: (0, 0)),
            pl.BlockSpec((d, tn), lambda i, j: (0, j)),
        ],
        out_specs=[
            pl.BlockSpec((tm, tn), lambda i, j: (i, jnp.minimum(j, n_qkv_tiles - 1))),
            pl.BlockSpec((tm, tn), lambda i, j: (i, jnp.maximum(j - n_qkv_tiles, 0))),
        ],
        out_shape=[
            jax.ShapeDtypeStruct((m, qkv_width), BF16),
            jax.ShapeDtypeStruct((m, lru_width), F32),
        ],
        scratch_shapes=[pltpu.VMEM((tm, d), BF16)],
        compiler_params=_params(2),
        name="inproj",
    )(x2, gain, w)


def _attn_kernel(q_ref, k_ref, v_ref, o_ref, tail_ref, *, tq, heads):
    tk = tq
    qi = pl.program_id(2)
    row = lax.broadcasted_iota(jnp.int32, (tk, tk), 0)
    col = lax.broadcasted_iota(jnp.int32, (tk, tk), 1)
    neg_later = jnp.where(row > col, -1.0, 0.0).astype(BF16)

    def tile(kj, first):
        start = pl.multiple_of(kj * tk, tk)
        log_betas = []
        softplus = []
        for h in range(heads):
            sl = slice(h * HEAD_DIM, (h + 1) * HEAD_DIM)
            k_t = k_ref[pl.ds(start, tk), sl]
            z2 = lax.dot_general(q_ref[:, sl], k_t, (((1,), (1,)), ((), ())),
                                 preferred_element_type=F32)
            pos = jnp.maximum(z2, 0.0)
            neg = z2 - pos
            log_term = jnp.log(1.0 + jnp.exp2(neg - pos)) * LOG2_E
            sp = pos + log_term
            if first:
                sp = jnp.where(col < row, sp, 0.0)
            softplus.append(sp)
            log_betas.append(neg - log_term)
        sp_all = jnp.concatenate(softplus, axis=0).astype(BF16)
        suffix_all = jnp.dot(sp_all, neg_later, preferred_element_type=F32)
        for h in range(heads):
            sl = slice(h * HEAD_DIM, (h + 1) * HEAD_DIM)
            v_t = v_ref[pl.ds(start, tk), sl]
            suffix = suffix_all[h * tq:(h + 1) * tq]
            expo = log_betas[h] + suffix
            if not first:
                expo = expo + jnp.concatenate([tail_ref[h]] * (tk // HEAD_DIM), axis=1)
            w = jnp.exp2(expo)
            if first:
                w = jnp.where(col < row, w, 0.0)
            pv = jnp.dot(w.astype(BF16), v_t, preferred_element_type=F32)
            total = jnp.broadcast_to(suffix[:, 0:1] - softplus[h][:, 0:1], (tq, HEAD_DIM))
            if first:
                o_ref[:, sl] = pv
                tail_ref[h] = total
            else:
                o_ref[:, sl] += pv
                tail_ref[h] += total

    def stick_left():
        return jnp.max(tail_ref[...]) > F32_ZERO_LOG2

    @pl.when(qi == 0)
    def _():
        tile(qi, True)

    @pl.when(qi > 0)
    def _():
        tile(qi, True)
        tile(qi - 1, False)

    def body(carry):
        it, _ = carry
        tile(qi - 1 - it, False)
        return it + 1, stick_left()

    lax.while_loop(lambda c: jnp.logical_and(c[0] < qi, c[1]), body,
                   (jnp.int32(1), stick_left()))


def _attention(qkv, batch, seq, n_heads, tq=256, heads=8):
    m = qkv.shape[0]
    nq = seq // tq
    ng = n_heads // heads
    width = heads * HEAD_DIM
    return pl.pallas_call(
        functools.partial(_attn_kernel, tq=tq, heads=heads),
        grid=(batch, ng, nq),
        in_specs=[
            pl.BlockSpec((tq, width), lambda b, g, i: (b * nq + i, g)),
            pl.BlockSpec((seq, width), lambda b, g, i: (b, ng + g)),
            pl.BlockSpec((seq, width), lambda b, g, i: (b, 2 * ng + g)),
        ],
        out_specs=pl.BlockSpec((tq, width), lambda b, g, i: (b * nq + i, g)),
        out_shape=jax.ShapeDtypeStruct((m, n_heads * HEAD_DIM), F32),
        scratch_shapes=[pltpu.VMEM((heads, tq, HEAD_DIM), F32)],
        compiler_params=_params(3),
        name="stickbreak_attn",
    )(qkv, qkv, qkv)


def _lru_kernel(x_ref, gate_ref, cw_ref, cb_ref, wr_ref, wi_ref, br_ref, bi_ref, lam_ref,
                gn_ref, o_ref, xbuf, a_buf, b_buf, h_buf, hcarry, *, tt):
    t = pl.program_id(1)
    c = x_ref.shape[1]
    n_groups = tt // SUBLANES
    grouped = (n_groups, SUBLANES, LRU_BLOCK)

    @pl.when(t == 0)
    def _():
        xbuf[0:SUBLANES, :] = jnp.zeros((SUBLANES, c), F32)
        hcarry[...] = jnp.zeros((SUBLANES, c), F32)

    xbuf[SUBLANES:SUBLANES + tt, :] = x_ref[...]

    row_in_vreg = lax.broadcasted_iota(jnp.int32, grouped, 1)
    log_sig_lam = _log_sigmoid(lam_ref[...])
    log2_a_coef = (0.5 * LRU_C * LOG2_E) * log_sig_lam
    neg_log_a_coef = (-0.5 * LRU_C) * log_sig_lam
    for n in range(c // LRU_BLOCK):
        sl = slice(n * LRU_BLOCK, (n + 1) * LRU_BLOCK)
        lane_row = lambda ref, k=0: ref[k:k + 1, sl].reshape(1, 1, LRU_BLOCK)
        window = xbuf[:, sl].reshape(n_groups + 1, SUBLANES, LRU_BLOCK)
        xc = lane_row(cb_ref) + lane_row(cw_ref, CONV_WIDTH - 1) * window[1:]
        for back in range(1, CONV_WIDTH):
            rolled = pltpu.roll(window, back, 1)
            shifted = jnp.where(row_in_vreg >= back, rolled[1:], rolled[:-1])
            xc = xc + lane_row(cw_ref, CONV_WIDTH - 1 - back) * shifted
        xc = xc.reshape(tt, LRU_BLOCK)
        xcb = xc.astype(BF16)
        tanh_r = jnp.tanh(jnp.dot(xcb, wr_ref[n], preferred_element_type=F32) + br_ref[:, sl])
        tanh_i = jnp.tanh(jnp.dot(xcb, wi_ref[n], preferred_element_type=F32) + bi_ref[:, sl])
        two_r = tanh_r + 1.0
        a = jnp.exp2(two_r * log2_a_coef[:, sl])
        one_m_a2 = jnp.tanh(two_r * neg_log_a_coef[:, sl]) * (1.0 + a * a)
        mult = one_m_a2 * lax.rsqrt(jnp.maximum(one_m_a2, F32_TINY))
        b = mult * ((tanh_i + 1.0) * (0.5 * xc))
        a = a.reshape(grouped)
        b = b.reshape(grouped)
        for shift in (1, 2, 4):
            keep = row_in_vreg >= shift
            a_prev = jnp.where(keep, pltpu.roll(a, shift, 1), 1.0)
            b_prev = jnp.where(keep, pltpu.roll(b, shift, 1), 0.0)
            b = a * b_prev + b
            a = a * a_prev
        a_buf[:, sl] = a.reshape(tt, LRU_BLOCK)
        b_buf[:, sl] = b.reshape(tt, LRU_BLOCK)

    xbuf[0:SUBLANES, :] = xbuf[tt:tt + SUBLANES, :]

    def group(g, hb):
        rows = pl.ds(pl.multiple_of(g * SUBLANES, SUBLANES), SUBLANES)
        h = a_buf[rows, :] * hb + b_buf[rows, :]
        h_buf[rows, :] = h
        return jnp.broadcast_to(h[SUBLANES - 1:SUBLANES, :], (SUBLANES, c))

    hcarry[...] = lax.fori_loop(0, n_groups, group, hcarry[...])

    g = gate_ref[...]
    inner = g * (GELU_C1 + GELU_C3 * (g * g))
    y = (h_buf[...] * (0.5 * g)) * (1.0 + jnp.tanh(inner))
    o_ref[...] = _rms(y, gn_ref[...]).astype(o_ref.dtype)


def _lru_branch(lru_in, conv_w, conv_b, w_r, w_i, b_r, b_i, lam, gain, batch, seq, tt=256):
    m = lru_in.shape[0]
    c = lru_in.shape[1] // 2
    nt = seq // tt
    n_blocks = c // LRU_BLOCK
    row = lambda b, t: (0, 0)
    return pl.pallas_call(
        functools.partial(_lru_kernel, tt=tt),
        grid=(batch, nt),
        in_specs=[
            pl.BlockSpec((tt, c), lambda b, t: (b * nt + t, 0)),
            pl.BlockSpec((tt, c), lambda b, t: (b * nt + t, 1)),
            pl.BlockSpec((CONV_WIDTH, c), row),
            pl.BlockSpec((1, c), row),
            pl.BlockSpec((n_blocks, LRU_BLOCK, LRU_BLOCK), lambda b, t: (0, 0, 0)),
            pl.BlockSpec((n_blocks, LRU_BLOCK, LRU_BLOCK), lambda b, t: (0, 0, 0)),
            pl.BlockSpec((1, c), row),
            pl.BlockSpec((1, c), row),
            pl.BlockSpec((1, c), row),
            pl.BlockSpec((1, c), row),
        ],
        out_specs=pl.BlockSpec((tt, c), lambda b, t: (b * nt + t, 0)),
        out_shape=jax.ShapeDtypeStruct((m, c), BF16),
        scratch_shapes=[
            pltpu.VMEM((tt + SUBLANES, c), F32),
            pltpu.VMEM((tt, c), F32),
            pltpu.VMEM((tt, c), F32),
            pltpu.VMEM((tt, c), F32),
            pltpu.VMEM((SUBLANES, c), F32),
        ],
        compiler_params=_params(2),
        name="rglru",
    )(lru_in, lru_in, conv_w, conv_b, w_r, w_i, b_r, b_i, lam, gain)


def _outproj_kernel(ya_ref, yl_ref, ga_ref, w_ref, x_ref, gpost_ref, x1_ref, wbf_ref):
    @pl.when(pl.program_id(0) == 0)
    def _():
        wbf_ref[...] = w_ref[...].astype(BF16)

    ya = _rms(ya_ref[...], ga_ref[...]).astype(BF16)
    y = jnp.concatenate([ya, yl_ref[...]], axis=1)
    mix = jnp.dot(y, wbf_ref[...], preferred_element_type=F32)
    x1_ref[...] = x_ref[...] + _rms(mix, gpost_ref[...])


def _outproj(y_attn, y_lru, attn_gain, w, x2, post_gain, tm=512):
    m, d = x2.shape
    ca = y_attn.shape[1]
    cl = y_lru.shape[1]
    row = lambda i: (0, 0)
    return pl.pallas_call(
        _outproj_kernel,
        grid=(m // tm,),
        in_specs=[
            pl.BlockSpec((tm, ca), lambda i: (i, 0)),
            pl.BlockSpec((tm, cl), lambda i: (i, 0)),
            pl.BlockSpec((1, ca), row),
            pl.BlockSpec((ca + cl, d), row, pipeline_mode=pl.Buffered(1)),
            pl.BlockSpec((tm, d), lambda i: (i, 0)),
            pl.BlockSpec((1, d), row),
        ],
        out_specs=pl.BlockSpec((tm, d), lambda i: (i, 0)),
        out_shape=jax.ShapeDtypeStruct((m, d), F32),
        scratch_shapes=[pltpu.VMEM((ca + cl, d), BF16)],
        compiler_params=_params(1),
        name="outproj",
    )(y_attn, y_lru, attn_gain, w, x2, post_gain)


def _ffn_kernel(x1_ref, gpre_ref, wg_ref, wu_ref, wd_ref, gpost_ref, o_ref, hn_ref):
    f = pl.program_id(1)

    @pl.when(f == 0)
    def _():
        hn_ref[...] = _rms(x1_ref[...], gpre_ref[...]).astype(BF16)
        o_ref[...] = jnp.zeros(o_ref.shape, F32)

    hn = hn_ref[...]
    gate = jnp.dot(hn, wg_ref[...].astype(BF16), preferred_element_type=F32)
    up = jnp.dot(hn, wu_ref[...].astype(BF16), preferred_element_type=F32)
    act = (jax.nn.silu(gate) * up).astype(BF16)
    o_ref[...] += jnp.dot(act, wd_ref[...].astype(BF16), preferred_element_type=F32)

    @pl.when(f == pl.num_programs(1) - 1)
    def _():
        o_ref[...] = x1_ref[...] + _rms(o_ref[...], gpost_ref[...])


def _ffn(x1, pre_gain, wg, wu, wd, post_gain, tm=1024, tf=256):
    m, d = x1.shape
    dff = wg.shape[1]
    return pl.pallas_call(
        _ffn_kernel,
        grid=(m // tm, dff // tf),
        in_specs=[
            pl.BlockSpec((tm, d), lambda i, f: (i, 0)),
            pl.BlockSpec((1, d), lambda i, f: (0, 0)),
            pl.BlockSpec((d, tf), lambda i, f: (0, f)),
            pl.BlockSpec((d, tf), lambda i, f: (0, f)),
            pl.BlockSpec((tf, d), lambda i, f: (f, 0)),
            pl.BlockSpec((1, d), lambda i, f: (0, 0)),
        ],
        out_specs=pl.BlockSpec((tm, d), lambda i, f: (i, 0)),
        out_shape=jax.ShapeDtypeStruct((m, d), F32),
        scratch_shapes=[pltpu.VMEM((tm, d), BF16)],
        compiler_params=_params(2),
        name="swiglu_ffn",
    )(x1, pre_gain, wg, wu, wd, post_gain)


def _layer(x2, batch, seq, pre_mix, post_mix, pre_ffn, post_ffn, w_in, conv_w, conv_b, w_r, b_r,
           w_i, b_i, lam, attn_gain, lru_gain, w_out, w_g, w_u, w_d):
    row = lambda v: v.reshape(1, -1)
    lru_width = conv_w.shape[1]
    qkv_width = w_in.shape[1] - 2 * lru_width
    n_heads = qkv_width // (3 * HEAD_DIM)

    qkv, lru_in = _inproj(x2, row(pre_mix), w_in, qkv_width)
    y_attn = _attention(qkv, batch, seq, n_heads)
    y_lru = _lru_branch(lru_in, conv_w, row(conv_b), (0.5 * w_r).astype(BF16),
                        (0.5 * w_i).astype(BF16), row(0.5 * b_r), row(0.5 * b_i), row(lam),
                        row(lru_gain), batch, seq)
    x1 = _outproj(y_attn, y_lru, row(attn_gain), w_out, x2, row(post_mix))
    return _ffn(x1, row(pre_ffn), w_g, w_u, w_d, row(post_ffn))


def kernel(x, pre_mix_norm, post_mix_norm, pre_ffn_norm, post_ffn_norm, w_in, conv_w, conv_b,
           w_rgate, b_rgate, w_igate, b_igate, lru_lambda, attn_out_norm, lru_out_norm, w_out,
           w_ffn_gate, w_ffn_up, w_ffn_down):
    batch, seq, d = x.shape
    x2 = x.reshape(batch * seq, d)
    for l in range(w_in.shape[0]):
        x2 = _layer(x2, batch, seq, pre_mix_norm[l], post_mix_norm[l], pre_ffn_norm[l],
                    post_ffn_norm[l], w_in[l], conv_w[l], conv_b[l], w_rgate[l], b_rgate[l],
                    w_igate[l], b_igate[l], lru_lambda[l], attn_out_norm[l], lru_out_norm[l],
                    w_out[l], w_ffn_gate[l], w_ffn_up[l], w_ffn_down[l])
    return x2.reshape(batch, seq, d)
```

```python
import functools
import math

import jax
import jax.numpy as jnp
from jax import lax
from jax.experimental import pallas as pl
from jax.experimental.pallas import tpu as pltpu

F32 = jnp.float32
BF16 = jnp.bfloat16

HEAD_DIM = 128
LRU_BLOCK = 128
CONV_WIDTH = 4
LRU_C = 8.0
RMS_EPS = 1e-6
LOG2_E = math.log2(math.e)
Q_SCALE = LOG2_E / math.sqrt(HEAD_DIM)
F32_TINY = float(jnp.finfo(jnp.float32).tiny)
GELU_C1 = math.sqrt(2.0 / math.pi)
GELU_C3 = GELU_C1 * 0.044715
F32_ZERO_LOG2 = -150.0
SUBLANES = 8
VMEM_LIMIT = 60 * 1024 * 1024


def _params(n_axes):
    return pltpu.CompilerParams(dimension_semantics=("arbitrary",) * n_axes,
                                vmem_limit_bytes=VMEM_LIMIT)


def _rms(x, gain):
    var = jnp.mean(x * x, axis=-1, keepdims=True)
    return (x * lax.rsqrt(var + RMS_EPS)) * gain


def _log_sigmoid(z):
    return jnp.minimum(z, 0.0) - jnp.log1p(jnp.exp(-jnp.abs(z)))


def _inproj_kernel(x_ref, g_ref, w_ref, qkv_ref, lru_ref, hn_ref, *, n_q_tiles, n_qkv_tiles):
    j = pl.program_id(1)

    @pl.when(j == 0)
    def _():
        hn_ref[...] = _rms(x_ref[...], g_ref[...]).astype(BF16)

    acc = jnp.dot(hn_ref[...], w_ref[...].astype(BF16), preferred_element_type=F32)

    @pl.when(j < n_qkv_tiles)
    def _():
        qkv_ref[...] = (acc * jnp.where(j < n_q_tiles, Q_SCALE, 1.0)).astype(BF16)

    @pl.when(j >= n_qkv_tiles)
    def _():
        lru_ref[...] = acc


def _inproj(x2, gain, w, qkv_width, tm=1024, tn=1024):
    m, d = x2.shape
    n = w.shape[1]
    lru_width = n - qkv_width
    n_qkv_tiles = qkv_width // tn
    n_q_tiles = qkv_width // (3 * tn)
    grid = (m // tm, n // tn)
    return pl.pallas_call(
        functools.partial(_inproj_kernel, n_q_tiles=n_q_tiles, n_qkv_tiles=n_qkv_tiles),
        grid=grid,
        in_specs=[
            pl.BlockSpec((tm, d), lambda i, j: (i, 0)),
            pl.BlockSpec((1, d), lambda i, j: (0, 0)),
            pl.BlockSpec((d, tn), lambda i, j: (0, j)),
        ],
        out_specs=[
            pl.BlockSpec((tm, tn), lambda i, j: (i, jnp.minimum(j, n_qkv_tiles - 1))),
            pl.BlockSpec((tm, tn), lambda i, j: (i, jnp.maximum(j - n_qkv_tiles, 0))),
        ],
        out_shape=[
            jax.ShapeDtypeStruct((m, qkv_width), BF16),
            jax.ShapeDtypeStruct((m, lru_width), F32),
        ],
        scratch_shapes=[pltpu.VMEM((tm, d), BF16)],
        compiler_params=_params(2),
        name="inproj",
    )(x2, gain, w)


def _attn_kernel(q_ref, k_ref, v_ref, o_ref, tail_ref, *, tq, heads):
    tk = tq
    qi = pl.program_id(2)
    row = lax.broadcasted_iota(jnp.int32, (tk, tk), 0)
    col = lax.broadcasted_iota(jnp.int32, (tk, tk), 1)
    neg_later = jnp.where(row > col, -1.0, 0.0).astype(BF16)

    def tile(kj, first):
        start = pl.multiple_of(kj * tk, tk)
        log_betas = []
        softplus = []
        for h in range(heads):
            sl = slice(h * HEAD_DIM, (h + 1) * HEAD_DIM)
            k_t = k_ref[pl.ds(start, tk), sl]
            z2 = lax.dot_general(q_ref[:, sl], k_t, (((1,), (1,)), ((), ())),
                                 preferred_element_type=F32)
            pos = jnp.maximum(z2, 0.0)
            neg = z2 - pos
            log_term = jnp.log(1.0 + jnp.exp2(neg - pos)) * LOG2_E
            sp = pos + log_term
            if first:
                sp = jnp.where(col < row, sp, 0.0)
            softplus.append(sp)
            log_betas.append(neg - log_term)
        sp_all = jnp.concatenate(softplus, axis=0).astype(BF16)
        suffix_all = jnp.dot(sp_all, neg_later, preferred_element_type=F32)
        for h in range(heads):
            sl = slice(h * HEAD_DIM, (h + 1) * HEAD_DIM)
            v_t = v_ref[pl.ds(start, tk), sl]
            suffix = suffix_all[h * tq:(h + 1) * tq]
            expo = log_betas[h] + suffix
            if not first:
                expo = expo + jnp.concatenate([tail_ref[h]] * (tk // HEAD_DIM), axis=1)
            w = jnp.exp2(expo)
            if first:
                w = jnp.where(col < row, w, 0.0)
            pv = jnp.dot(w.astype(BF16), v_t, preferred_element_type=F32)
            total = jnp.broadcast_to(suffix[:, 0:1] - softplus[h][:, 0:1], (tq, HEAD_DIM))
            if first:
                o_ref[:, sl] = pv
                tail_ref[h] = total
            else:
                o_ref[:, sl] += pv
                tail_ref[h] += total

    def stick_left():
        return jnp.max(tail_ref[...]) > F32_ZERO_LOG2

    @pl.when(qi == 0)
    def _():
        tile(qi, True)

    @pl.when(qi > 0)
    def _():
        tile(qi, True)
        tile(qi - 1, False)

    def body(carry):
        it, _ = carry
        tile(qi - 1 - it, False)
        return it + 1, stick_left()

    lax.while_loop(lambda c: jnp.logical_and(c[0] < qi, c[1]), body,
                   (jnp.int32(1), stick_left()))


def _attention(qkv, batch, seq, n_heads, tq=256, heads=8):
    m = qkv.shape[0]
    nq = seq // tq
    ng = n_heads // heads
    width = heads * HEAD_DIM
    return pl.pallas_call(
        functools.partial(_attn_kernel, tq=tq, heads=heads),
        grid=(batch, ng, nq),
        in_specs=[
            pl.BlockSpec((tq, width), lambda b, g, i: (b * nq + i, g)),
            pl.BlockSpec((seq, width), lambda b, g, i: (b, ng + g)),
            pl.BlockSpec((seq, width), lambda b, g, i: (b, 2 * ng + g)),
        ],
        out_specs=pl.BlockSpec((tq, width), lambda b, g, i: (b * nq + i, g)),
        out_shape=jax.ShapeDtypeStruct((m, n_heads * HEAD_DIM), F32),
        scratch_shapes=[pltpu.VMEM((heads, tq, HEAD_DIM), F32)],
        compiler_params=_params(3),
        name="stickbreak_attn",
    )(qkv, qkv, qkv)


def _lru_kernel(x_ref, gate_ref, cw_ref, cb_ref, wr_ref, wi_ref, br_ref, bi_ref, lam_ref,
                gn_ref, o_ref, xbuf, a_buf, b_buf, h_buf, hcarry, *, tt):
    t = pl.program_id(1)
    c = x_ref.shape[1]
    n_groups = tt // SUBLANES
    grouped = (n_groups, SUBLANES, LRU_BLOCK)

    @pl.when(t == 0)
    def _():
        xbuf[0:SUBLANES, :] = jnp.zeros((SUBLANES, c), F32)
        hcarry[...] = jnp.zeros((SUBLANES, c), F32)

    xbuf[SUBLANES:SUBLANES + tt, :] = x_ref[...]

    row_in_vreg = lax.broadcasted_iota(jnp.int32, grouped, 1)
    log_sig_lam = _log_sigmoid(lam_ref[...])
    log2_a_coef = (0.5 * LRU_C * LOG2_E) * log_sig_lam
    neg_log_a_coef = (-0.5 * LRU_C) * log_sig_lam
    for n in range(c // LRU_BLOCK):
        sl = slice(n * LRU_BLOCK, (n + 1) * LRU_BLOCK)
        lane_row = lambda ref, k=0: ref[k:k + 1, sl].reshape(1, 1, LRU_BLOCK)
        window = xbuf[:, sl].reshape(n_groups + 1, SUBLANES, LRU_BLOCK)
        xc = lane_row(cb_ref) + lane_row(cw_ref, CONV_WIDTH - 1) * window[1:]
        for back in range(1, CONV_WIDTH):
            rolled = pltpu.roll(window, back, 1)
            shifted = jnp.where(row_in_vreg >= back, rolled[1:], rolled[:-1])
            xc = xc + lane_row(cw_ref, CONV_WIDTH - 1 - back) * shifted
        xc = xc.reshape(tt, LRU_BLOCK)
        xcb = xc.astype(BF16)
        tanh_r = jnp.tanh(jnp.dot(xcb, wr_ref[n], preferred_element_type=F32) + br_ref[:, sl])
        tanh_i = jnp.tanh(jnp.dot(xcb, wi_ref[n], preferred_element_type=F32) + bi_ref[:, sl])
        two_r = tanh_r + 1.0
        a = jnp.exp2(two_r * log2_a_coef[:, sl])
        one_m_a2 = jnp.tanh(two_r * neg_log_a_coef[:, sl]) * (1.0 + a * a)
        mult = one_m_a2 * lax.rsqrt(jnp.maximum(one_m_a2, F32_TINY))
        b = mult * ((tanh_i + 1.0) * (0.5 * xc))
        a = a.reshape(grouped)
        b = b.reshape(grouped)
        for shift in (1, 2, 4):
            keep = row_in_vreg >= shift
            a_prev = jnp.where(keep, pltpu.roll(a, shift, 1), 1.0)
            b_prev = jnp.where(keep, pltpu.roll(b, shift, 1), 0.0)
            b = a * b_prev + b
            a = a * a_prev
        a_buf[:, sl] = a.reshape(tt, LRU_BLOCK)
        b_buf[:, sl] = b.reshape(tt, LRU_BLOCK)

    xbuf[0:SUBLANES, :] = xbuf[tt:tt + SUBLANES, :]

    def group(g, hb):
        rows = pl.ds(pl.multiple_of(g * SUBLANES, SUBLANES), SUBLANES)
        h = a_buf[rows, :] * hb + b_buf[rows, :]
        h_buf[rows, :] = h
        return jnp.broadcast_to(h[SUBLANES - 1:SUBLANES, :], (SUBLANES, c))

    hcarry[...] = lax.fori_loop(0, n_groups, group, hcarry[...])

    g = gate_ref[...]
    inner = g * (GELU_C1 + GELU_C3 * (g * g))
    y = (h_buf[...] * (0.5 * g)) * (1.0 + jnp.tanh(inner))
    o_ref[...] = _rms(y, gn_ref[...]).astype(o_ref.dtype)


def _lru_branch(lru_in, conv_w, conv_b, w_r, w_i, b_r, b_i, lam, gain, batch, seq, tt=256):
    m = lru_in.shape[0]
    c = lru_in.shape[1] // 2
    nt = seq // tt
    n_blocks = c // LRU_BLOCK
    row = lambda b, t: (0, 0)
    return pl.pallas_call(
        functools.partial(_lru_kernel, tt=tt),
        grid=(batch, nt),
        in_specs=[
            pl.BlockSpec((tt, c), lambda b, t: (b * nt + t, 0)),
            pl.BlockSpec((tt, c), lambda b, t: (b * nt + t, 1)),
            pl.BlockSpec((CONV_WIDTH, c), row),
            pl.BlockSpec((1, c), row),
            pl.BlockSpec((n_blocks, LRU_BLOCK, LRU_BLOCK), lambda b, t: (0, 0, 0)),
            pl.BlockSpec((n_blocks, LRU_BLOCK, LRU_BLOCK), lambda b, t: (0, 0, 0)),
            pl.BlockSpec((1, c), row),
            pl.BlockSpec((1, c), row),
            pl.BlockSpec((1, c), row),
            pl.BlockSpec((1, c), row),
        ],
        out_specs=pl.BlockSpec((tt, c), lambda b, t: (b * nt + t, 0)),
        out_shape=jax.ShapeDtypeStruct((m, c), BF16),
        scratch_shapes=[
            pltpu.VMEM((tt + SUBLANES, c), F32),
            pltpu.VMEM((tt, c), F32),
            pltpu.VMEM((tt, c), F32),
            pltpu.VMEM((tt, c), F32),
            pltpu.VMEM((SUBLANES, c), F32),
        ],
        compiler_params=_params(2),
        name="rglru",
    )(lru_in, lru_in, conv_w, conv_b, w_r, w_i, b_r, b_i, lam, gain)


def _outproj_kernel(ya_ref, yl_ref, ga_ref, w_ref, x_ref, gpost_ref, x1_ref, wbf_ref):
    @pl.when(pl.program_id(0) == 0)
    def _():
        wbf_ref[...] = w_ref[...].astype(BF16)

    ya = _rms(ya_ref[...], ga_ref[...]).astype(BF16)
    y = jnp.concatenate([ya, yl_ref[...]], axis=1)
    mix = jnp.dot(y, wbf_ref[...], preferred_element_type=F32)
    x1_ref[...] = x_ref[...] + _rms(mix, gpost_ref[...])


def _outproj(y_attn, y_lru, attn_gain, w, x2, post_gain, tm=512):
    m, d = x2.shape
    ca = y_attn.shape[1]
    cl = y_lru.shape[1]
    row = lambda i: (0, 0)
    return pl.pallas_call(
        _outproj_kernel,
        grid=(m // tm,),
        in_specs=[
            pl.BlockSpec((tm, ca), lambda i: (i, 0)),
            pl.BlockSpec((tm, cl), lambda i: (i, 0)),
            pl.BlockSpec((1, ca), row),
            pl.BlockSpec((ca + cl, d), row, pipeline_mode=pl.Buffered(1)),
            pl.BlockSpec((tm, d), lambda i: (i, 0)),
            pl.BlockSpec((1, d), row),
        ],
        out_specs=pl.BlockSpec((tm, d), lambda i: (i, 0)),
        out_shape=jax.ShapeDtypeStruct((m, d), F32),
        scratch_shapes=[pltpu.VMEM((ca + cl, d), BF16)],
        compiler_params=_params(1),
        name="outproj",
    )(y_attn, y_lru, attn_gain, w, x2, post_gain)


def _ffn_kernel(x1_ref, gpre_ref, wg_ref, wu_ref, wd_ref, gpost_ref, o_ref, hn_ref):
    f = pl.program_id(1)

    @pl.when(f == 0)
    def _():
        hn_ref[...] = _rms(x1_ref[...], gpre_ref[...]).astype(BF16)
        o_ref[...] = jnp.zeros(o_ref.shape, F32)

    hn = hn_ref[...]
    gate = jnp.dot(hn, wg_ref[...].astype(BF16), preferred_element_type=F32)
    up = jnp.dot(hn, wu_ref[...].astype(BF16), preferred_element_type=F32)
    act = (jax.nn.silu(gate) * up).astype(BF16)
    o_ref[...] += jnp.dot(act, wd_ref[...].astype(BF16), preferred_element_type=F32)

    @pl.when(f == pl.num_programs(1) - 1)
    def _():
        o_ref[...] = x1_ref[...] + _rms(o_ref[...], gpost_ref[...])


def _ffn(x1, pre_gain, wg, wu, wd, post_gain, tm=1024, tf=256):
    m, d = x1.shape
    dff = wg.shape[1]
    return pl.pallas_call(
        _ffn_kernel,
        grid=(m // tm, dff // tf),
        in_specs=[
            pl.BlockSpec((tm, d), lambda i, f: (i, 0)),
            pl.BlockSpec((1, d), lambda i, f: (0, 0)),
            pl.BlockSpec((d, tf), lambda i, f: (0, f)),
            pl.BlockSpec((d, tf), lambda i, f: (0, f)),
            pl.BlockSpec((tf, d), lambda i, f: (f, 0)),
            pl.BlockSpec((1, d), lambda i, f: (0, 0)),
        ],
        out_specs=pl.BlockSpec((tm, d), lambda i, f: (i, 0)),
        out_shape=jax.ShapeDtypeStruct((m, d), F32),
        scratch_shapes=[pltpu.VMEM((tm, d), BF16)],
        compiler_params=_params(2),
        name="swiglu_ffn",
    )(x1, pre_gain, wg, wu, wd, post_gain)


def _layer(x2, batch, seq, pre_mix, post_mix, pre_ffn, post_ffn, w_in, conv_w, conv_b, w_r, b_r,
           w_i, b_i, lam, attn_gain, lru_gain, w_out, w_g, w_u, w_d):
    row = lambda v: v.reshape(1, -1)
    lru_width = conv_w.shape[1]
    qkv_width = w_in.shape[1] - 2 * lru_width
    n_heads = qkv_width // (3 * HEAD_DIM)

    qkv, lru_in = _inproj(x2, row(pre_mix), w_in, qkv_width)
    y_attn = _attention(qkv, batch, seq, n_heads)
    y_lru = _lru_branch(lru_in, conv_w, row(conv_b), (0.5 * w_r).astype(BF16),
                        (0.5 * w_i).astype(BF16), row(0.5 * b_r), row(0.5 * b_i), row(lam),
                        row(lru_gain), batch, seq)
    x1 = _outproj(y_attn, y_lru, row(attn_gain), w_out, x2, row(post_mix))
    return _ffn(x1, row(pre_ffn), w_g.astype(BF16), w_u.astype(BF16), w_d.astype(BF16),
                row(post_ffn), tf=256)


def kernel(x, pre_mix_norm, post_mix_norm, pre_ffn_norm, post_ffn_norm, w_in, conv_w, conv_b,
           w_rgate, b_rgate, w_igate, b_igate, lru_lambda, attn_out_norm, lru_out_norm, w_out,
           w_ffn_gate, w_ffn_up, w_ffn_down):
    batch, seq, d = x.shape
    x2 = x.reshape(batch * seq, d)
    for l in range(w_in.shape[0]):
        x2 = _layer(x2, batch, seq, pre_mix_norm[l], post_mix_norm[l], pre_ffn_norm[l],
                    post_ffn_norm[l], w_in[l], conv_w[l], conv_b[l], w_rgate[l], b_rgate[l],
                    w_igate[l], b_igate[l], lru_lambda[l], attn_out_norm[l], lru_out_norm[l],
                    w_out[l], w_ffn_gate[l], w_ffn_up[l], w_ffn_down[l])
    return x2.reshape(batch, seq, d)
```

```python
import functools
import math

import jax
import jax.numpy as jnp
from jax import lax
from jax.experimental import pallas as pl
from jax.experimental.pallas import tpu as pltpu

F32 = jnp.float32
BF16 = jnp.bfloat16

HEAD_DIM = 128
LRU_BLOCK = 128
CONV_WIDTH = 4
LRU_C = 8.0
RMS_EPS = 1e-6
LOG2_E = math.log2(math.e)
Q_SCALE = LOG2_E / math.sqrt(HEAD_DIM)
F32_TINY = float(jnp.finfo(jnp.float32).tiny)
GELU_C1 = math.sqrt(2.0 / math.pi)
GELU_C3 = GELU_C1 * 0.044715
F32_ZERO_LOG2 = -150.0
SUBLANES = 8
NORM_ROWS = 128
FFN_SLAB = 256
VMEM_LIMIT = 60 * 1024 * 1024


def _params(n_axes):
    return pltpu.CompilerParams(dimension_semantics=("arbitrary",) * n_axes,
                                vmem_limit_bytes=VMEM_LIMIT)


def _rms(x, gain):
    var = jnp.mean(x * x, axis=-1, keepdims=True)
    return (x * lax.rsqrt(var + RMS_EPS)) * gain


def _log_sigmoid(z):
    return jnp.minimum(z, 0.0) - jnp.log1p(jnp.exp(-jnp.abs(z)))


def _inproj_kernel(x_ref, g_ref, w_ref, qkv_ref, lru_ref, hn_ref, *, n_q_tiles, n_qkv_tiles):
    j = pl.program_id(1)

    @pl.when(j == 0)
    def _():
        for r0 in range(0, x_ref.shape[0], NORM_ROWS):
            rows = slice(r0, r0 + NORM_ROWS)
            hn_ref[rows, :] = _rms(x_ref[rows, :], g_ref[...]).astype(BF16)

    acc = jnp.dot(hn_ref[...], w_ref[...].astype(BF16), preferred_element_type=F32)

    @pl.when(j < n_qkv_tiles)
    def _():
        qkv_ref[...] = (acc * jnp.where(j < n_q_tiles, Q_SCALE, 1.0)).astype(BF16)

    @pl.when(j >= n_qkv_tiles)
    def _():
        lru_ref[...] = acc


def _inproj(x2, gain, w, qkv_width, tm=1024, tn=1024):
    m, d = x2.shape
    n = w.shape[1]
    lru_width = n - qkv_width
    n_qkv_tiles = qkv_width // tn
    n_q_tiles = qkv_width // (3 * tn)
    grid = (m // tm, n // tn)
    return pl.pallas_call(
        functools.partial(_inproj_kernel, n_q_tiles=n_q_tiles, n_qkv_tiles=n_qkv_tiles),
        grid=grid,
        in_specs=[
            pl.BlockSpec((tm, d), lambda i, j: (i, 0)),
            pl.BlockSpec((1, d), lambda i, j: (0, 0)),
            pl.BlockSpec((d, tn), lambda i, j: (0, j)),
        ],
        out_specs=[
            pl.BlockSpec((tm, tn), lambda i, j: (i, jnp.minimum(j, n_qkv_tiles - 1))),
            pl.BlockSpec((tm, tn), lambda i, j: (i, jnp.maximum(j - n_qkv_tiles, 0))),
        ],
        out_shape=[
            jax.ShapeDtypeStruct((m, qkv_width), BF16),
            jax.ShapeDtypeStruct((m, lru_width), F32),
        ],
        scratch_shapes=[pltpu.VMEM((tm, d), BF16)],
        compiler_params=_params(2),
        name="inproj",
    )(x2, gain, w)


def _attn_kernel(q_ref, k_ref, v_ref, *refs, tq, heads, n_cast):
    cast_in, o_ref = refs[:n_cast], refs[n_cast]
    cast_out, tail_ref = refs[n_cast + 1:2 * n_cast + 1], refs[2 * n_cast + 1]

    def cast_weights():
        for src, dst in zip(cast_in, cast_out):
            dst[...] = src[...].astype(BF16)

    tk = tq
    qi = pl.program_id(2)
    row = lax.broadcasted_iota(jnp.int32, (tk, tk), 0)
    col = lax.broadcasted_iota(jnp.int32, (tk, tk), 1)
    neg_later = jnp.where(row > col, -1.0, 0.0).astype(BF16)

    def tile(kj, first):
        start = pl.multiple_of(kj * tk, tk)
        log_betas = []
        softplus = []
        for h in range(heads):
            sl = slice(h * HEAD_DIM, (h + 1) * HEAD_DIM)
            k_t = k_ref[pl.ds(start, tk), sl]
            z2 = lax.dot_general(q_ref[:, sl], k_t, (((1,), (1,)), ((), ())),
                                 preferred_element_type=F32)
            pos = jnp.maximum(z2, 0.0)
            neg = z2 - pos
            log_term = jnp.log(1.0 + jnp.exp2(neg - pos)) * LOG2_E
            sp = pos + log_term
            if first:
                sp = jnp.where(col < row, sp, 0.0)
            softplus.append(sp)
            log_betas.append(neg - log_term)
        sp_all = jnp.concatenate(softplus, axis=0).astype(BF16)
        suffix_all = jnp.dot(sp_all, neg_later, preferred_element_type=F32)
        for h in range(heads):
            sl = slice(h * HEAD_DIM, (h + 1) * HEAD_DIM)
            v_t = v_ref[pl.ds(start, tk), sl]
            suffix = suffix_all[h * tq:(h + 1) * tq]
            expo = log_betas[h] + suffix
            if not first:
                expo = expo + jnp.concatenate([tail_ref[h]] * (tk // HEAD_DIM), axis=1)
            w = jnp.exp2(expo)
            if first:
                w = jnp.where(col < row, w, 0.0)
            pv = jnp.dot(w.astype(BF16), v_t, preferred_element_type=F32)
            total = jnp.broadcast_to(suffix[:, 0:1] - softplus[h][:, 0:1], (tq, HEAD_DIM))
            if first:
                o_ref[:, sl] = pv
                tail_ref[h] = total
            else:
                o_ref[:, sl] += pv
                tail_ref[h] += total

    def stick_left():
        return jnp.max(tail_ref[...]) > F32_ZERO_LOG2

    @pl.when(qi == 0)
    def _():
        cast_weights()
        tile(qi, True)

    @pl.when(qi > 0)
    def _():
        cast_weights()
        tile(qi, True)
        tile(qi - 1, False)

    def body(carry):
        it, _ = carry
        tile(qi - 1 - it, False)
        return it + 1, stick_left()

    lax.while_loop(lambda c: jnp.logical_and(c[0] < qi, c[1]), body,
                   (jnp.int32(1), stick_left()))


def _attention(qkv, batch, seq, n_heads, later_weights, tq=256, heads=8):
    m = qkv.shape[0]
    nq = seq // tq
    ng = n_heads // heads
    width = heads * HEAD_DIM
    n_steps = batch * ng * nq
    step = lambda b, g, i: ((b * ng + g) * nq + i, 0)
    slab_specs = [pl.BlockSpec((w.shape[0] // n_steps, w.shape[1]), step) for w in later_weights]
    outs = pl.pallas_call(
        functools.partial(_attn_kernel, tq=tq, heads=heads, n_cast=len(later_weights)),
        grid=(batch, ng, nq),
        in_specs=[
            pl.BlockSpec((tq, width), lambda b, g, i: (b * nq + i, g)),
            pl.BlockSpec((seq, width), lambda b, g, i: (b, ng + g)),
            pl.BlockSpec((seq, width), lambda b, g, i: (b, 2 * ng + g)),
            *slab_specs,
        ],
        out_specs=[pl.BlockSpec((tq, width), lambda b, g, i: (b * nq + i, g)), *slab_specs],
        out_shape=[jax.ShapeDtypeStruct((m, n_heads * HEAD_DIM), F32),
                   *[jax.ShapeDtypeStruct(w.shape, BF16) for w in later_weights]],
        scratch_shapes=[pltpu.VMEM((heads, tq, HEAD_DIM), F32)],
        compiler_params=_params(3),
        name="stickbreak_attn",
    )(qkv, qkv, qkv, *later_weights)
    return outs[0], outs[1:]


def _lru_kernel(x_ref, gate_ref, cw_ref, cb_ref, wr_ref, wi_ref, br_ref, bi_ref, lam_ref,
                gn_ref, o_ref, xbuf, a_buf, b_buf, h_buf, hcarry, *, tt):
    t = pl.program_id(1)
    c = x_ref.shape[1]
    n_groups = tt // SUBLANES
    grouped = (n_groups, SUBLANES, LRU_BLOCK)

    @pl.when(t == 0)
    def _():
        xbuf[0:SUBLANES, :] = jnp.zeros((SUBLANES, c), F32)
        hcarry[...] = jnp.zeros((SUBLANES, c), F32)

    xbuf[SUBLANES:SUBLANES + tt, :] = x_ref[...]

    row_in_vreg = lax.broadcasted_iota(jnp.int32, grouped, 1)
    log_sig_lam = _log_sigmoid(lam_ref[...])
    log2_a_coef = (0.5 * LRU_C * LOG2_E) * log_sig_lam
    neg_log_a_coef = (-0.5 * LRU_C) * log_sig_lam
    for n in range(c // LRU_BLOCK):
        sl = slice(n * LRU_BLOCK, (n + 1) * LRU_BLOCK)
        lane_row = lambda ref, k=0: ref[k:k + 1, sl].reshape(1, 1, LRU_BLOCK)
        window = xbuf[:, sl].reshape(n_groups + 1, SUBLANES, LRU_BLOCK)
        xc = lane_row(cb_ref) + lane_row(cw_ref, CONV_WIDTH - 1) * window[1:]
        for back in range(1, CONV_WIDTH):
            rolled = pltpu.roll(window, back, 1)
            shifted = jnp.where(row_in_vreg >= back, rolled[1:], rolled[:-1])
            xc = xc + lane_row(cw_ref, CONV_WIDTH - 1 - back) * shifted
        xc = xc.reshape(tt, LRU_BLOCK)
        xcb = xc.astype(BF16)
        tanh_r = jnp.tanh(jnp.dot(xcb, wr_ref[n], preferred_element_type=F32) + br_ref[:, sl])
        tanh_i = jnp.tanh(jnp.dot(xcb, wi_ref[n], preferred_element_type=F32) + bi_ref[:, sl])
        two_r = tanh_r + 1.0
        a = jnp.exp2(two_r * log2_a_coef[:, sl])
        one_m_a2 = jnp.tanh(two_r * neg_log_a_coef[:, sl]) * (1.0 + a * a)
        mult = one_m_a2 * lax.rsqrt(jnp.maximum(one_m_a2, F32_TINY))
        b = mult * ((tanh_i + 1.0) * (0.5 * xc))
        a = a.reshape(grouped)
        b = b.reshape(grouped)
        for shift in (1, 2, 4):
            keep = row_in_vreg >= shift
            a_prev = jnp.where(keep, pltpu.roll(a, shift, 1), 1.0)
            b_prev = jnp.where(keep, pltpu.roll(b, shift, 1), 0.0)
            b = a * b_prev + b
            a = a * a_prev
        a_buf[:, sl] = a.reshape(tt, LRU_BLOCK)
        b_buf[:, sl] = b.reshape(tt, LRU_BLOCK)

    xbuf[0:SUBLANES, :] = xbuf[tt:tt + SUBLANES, :]

    def group(g, hb):
        rows = pl.ds(pl.multiple_of(g * SUBLANES, SUBLANES), SUBLANES)
        h = a_buf[rows, :] * hb + b_buf[rows, :]
        h_buf[rows, :] = h
        return jnp.broadcast_to(h[SUBLANES - 1:SUBLANES, :], (SUBLANES, c))

    hcarry[...] = lax.fori_loop(0, n_groups, group, hcarry[...])

    g = gate_ref[...]
    inner = g * (GELU_C1 + GELU_C3 * (g * g))
    y = (h_buf[...] * (0.5 * g)) * (1.0 + jnp.tanh(inner))
    o_ref[...] = _rms(y, gn_ref[...]).astype(o_ref.dtype)


def _lru_branch(lru_in, conv_w, conv_b, w_r, w_i, b_r, b_i, lam, gain, batch, seq, tt=512):
    m = lru_in.shape[0]
    c = lru_in.shape[1] // 2
    nt = seq // tt
    n_blocks = c // LRU_BLOCK
    row = lambda b, t: (0, 0)
    return pl.pallas_call(
        functools.partial(_lru_kernel, tt=tt),
        grid=(batch, nt),
        in_specs=[
            pl.BlockSpec((tt, c), lambda b, t: (b * nt + t, 0)),
            pl.BlockSpec((tt, c), lambda b, t: (b * nt + t, 1)),
            pl.BlockSpec((CONV_WIDTH, c), row),
            pl.BlockSpec((1, c), row),
            pl.BlockSpec((n_blocks, LRU_BLOCK, LRU_BLOCK), lambda b, t: (0, 0, 0)),
            pl.BlockSpec((n_blocks, LRU_BLOCK, LRU_BLOCK), lambda b, t: (0, 0, 0)),
            pl.BlockSpec((1, c), row),
            pl.BlockSpec((1, c), row),
            pl.BlockSpec((1, c), row),
            pl.BlockSpec((1, c), row),
        ],
        out_specs=pl.BlockSpec((tt, c), lambda b, t: (b * nt + t, 0)),
        out_shape=jax.ShapeDtypeStruct((m, c), BF16),
        scratch_shapes=[
            pltpu.VMEM((tt + SUBLANES, c), F32),
            pltpu.VMEM((tt, c), F32),
            pltpu.VMEM((tt, c), F32),
            pltpu.VMEM((tt, c), F32),
            pltpu.VMEM((SUBLANES, c), F32),
        ],
        compiler_params=_params(2),
        name="rglru",
    )(lru_in, lru_in, conv_w, conv_b, w_r, w_i, b_r, b_i, lam, gain)


def _outproj_kernel(ya_ref, yl_ref, ga_ref, w_ref, x_ref, gpost_ref, x1_ref):
    ya = _rms(ya_ref[...], ga_ref[...]).astype(BF16)
    y = jnp.concatenate([ya, yl_ref[...]], axis=1)
    mix = jnp.dot(y, w_ref[...], preferred_element_type=F32)
    x1_ref[...] = x_ref[...] + _rms(mix, gpost_ref[...])


def _outproj(y_attn, y_lru, attn_gain, w_bf, x2, post_gain, tm=512):
    m, d = x2.shape
    ca = y_attn.shape[1]
    cl = y_lru.shape[1]
    row = lambda i: (0, 0)
    return pl.pallas_call(
        _outproj_kernel,
        grid=(m // tm,),
        in_specs=[
            pl.BlockSpec((tm, ca), lambda i: (i, 0)),
            pl.BlockSpec((tm, cl), lambda i: (i, 0)),
            pl.BlockSpec((1, ca), row),
            pl.BlockSpec((ca + cl, d), row, pipeline_mode=pl.Buffered(1)),
            pl.BlockSpec((tm, d), lambda i: (i, 0)),
            pl.BlockSpec((1, d), row),
        ],
        out_specs=pl.BlockSpec((tm, d), lambda i: (i, 0)),
        out_shape=jax.ShapeDtypeStruct((m, d), F32),
        compiler_params=_params(1),
        name="outproj",
    )(y_attn, y_lru, attn_gain, w_bf, x2, post_gain)


def _ffn_kernel(x1_ref, gpre_ref, wg_ref, wu_ref, wd_ref, gpost_ref, o_ref, hn_ref):
    f = pl.program_id(1)

    row_chunks = [slice(r0, r0 + NORM_ROWS) for r0 in range(0, o_ref.shape[0], NORM_ROWS)]

    @pl.when(f == 0)
    def _():
        for rows in row_chunks:
            hn_ref[rows, :] = _rms(x1_ref[rows, :], gpre_ref[...]).astype(BF16)
        o_ref[...] = jnp.zeros(o_ref.shape, F32)

    hn = hn_ref[...]
    for c0 in range(0, wg_ref.shape[1], FFN_SLAB):
        cols = slice(c0, c0 + FFN_SLAB)
        gate = jnp.dot(hn, wg_ref[:, cols], preferred_element_type=F32)
        up = jnp.dot(hn, wu_ref[:, cols], preferred_element_type=F32)
        act = (jax.nn.silu(gate) * up).astype(BF16)
        o_ref[...] += jnp.dot(act, wd_ref[cols, :], preferred_element_type=F32)

    @pl.when(f == pl.num_programs(1) - 1)
    def _():
        for rows in row_chunks:
            o_ref[rows, :] = x1_ref[rows, :] + _rms(o_ref[rows, :], gpost_ref[...])


def _ffn(x1, pre_gain, wg, wu, wd, post_gain, tm=1024, tf=512):
    m, d = x1.shape
    dff = wg.shape[1]
    return pl.pallas_call(
        _ffn_kernel,
        grid=(m // tm, dff // tf),
        in_specs=[
            pl.BlockSpec((tm, d), lambda i, f: (i, 0)),
            pl.BlockSpec((1, d), lambda i, f: (0, 0)),
            pl.BlockSpec((d, tf), lambda i, f: (0, f)),
            pl.BlockSpec((d, tf), lambda i, f: (0, f)),
            pl.BlockSpec((tf, d), lambda i, f: (f, 0)),
            pl.BlockSpec((1, d), lambda i, f: (0, 0)),
        ],
        out_specs=pl.BlockSpec((tm, d), lambda i, f: (i, 0)),
        out_shape=jax.ShapeDtypeStruct((m, d), F32),
        scratch_shapes=[pltpu.VMEM((tm, d), BF16)],
        compiler_params=_params(2),
        name="swiglu_ffn",
    )(x1, pre_gain, wg, wu, wd, post_gain)


def _layer(x2, batch, seq, pre_mix, post_mix, pre_ffn, post_ffn, w_in, conv_w, conv_b, w_r, b_r,
           w_i, b_i, lam, attn_gain, lru_gain, w_out, w_g, w_u, w_d):
    row = lambda v: v.reshape(1, -1)
    lru_width = conv_w.shape[1]
    qkv_width = w_in.shape[1] - 2 * lru_width
    n_heads = qkv_width // (3 * HEAD_DIM)

    qkv, lru_in = _inproj(x2, row(pre_mix), w_in, qkv_width)
    y_attn, (w_out_bf, w_g_bf, w_u_bf, w_d_bf) = _attention(qkv, batch, seq, n_heads,
                                                            (w_out, w_g, w_u, w_d))
    y_lru = _lru_branch(lru_in, conv_w, row(conv_b), (0.5 * w_r).astype(BF16),
                        (0.5 * w_i).astype(BF16), row(0.5 * b_r), row(0.5 * b_i), row(lam),
                        row(lru_gain), batch, seq)
    x1 = _outproj(y_attn, y_lru, row(attn_gain), w_out_bf, x2, row(post_mix))
    return _ffn(x1, row(pre_ffn), w_g_bf, w_u_bf, w_d_bf, row(post_ffn))


def kernel(x, pre_mix_norm, post_mix_norm, pre_ffn_norm, post_ffn_norm, w_in, conv_w, conv_b,
           w_rgate, b_rgate, w_igate, b_igate, lru_lambda, attn_out_norm, lru_out_norm, w_out,
           w_ffn_gate, w_ffn_up, w_ffn_down):
    batch, seq, d = x.shape
    x2 = x.reshape(batch * seq, d)
    for l in range(w_in.shape[0]):
        x2 = _layer(x2, batch, seq, pre_mix_norm[l], post_mix_norm[l], pre_ffn_norm[l],
                    post_ffn_norm[l], w_in[l], conv_w[l], conv_b[l], w_rgate[l], b_rgate[l],
                    w_igate[l], b_igate[l], lru_lambda[l], attn_out_norm[l], lru_out_norm[l],
                    w_out[l], w_ffn_gate[l], w_ffn_up[l], w_ffn_down[l])
    return x2.reshape(batch, seq, d)
```

```python
import functools
import math

import jax
import jax.numpy as jnp
from jax import lax
from jax.experimental import pallas as pl
from jax.experimental.pallas import tpu as pltpu

F32 = jnp.float32
BF16 = jnp.bfloat16

HEAD_DIM = 128
LRU_BLOCK = 128
CONV_WIDTH = 4
LRU_C = 8.0
RMS_EPS = 1e-6
LOG2_E = math.log2(math.e)
Q_SCALE = LOG2_E / math.sqrt(HEAD_DIM)
F32_TINY = float(jnp.finfo(jnp.float32).tiny)
GELU_C1 = math.sqrt(2.0 / math.pi)
GELU_C3 = GELU_C1 * 0.044715
F32_ZERO_LOG2 = -150.0
SUBLANES = 8
NORM_ROWS = 128
FFN_SLAB = 256
VMEM_LIMIT = 60 * 1024 * 1024


def _params(n_axes):
    return pltpu.CompilerParams(dimension_semantics=("arbitrary",) * n_axes,
                                vmem_limit_bytes=VMEM_LIMIT)


def _rms(x, gain):
    var = jnp.mean(x * x, axis=-1, keepdims=True)
    return (x * lax.rsqrt(var + RMS_EPS)) * gain


def _log_sigmoid(z):
    return jnp.minimum(z, 0.0) - jnp.log1p(jnp.exp(-jnp.abs(z)))


def _inproj_kernel(x_ref, g_ref, w_ref, qkv_ref, lru_ref, hn_ref, *, n_q_tiles, n_qkv_tiles):
    j = pl.program_id(1)

    @pl.when(j == 0)
    def _():
        for r0 in range(0, x_ref.shape[0], NORM_ROWS):
            rows = slice(r0, r0 + NORM_ROWS)
            hn_ref[rows, :] = _rms(x_ref[rows, :], g_ref[...]).astype(BF16)

    acc = jnp.dot(hn_ref[...], w_ref[...].astype(BF16), preferred_element_type=F32)

    @pl.when(j < n_qkv_tiles)
    def _():
        qkv_ref[...] = (acc * jnp.where(j < n_q_tiles, Q_SCALE, 1.0)).astype(BF16)

    @pl.when(j >= n_qkv_tiles)
    def _():
        lru_ref[...] = acc


def _inproj(x2, gain, w, qkv_width, tm=1024, tn=1024):
    m, d = x2.shape
    n = w.shape[1]
    lru_width = n - qkv_width
    n_qkv_tiles = qkv_width // tn
    n_q_tiles = qkv_width // (3 * tn)
    grid = (m // tm, n // tn)
    return pl.pallas_call(
        functools.partial(_inproj_kernel, n_q_tiles=n_q_tiles, n_qkv_tiles=n_qkv_tiles),
        grid=grid,
        in_specs=[
            pl.BlockSpec((tm, d), lambda i, j: (i, 0)),
            pl.BlockSpec((1, d), lambda i, j: (0, 0)),
            pl.BlockSpec((d, tn), lambda i, j: (0, j)),
        ],
        out_specs=[
            pl.BlockSpec((tm, tn), lambda i, j: (i, jnp.minimum(j, n_qkv_tiles - 1))),
            pl.BlockSpec((tm, tn), lambda i, j: (i, jnp.maximum(j - n_qkv_tiles, 0))),
        ],
        out_shape=[
            jax.ShapeDtypeStruct((m, qkv_width), BF16),
            jax.ShapeDtypeStruct((m, lru_width), F32),
        ],
        scratch_shapes=[pltpu.VMEM((tm, d), BF16)],
        compiler_params=_params(2),
        name="inproj",
    )(x2, gain, w)


def _attn_kernel(q_ref, k_ref, v_ref, *refs, tq, heads, n_cast):
    cast_in, o_ref = refs[:n_cast], refs[n_cast]
    cast_out, tail_ref = refs[n_cast + 1:2 * n_cast + 1], refs[2 * n_cast + 1]

    def cast_weights():
        for src, dst in zip(cast_in, cast_out):
            dst[...] = src[...].astype(BF16)

    tk = tq
    qi = pl.program_id(2)
    row = lax.broadcasted_iota(jnp.int32, (tk, tk), 0)
    col = lax.broadcasted_iota(jnp.int32, (tk, tk), 1)
    neg_later = jnp.where(row > col, -1.0, 0.0).astype(BF16)

    def tile(kj, first):
        start = pl.multiple_of(kj * tk, tk)
        log_betas = []
        softplus = []
        for h in range(heads):
            sl = slice(h * HEAD_DIM, (h + 1) * HEAD_DIM)
            k_t = k_ref[pl.ds(start, tk), sl]
            z2 = lax.dot_general(q_ref[:, sl], k_t, (((1,), (1,)), ((), ())),
                                 preferred_element_type=F32)
            pos = jnp.maximum(z2, 0.0)
            neg = z2 - pos
            log_term = jnp.log(1.0 + jnp.exp2(neg - pos)) * LOG2_E
            sp = pos + log_term
            if first:
                sp = jnp.where(col < row, sp, 0.0)
            softplus.append(sp)
            log_betas.append(neg - log_term)
        sp_all = jnp.concatenate(softplus, axis=0).astype(BF16)
        suffix_all = jnp.dot(sp_all, neg_later, preferred_element_type=F32)
        for h in range(heads):
            sl = slice(h * HEAD_DIM, (h + 1) * HEAD_DIM)
            v_t = v_ref[pl.ds(start, tk), sl]
            suffix = suffix_all[h * tq:(h + 1) * tq]
            expo = log_betas[h] + suffix
            if not first:
                expo = expo + jnp.concatenate([tail_ref[h]] * (tk // HEAD_DIM), axis=1)
            w = jnp.exp2(expo)
            if first:
                w = jnp.where(col < row, w, 0.0)
            pv = jnp.dot(w.astype(BF16), v_t, preferred_element_type=F32)
            total = jnp.broadcast_to(suffix[:, 0:1] - softplus[h][:, 0:1], (tq, HEAD_DIM))
            if first:
                o_ref[:, sl] = pv
                tail_ref[h] = total
            else:
                o_ref[:, sl] += pv
                tail_ref[h] += total

    def stick_left():
        return jnp.max(tail_ref[...]) > F32_ZERO_LOG2

    @pl.when(qi == 0)
    def _():
        cast_weights()
        tile(qi, True)

    @pl.when(qi > 0)
    def _():
        cast_weights()
        tile(qi, True)
        tile(qi - 1, False)

    def body(carry):
        it, _ = carry
        tile(qi - 1 - it, False)
        return it + 1, stick_left()

    lax.while_loop(lambda c: jnp.logical_and(c[0] < qi, c[1]), body,
                   (jnp.int32(1), stick_left()))


def _attention(qkv, batch, seq, n_heads, later_weights, tq=256, heads=8):
    m = qkv.shape[0]
    nq = seq // tq
    ng = n_heads // heads
    width = heads * HEAD_DIM
    n_steps = batch * ng * nq
    step = lambda b, g, i: ((b * ng + g) * nq + i, 0)
    slab_specs = [pl.BlockSpec((w.shape[0] // n_steps, w.shape[1]), step) for w in later_weights]
    outs = pl.pallas_call(
        functools.partial(_attn_kernel, tq=tq, heads=heads, n_cast=len(later_weights)),
        grid=(batch, ng, nq),
        in_specs=[
            pl.BlockSpec((tq, width), lambda b, g, i: (b * nq + i, g)),
            pl.BlockSpec((seq, width), lambda b, g, i: (b, ng + g)),
            pl.BlockSpec((seq, width), lambda b, g, i: (b, 2 * ng + g)),
            *slab_specs,
        ],
        out_specs=[pl.BlockSpec((tq, width), lambda b, g, i: (b * nq + i, g)), *slab_specs],
        out_shape=[jax.ShapeDtypeStruct((m, n_heads * HEAD_DIM), F32),
                   *[jax.ShapeDtypeStruct(w.shape, BF16) for w in later_weights]],
        scratch_shapes=[pltpu.VMEM((heads, tq, HEAD_DIM), F32)],
        compiler_params=_params(3),
        name="stickbreak_attn",
    )(qkv, qkv, qkv, *later_weights)
    return outs[0], outs[1:]


def _lru_kernel(x_ref, gate_ref, cw_ref, cb_ref, wr_ref, wi_ref, br_ref, bi_ref, lam_ref,
                gn_ref, later_w_ref, o_ref, later_w_bf_ref, xbuf, a_buf, b_buf, h_buf, hcarry,
                *, tt):
    t = pl.program_id(1)
    c = x_ref.shape[1]
    n_groups = tt // SUBLANES
    grouped = (n_groups, SUBLANES, LRU_BLOCK)

    @pl.when(t == 0)
    def _():
        xbuf[0:SUBLANES, :] = jnp.zeros((SUBLANES, c), F32)
        hcarry[...] = jnp.zeros((SUBLANES, c), F32)

    xbuf[SUBLANES:SUBLANES + tt, :] = x_ref[...]
    later_w_bf_ref[...] = later_w_ref[...].astype(BF16)

    row_in_vreg = lax.broadcasted_iota(jnp.int32, grouped, 1)
    log_sig_lam = _log_sigmoid(lam_ref[...])
    log2_a_coef = (0.5 * LRU_C * LOG2_E) * log_sig_lam
    neg_log_a_coef = (-0.5 * LRU_C) * log_sig_lam
    for n in range(c // LRU_BLOCK):
        sl = slice(n * LRU_BLOCK, (n + 1) * LRU_BLOCK)
        lane_row = lambda ref, k=0: ref[k:k + 1, sl].reshape(1, 1, LRU_BLOCK)
        window = xbuf[:, sl].reshape(n_groups + 1, SUBLANES, LRU_BLOCK)
        xc = lane_row(cb_ref) + lane_row(cw_ref, CONV_WIDTH - 1) * window[1:]
        for back in range(1, CONV_WIDTH):
            rolled = pltpu.roll(window, back, 1)
            shifted = jnp.where(row_in_vreg >= back, rolled[1:], rolled[:-1])
            xc = xc + lane_row(cw_ref, CONV_WIDTH - 1 - back) * shifted
        xc = xc.reshape(tt, LRU_BLOCK)
        xcb = xc.astype(BF16)
        tanh_r = jnp.tanh(jnp.dot(xcb, wr_ref[n], preferred_element_type=F32) + br_ref[:, sl])
        tanh_i = jnp.tanh(jnp.dot(xcb, wi_ref[n], preferred_element_type=F32) + bi_ref[:, sl])
        two_r = tanh_r + 1.0
        a = jnp.exp2(two_r * log2_a_coef[:, sl])
        one_m_a2 = jnp.tanh(two_r * neg_log_a_coef[:, sl]) * (1.0 + a * a)
        mult = one_m_a2 * lax.rsqrt(jnp.maximum(one_m_a2, F32_TINY))
        b = mult * ((tanh_i + 1.0) * (0.5 * xc))
        a = a.reshape(grouped)
        b = b.reshape(grouped)
        for shift in (1, 2, 4):
            keep = row_in_vreg >= shift
            a_prev = jnp.where(keep, pltpu.roll(a, shift, 1), 1.0)
            b_prev = jnp.where(keep, pltpu.roll(b, shift, 1), 0.0)
            b = a * b_prev + b
            a = a * a_prev
        a_buf[:, sl] = a.reshape(tt, LRU_BLOCK)
        b_buf[:, sl] = b.reshape(tt, LRU_BLOCK)

    xbuf[0:SUBLANES, :] = xbuf[tt:tt + SUBLANES, :]

    def group(g, hb):
        rows = pl.ds(pl.multiple_of(g * SUBLANES, SUBLANES), SUBLANES)
        h = a_buf[rows, :] * hb + b_buf[rows, :]
        h_buf[rows, :] = h
        return jnp.broadcast_to(h[SUBLANES - 1:SUBLANES, :], (SUBLANES, c))

    hcarry[...] = lax.fori_loop(0, n_groups, group, hcarry[...])

    g = gate_ref[...]
    inner = g * (GELU_C1 + GELU_C3 * (g * g))
    y = (h_buf[...] * (0.5 * g)) * (1.0 + jnp.tanh(inner))
    o_ref[...] = _rms(y, gn_ref[...]).astype(o_ref.dtype)


def _lru_branch(lru_in, conv_w, conv_b, w_r, w_i, b_r, b_i, lam, gain, later_w, batch, seq,
                tt=512):
    m = lru_in.shape[0]
    c = lru_in.shape[1] // 2
    nt = seq // tt
    n_blocks = c // LRU_BLOCK
    row = lambda b, t: (0, 0)
    slab = pl.BlockSpec((later_w.shape[0] // (batch * nt), later_w.shape[1]),
                        lambda b, t: (b * nt + t, 0))
    return pl.pallas_call(
        functools.partial(_lru_kernel, tt=tt),
        grid=(batch, nt),
        in_specs=[
            pl.BlockSpec((tt, c), lambda b, t: (b * nt + t, 0)),
            pl.BlockSpec((tt, c), lambda b, t: (b * nt + t, 1)),
            pl.BlockSpec((CONV_WIDTH, c), row),
            pl.BlockSpec((1, c), row),
            pl.BlockSpec((n_blocks, LRU_BLOCK, LRU_BLOCK), lambda b, t: (0, 0, 0)),
            pl.BlockSpec((n_blocks, LRU_BLOCK, LRU_BLOCK), lambda b, t: (0, 0, 0)),
            pl.BlockSpec((1, c), row),
            pl.BlockSpec((1, c), row),
            pl.BlockSpec((1, c), row),
            pl.BlockSpec((1, c), row),
            slab,
        ],
        out_specs=[pl.BlockSpec((tt, c), lambda b, t: (b * nt + t, 0)), slab],
        out_shape=[jax.ShapeDtypeStruct((m, c), BF16),
                   jax.ShapeDtypeStruct(later_w.shape, BF16)],
        scratch_shapes=[
            pltpu.VMEM((tt + SUBLANES, c), F32),
            pltpu.VMEM((tt, c), F32),
            pltpu.VMEM((tt, c), F32),
            pltpu.VMEM((tt, c), F32),
            pltpu.VMEM((SUBLANES, c), F32),
        ],
        compiler_params=_params(2),
        name="rglru",
    )(lru_in, lru_in, conv_w, conv_b, w_r, w_i, b_r, b_i, lam, gain, later_w)


def _outproj_kernel(ya_ref, yl_ref, ga_ref, w_ref, x_ref, gpost_ref, x1_ref):
    ya = _rms(ya_ref[...], ga_ref[...]).astype(BF16)
    y = jnp.concatenate([ya, yl_ref[...]], axis=1)
    mix = jnp.dot(y, w_ref[...], preferred_element_type=F32)
    x1_ref[...] = x_ref[...] + _rms(mix, gpost_ref[...])


def _outproj(y_attn, y_lru, attn_gain, w_bf, x2, post_gain, tm=512):
    m, d = x2.shape
    ca = y_attn.shape[1]
    cl = y_lru.shape[1]
    row = lambda i: (0, 0)
    return pl.pallas_call(
        _outproj_kernel,
        grid=(m // tm,),
        in_specs=[
            pl.BlockSpec((tm, ca), lambda i: (i, 0)),
            pl.BlockSpec((tm, cl), lambda i: (i, 0)),
            pl.BlockSpec((1, ca), row),
            pl.BlockSpec((ca + cl, d), row, pipeline_mode=pl.Buffered(1)),
            pl.BlockSpec((tm, d), lambda i: (i, 0)),
            pl.BlockSpec((1, d), row),
        ],
        out_specs=pl.BlockSpec((tm, d), lambda i: (i, 0)),
        out_shape=jax.ShapeDtypeStruct((m, d), F32),
        compiler_params=_params(1),
        name="outproj",
    )(y_attn, y_lru, attn_gain, w_bf, x2, post_gain)


def _ffn_kernel(x1_ref, gpre_ref, wg_ref, wu_ref, wd_ref, gpost_ref, o_ref, hn_ref):
    f = pl.program_id(1)

    row_chunks = [slice(r0, r0 + NORM_ROWS) for r0 in range(0, o_ref.shape[0], NORM_ROWS)]

    @pl.when(f == 0)
    def _():
        for rows in row_chunks:
            hn_ref[rows, :] = _rms(x1_ref[rows, :], gpre_ref[...]).astype(BF16)
        o_ref[...] = jnp.zeros(o_ref.shape, F32)

    hn = hn_ref[...]
    for c0 in range(0, wg_ref.shape[1], FFN_SLAB):
        cols = slice(c0, c0 + FFN_SLAB)
        gate = jnp.dot(hn, wg_ref[:, cols], preferred_element_type=F32)
        up = jnp.dot(hn, wu_ref[:, cols], preferred_element_type=F32)
        act = (jax.nn.silu(gate) * up).astype(BF16)
        o_ref[...] += jnp.dot(act, wd_ref[cols, :], preferred_element_type=F32)

    @pl.when(f == pl.num_programs(1) - 1)
    def _():
        for rows in row_chunks:
            o_ref[rows, :] = x1_ref[rows, :] + _rms(o_ref[rows, :], gpost_ref[...])


def _ffn(x1, pre_gain, wg, wu, wd, post_gain, tm=1024, tf=512):
    m, d = x1.shape
    dff = wg.shape[1]
    return pl.pallas_call(
        _ffn_kernel,
        grid=(m // tm, dff // tf),
        in_specs=[
            pl.BlockSpec((tm, d), lambda i, f: (i, 0)),
            pl.BlockSpec((1, d), lambda i, f: (0, 0)),
            pl.BlockSpec((d, tf), lambda i, f: (0, f)),
            pl.BlockSpec((d, tf), lambda i, f: (0, f)),
            pl.BlockSpec((tf, d), lambda i, f: (f, 0)),
            pl.BlockSpec((1, d), lambda i, f: (0, 0)),
        ],
        out_specs=pl.BlockSpec((tm, d), lambda i, f: (i, 0)),
        out_shape=jax.ShapeDtypeStruct((m, d), F32),
        scratch_shapes=[pltpu.VMEM((tm, d), BF16)],
        compiler_params=_params(2),
        name="swiglu_ffn",
    )(x1, pre_gain, wg, wu, wd, post_gain)


def _layer(x2, batch, seq, pre_mix, post_mix, pre_ffn, post_ffn, w_in, conv_w, conv_b, w_r, b_r,
           w_i, b_i, lam, attn_gain, lru_gain, w_out, w_g, w_u, w_d):
    row = lambda v: v.reshape(1, -1)
    lru_width = conv_w.shape[1]
    qkv_width = w_in.shape[1] - 2 * lru_width
    n_heads = qkv_width // (3 * HEAD_DIM)

    qkv, lru_in = _inproj(x2, row(pre_mix), w_in, qkv_width)
    y_attn, (w_out_bf, w_g_bf, w_u_bf) = _attention(qkv, batch, seq, n_heads, (w_out, w_g, w_u))
    y_lru, w_d_bf = _lru_branch(lru_in, conv_w, row(conv_b), (0.5 * w_r).astype(BF16),
                                (0.5 * w_i).astype(BF16), row(0.5 * b_r), row(0.5 * b_i),
                                row(lam), row(lru_gain), w_d, batch, seq)
    x1 = _outproj(y_attn, y_lru, row(attn_gain), w_out_bf, x2, row(post_mix))
    return _ffn(x1, row(pre_ffn), w_g_bf, w_u_bf, w_d_bf, row(post_ffn))


def kernel(x, pre_mix_norm, post_mix_norm, pre_ffn_norm, post_ffn_norm, w_in, conv_w, conv_b,
           w_rgate, b_rgate, w_igate, b_igate, lru_lambda, attn_out_norm, lru_out_norm, w_out,
           w_ffn_gate, w_ffn_up, w_ffn_down):
    batch, seq, d = x.shape
    x2 = x.reshape(batch * seq, d)
    for l in range(w_in.shape[0]):
        x2 = _layer(x2, batch, seq, pre_mix_norm[l], post_mix_norm[l], pre_ffn_norm[l],
                    post_ffn_norm[l], w_in[l], conv_w[l], conv_b[l], w_rgate[l], b_rgate[l],
                    w_igate[l], b_igate[l], lru_lambda[l], attn_out_norm[l], lru_out_norm[l],
                    w_out[l], w_ffn_gate[l], w_ffn_up[l], w_ffn_down[l])
    return x2.reshape(batch, seq, d)
```

```python
import functools
import math

import jax
import jax.numpy as jnp
from jax import lax
from jax.experimental import pallas as pl
from jax.experimental.pallas import tpu as pltpu

F32 = jnp.float32
BF16 = jnp.bfloat16

HEAD_DIM = 128
LRU_BLOCK = 128
CONV_WIDTH = 4
LRU_C = 8.0
RMS_EPS = 1e-6
LOG2_E = math.log2(math.e)
Q_SCALE = LOG2_E / math.sqrt(HEAD_DIM)
F32_TINY = float(jnp.finfo(jnp.float32).tiny)
GELU_C1 = math.sqrt(2.0 / math.pi)
GELU_C3 = GELU_C1 * 0.044715
F32_ZERO_LOG2 = -150.0
SUBLANES = 8
NORM_ROWS = 128
FFN_SLAB = 256
VMEM_LIMIT = 60 * 1024 * 1024


def _params(n_axes):
    return pltpu.CompilerParams(dimension_semantics=("arbitrary",) * n_axes,
                                vmem_limit_bytes=VMEM_LIMIT)


def _rms(x, gain):
    var = jnp.mean(x * x, axis=-1, keepdims=True)
    return (x * lax.rsqrt(var + RMS_EPS)) * gain


def _log_sigmoid(z):
    return jnp.minimum(z, 0.0) - jnp.log1p(jnp.exp(-jnp.abs(z)))


def _inproj_kernel(x_ref, g_ref, w_ref, qkv_ref, lru_ref, hn_ref, *, n_q_tiles):
    j = pl.program_id(1)

    @pl.when(j == 0)
    def _():
        for r0 in range(0, x_ref.shape[0], NORM_ROWS):
            rows = slice(r0, r0 + NORM_ROWS)
            hn_ref[rows, :] = _rms(x_ref[rows, :], g_ref[...]).astype(BF16)

    acc = jnp.dot(hn_ref[...], w_ref[...].astype(BF16), preferred_element_type=F32)
    qkv_ref[...] = (acc * jnp.where(j < n_q_tiles, Q_SCALE, 1.0)).astype(BF16)
    lru_ref[...] = acc


def _inproj(x2, gain, w, qkv_width, tm=1024, tn=1024):
    m, d = x2.shape
    n = w.shape[1]
    lru_width = n - qkv_width
    n_qkv_tiles = qkv_width // tn
    n_q_tiles = qkv_width // (3 * tn)
    grid = (m // tm, n // tn)
    return pl.pallas_call(
        functools.partial(_inproj_kernel, n_q_tiles=n_q_tiles),
        grid=grid,
        in_specs=[
            pl.BlockSpec((tm, d), lambda i, j: (i, 0)),
            pl.BlockSpec((1, d), lambda i, j: (0, 0)),
            pl.BlockSpec((d, tn), lambda i, j: (0, j)),
        ],
        out_specs=[
            pl.BlockSpec((tm, tn), lambda i, j: (i, jnp.minimum(j, n_qkv_tiles))),
            pl.BlockSpec((tm, tn), lambda i, j: (i, jnp.maximum(j - n_qkv_tiles, 0))),
        ],
        out_shape=[
            jax.ShapeDtypeStruct((m, qkv_width + tn), BF16),
            jax.ShapeDtypeStruct((m, lru_width), F32),
        ],
        scratch_shapes=[pltpu.VMEM((tm, d), BF16)],
        compiler_params=_params(2),
        name="inproj",
    )(x2, gain, w)


def _attn_kernel(q_ref, k_ref, v_ref, *refs, tq, heads, n_cast):
    cast_in, o_ref = refs[:n_cast], refs[n_cast]
    cast_out, tail_ref = refs[n_cast + 1:2 * n_cast + 1], refs[2 * n_cast + 1]

    def cast_weights():
        for src, dst in zip(cast_in, cast_out):
            dst[...] = src[...].astype(BF16)

    tk = tq
    qi = pl.program_id(2)
    row = lax.broadcasted_iota(jnp.int32, (tk, tk), 0)
    col = lax.broadcasted_iota(jnp.int32, (tk, tk), 1)
    neg_later = jnp.where(row > col, -1.0, 0.0).astype(BF16)

    def tile(kj, first):
        start = pl.multiple_of(kj * tk, tk)
        log_betas = []
        softplus = []
        for h in range(heads):
            sl = slice(h * HEAD_DIM, (h + 1) * HEAD_DIM)
            k_t = k_ref[pl.ds(start, tk), sl]
            z2 = lax.dot_general(q_ref[:, sl], k_t, (((1,), (1,)), ((), ())),
                                 preferred_element_type=F32)
            pos = jnp.maximum(z2, 0.0)
            neg = z2 - pos
            log_term = jnp.log(1.0 + jnp.exp2(neg - pos)) * LOG2_E
            sp = pos + log_term
            if first:
                sp = jnp.where(col < row, sp, 0.0)
            softplus.append(sp)
            log_betas.append(neg - log_term)
        sp_all = jnp.concatenate(softplus, axis=0).astype(BF16)
        suffix_all = jnp.dot(sp_all, neg_later, preferred_element_type=F32)
        for h in range(heads):
            sl = slice(h * HEAD_DIM, (h + 1) * HEAD_DIM)
            v_t = v_ref[pl.ds(start, tk), sl]
            suffix = suffix_all[h * tq:(h + 1) * tq]
            expo = log_betas[h] + suffix
            if not first:
                expo = expo + jnp.concatenate([tail_ref[h]] * (tk // HEAD_DIM), axis=1)
            w = jnp.exp2(expo)
            if first:
                w = jnp.where(col < row, w, 0.0)
            pv = jnp.dot(w.astype(BF16), v_t, preferred_element_type=F32)
            total = jnp.broadcast_to(suffix[:, 0:1] - softplus[h][:, 0:1], (tq, HEAD_DIM))
            if first:
                o_ref[:, sl] = pv
                tail_ref[h] = total
            else:
                o_ref[:, sl] += pv
                tail_ref[h] += total

    def stick_left():
        return jnp.max(tail_ref[...]) > F32_ZERO_LOG2

    @pl.when(qi == 0)
    def _():
        cast_weights()
        tile(qi, True)

    @pl.when(qi > 0)
    def _():
        cast_weights()
        tile(qi, True)
        tile(qi - 1, False)

    def body(carry):
        it, _ = carry
        tile(qi - 1 - it, False)
        return it + 1, stick_left()

    lax.while_loop(lambda c: jnp.logical_and(c[0] < qi, c[1]), body,
                   (jnp.int32(1), stick_left()))


def _attention(qkv, batch, seq, n_heads, later_weights, tq=256, heads=8):
    m = qkv.shape[0]
    nq = seq // tq
    ng = n_heads // heads
    width = heads * HEAD_DIM
    n_steps = batch * ng * nq
    step = lambda b, g, i: ((b * ng + g) * nq + i, 0)
    slab_specs = [pl.BlockSpec((w.shape[0] // n_steps, w.shape[1]), step) for w in later_weights]
    outs = pl.pallas_call(
        functools.partial(_attn_kernel, tq=tq, heads=heads, n_cast=len(later_weights)),
        grid=(batch, ng, nq),
        in_specs=[
            pl.BlockSpec((tq, width), lambda b, g, i: (b * nq + i, g)),
            pl.BlockSpec((seq, width), lambda b, g, i: (b, ng + g)),
            pl.BlockSpec((seq, width), lambda b, g, i: (b, 2 * ng + g)),
            *slab_specs,
        ],
        out_specs=[pl.BlockSpec((tq, width), lambda b, g, i: (b * nq + i, g)), *slab_specs],
        out_shape=[jax.ShapeDtypeStruct((m, n_heads * HEAD_DIM), F32),
                   *[jax.ShapeDtypeStruct(w.shape, BF16) for w in later_weights]],
        scratch_shapes=[pltpu.VMEM((heads, tq, HEAD_DIM), F32)],
        compiler_params=_params(3),
        name="stickbreak_attn",
    )(qkv, qkv, qkv, *later_weights)
    return outs[0], outs[1:]


def _lru_kernel(x_ref, gate_ref, cw_ref, cb_ref, wr_ref, wi_ref, br_ref, bi_ref, lam_ref,
                gn_ref, later_w_ref, o_ref, later_w_bf_ref, xbuf, a_buf, b_buf, h_buf, hcarry,
                *, tt):
    t = pl.program_id(1)
    c = x_ref.shape[1]
    n_groups = tt // SUBLANES
    grouped = (n_groups, SUBLANES, LRU_BLOCK)

    @pl.when(t == 0)
    def _():
        xbuf[0:SUBLANES, :] = jnp.zeros((SUBLANES, c), F32)
        hcarry[...] = jnp.zeros((SUBLANES, c), F32)

    xbuf[SUBLANES:SUBLANES + tt, :] = x_ref[...]
    later_w_bf_ref[...] = later_w_ref[...].astype(BF16)

    row_in_vreg = lax.broadcasted_iota(jnp.int32, grouped, 1)
    log_sig_lam = _log_sigmoid(lam_ref[...])
    log2_a_coef = (0.5 * LRU_C * LOG2_E) * log_sig_lam
    neg_log_a_coef = (-0.5 * LRU_C) * log_sig_lam
    for n in range(c // LRU_BLOCK):
        sl = slice(n * LRU_BLOCK, (n + 1) * LRU_BLOCK)
        lane_row = lambda ref, k=0: ref[k:k + 1, sl].reshape(1, 1, LRU_BLOCK)
        window = xbuf[:, sl].reshape(n_groups + 1, SUBLANES, LRU_BLOCK)
        xc = lane_row(cb_ref) + lane_row(cw_ref, CONV_WIDTH - 1) * window[1:]
        for back in range(1, CONV_WIDTH):
            rolled = pltpu.roll(window, back, 1)
            shifted = jnp.where(row_in_vreg >= back, rolled[1:], rolled[:-1])
            xc = xc + lane_row(cw_ref, CONV_WIDTH - 1 - back) * shifted
        xc = xc.reshape(tt, LRU_BLOCK)
        xcb = xc.astype(BF16)
        tanh_r = jnp.tanh(jnp.dot(xcb, wr_ref[n], preferred_element_type=F32) + br_ref[:, sl])
        tanh_i = jnp.tanh(jnp.dot(xcb, wi_ref[n], preferred_element_type=F32) + bi_ref[:, sl])
        two_r = tanh_r + 1.0
        a = jnp.exp2(two_r * log2_a_coef[:, sl])
        one_m_a2 = jnp.tanh(two_r * neg_log_a_coef[:, sl]) * (1.0 + a * a)
        mult = one_m_a2 * lax.rsqrt(jnp.maximum(one_m_a2, F32_TINY))
        b = mult * ((tanh_i + 1.0) * (0.5 * xc))
        a = a.reshape(grouped)
        b = b.reshape(grouped)
        for shift in (1, 2, 4):
            keep = row_in_vreg >= shift
            a_prev = jnp.where(keep, pltpu.roll(a, shift, 1), 1.0)
            b_prev = jnp.where(keep, pltpu.roll(b, shift, 1), 0.0)
            b = a * b_prev + b
            a = a * a_prev
        a_buf[:, sl] = a.reshape(tt, LRU_BLOCK)
        b_buf[:, sl] = b.reshape(tt, LRU_BLOCK)

    xbuf[0:SUBLANES, :] = xbuf[tt:tt + SUBLANES, :]

    def group(g, hb):
        rows = pl.ds(pl.multiple_of(g * SUBLANES, SUBLANES), SUBLANES)
        h = a_buf[rows, :] * hb + b_buf[rows, :]
        h_buf[rows, :] = h
        return jnp.broadcast_to(h[SUBLANES - 1:SUBLANES, :], (SUBLANES, c))

    hcarry[...] = lax.fori_loop(0, n_groups, group, hcarry[...])

    g = gate_ref[...]
    inner = g * (GELU_C1 + GELU_C3 * (g * g))
    y = (h_buf[...] * (0.5 * g)) * (1.0 + jnp.tanh(inner))
    o_ref[...] = _rms(y, gn_ref[...]).astype(o_ref.dtype)


def _lru_branch(lru_in, conv_w, conv_b, w_r, w_i, b_r, b_i, lam, gain, later_w, batch, seq,
                tt=512):
    m = lru_in.shape[0]
    c = lru_in.shape[1] // 2
    nt = seq // tt
    n_blocks = c // LRU_BLOCK
    row = lambda b, t: (0, 0)
    slab = pl.BlockSpec((later_w.shape[0] // (batch * nt), later_w.shape[1]),
                        lambda b, t: (b * nt + t, 0))
    return pl.pallas_call(
        functools.partial(_lru_kernel, tt=tt),
        grid=(batch, nt),
        in_specs=[
            pl.BlockSpec((tt, c), lambda b, t: (b * nt + t, 0)),
            pl.BlockSpec((tt, c), lambda b, t: (b * nt + t, 1)),
            pl.BlockSpec((CONV_WIDTH, c), row),
            pl.BlockSpec((1, c), row),
            pl.BlockSpec((n_blocks, LRU_BLOCK, LRU_BLOCK), lambda b, t: (0, 0, 0)),
            pl.BlockSpec((n_blocks, LRU_BLOCK, LRU_BLOCK), lambda b, t: (0, 0, 0)),
            pl.BlockSpec((1, c), row),
            pl.BlockSpec((1, c), row),
            pl.BlockSpec((1, c), row),
            pl.BlockSpec((1, c), row),
            slab,
        ],
        out_specs=[pl.BlockSpec((tt, c), lambda b, t: (b * nt + t, 0)), slab],
        out_shape=[jax.ShapeDtypeStruct((m, c), BF16),
                   jax.ShapeDtypeStruct(later_w.shape, BF16)],
        scratch_shapes=[
            pltpu.VMEM((tt + SUBLANES, c), F32),
            pltpu.VMEM((tt, c), F32),
            pltpu.VMEM((tt, c), F32),
            pltpu.VMEM((tt, c), F32),
            pltpu.VMEM((SUBLANES, c), F32),
        ],
        compiler_params=_params(2),
        name="rglru",
    )(lru_in, lru_in, conv_w, conv_b, w_r, w_i, b_r, b_i, lam, gain, later_w)


def _outproj_kernel(ya_ref, yl_ref, ga_ref, w_ref, x_ref, gpost_ref, x1_ref):
    ya = _rms(ya_ref[...], ga_ref[...]).astype(BF16)
    y = jnp.concatenate([ya, yl_ref[...]], axis=1)
    mix = jnp.dot(y, w_ref[...], preferred_element_type=F32)
    x1_ref[...] = x_ref[...] + _rms(mix, gpost_ref[...])


def _outproj(y_attn, y_lru, attn_gain, w_bf, x2, post_gain, tm=512):
    m, d = x2.shape
    ca = y_attn.shape[1]
    cl = y_lru.shape[1]
    row = lambda i: (0, 0)
    return pl.pallas_call(
        _outproj_kernel,
        grid=(m // tm,),
        in_specs=[
            pl.BlockSpec((tm, ca), lambda i: (i, 0)),
            pl.BlockSpec((tm, cl), lambda i: (i, 0)),
            pl.BlockSpec((1, ca), row),
            pl.BlockSpec((ca + cl, d), row, pipeline_mode=pl.Buffered(1)),
            pl.BlockSpec((tm, d), lambda i: (i, 0)),
            pl.BlockSpec((1, d), row),
        ],
        out_specs=pl.BlockSpec((tm, d), lambda i: (i, 0)),
        out_shape=jax.ShapeDtypeStruct((m, d), F32),
        compiler_params=_params(1),
        name="outproj",
    )(y_attn, y_lru, attn_gain, w_bf, x2, post_gain)


def _ffn_kernel(x1_ref, gpre_ref, wg_ref, wu_ref, wd_ref, gpost_ref, o_ref, hn_ref):
    f = pl.program_id(1)

    row_chunks = [slice(r0, r0 + NORM_ROWS) for r0 in range(0, o_ref.shape[0], NORM_ROWS)]

    @pl.when(f == 0)
    def _():
        for rows in row_chunks:
            hn_ref[rows, :] = _rms(x1_ref[rows, :], gpre_ref[...]).astype(BF16)
        o_ref[...] = jnp.zeros(o_ref.shape, F32)

    hn = hn_ref[...]
    for c0 in range(0, wg_ref.shape[1], FFN_SLAB):
        cols = slice(c0, c0 + FFN_SLAB)
        gate = jnp.dot(hn, wg_ref[:, cols], preferred_element_type=F32)
        up = jnp.dot(hn, wu_ref[:, cols], preferred_element_type=F32)
        act = (jax.nn.silu(gate) * up).astype(BF16)
        o_ref[...] += jnp.dot(act, wd_ref[cols, :], preferred_element_type=F32)

    @pl.when(f == pl.num_programs(1) - 1)
    def _():
        for rows in row_chunks:
            o_ref[rows, :] = x1_ref[rows, :] + _rms(o_ref[rows, :], gpost_ref[...])


def _ffn(x1, pre_gain, wg, wu, wd, post_gain, tm=1024, tf=512):
    m, d = x1.shape
    dff = wg.shape[1]
    return pl.pallas_call(
        _ffn_kernel,
        grid=(m // tm, dff // tf),
        in_specs=[
            pl.BlockSpec((tm, d), lambda i, f: (i, 0)),
            pl.BlockSpec((1, d), lambda i, f: (0, 0)),
            pl.BlockSpec((d, tf), lambda i, f: (0, f)),
            pl.BlockSpec((d, tf), lambda i, f: (0, f)),
            pl.BlockSpec((tf, d), lambda i, f: (f, 0)),
            pl.BlockSpec((1, d), lambda i, f: (0, 0)),
        ],
        out_specs=pl.BlockSpec((tm, d), lambda i, f: (i, 0)),
        out_shape=jax.ShapeDtypeStruct((m, d), F32),
        scratch_shapes=[pltpu.VMEM((tm, d), BF16)],
        compiler_params=_params(2),
        name="swiglu_ffn",
    )(x1, pre_gain, wg, wu, wd, post_gain)


def _layer(x2, batch, seq, pre_mix, post_mix, pre_ffn, post_ffn, w_in, conv_w, conv_b, w_r, b_r,
           w_i, b_i, lam, attn_gain, lru_gain, w_out, w_g, w_u, w_d):
    row = lambda v: v.reshape(1, -1)
    lru_width = conv_w.shape[1]
    qkv_width = w_in.shape[1] - 2 * lru_width
    n_heads = qkv_width // (3 * HEAD_DIM)

    qkv, lru_in = _inproj(x2, row(pre_mix), w_in, qkv_width)
    y_attn, (w_out_bf, w_g_bf, w_u_bf) = _attention(qkv, batch, seq, n_heads, (w_out, w_g, w_u))
    y_lru, w_d_bf = _lru_branch(lru_in, conv_w, row(conv_b), (0.5 * w_r).astype(BF16),
                                (0.5 * w_i).astype(BF16), row(0.5 * b_r), row(0.5 * b_i),
                                row(lam), row(lru_gain), w_d, batch, seq)
    x1 = _outproj(y_attn, y_lru, row(attn_gain), w_out_bf, x2, row(post_mix))
    return _ffn(x1, row(pre_ffn), w_g_bf, w_u_bf, w_d_bf, row(post_ffn))


def kernel(x, pre_mix_norm, post_mix_norm, pre_ffn_norm, post_ffn_norm, w_in, conv_w, conv_b,
           w_rgate, b_rgate, w_igate, b_igate, lru_lambda, attn_out_norm, lru_out_norm, w_out,
           w_ffn_gate, w_ffn_up, w_ffn_down):
    batch, seq, d = x.shape
    x2 = x.reshape(batch * seq, d)
    for l in range(w_in.shape[0]):
        x2 = _layer(x2, batch, seq, pre_mix_norm[l], post_mix_norm[l], pre_ffn_norm[l],
                    post_ffn_norm[l], w_in[l], conv_w[l], conv_b[l], w_rgate[l], b_rgate[l],
                    w_igate[l], b_igate[l], lru_lambda[l], attn_out_norm[l], lru_out_norm[l],
                    w_out[l], w_ffn_gate[l], w_ffn_up[l], w_ffn_down[l])
    return x2.reshape(batch, seq, d)
```

```python
import functools
import math

import jax
import jax.numpy as jnp
from jax import lax
from jax.experimental import pallas as pl
from jax.experimental.pallas import tpu as pltpu

F32 = jnp.float32
BF16 = jnp.bfloat16

HEAD_DIM = 128
LRU_BLOCK = 128
CONV_WIDTH = 4
LRU_C = 8.0
RMS_EPS = 1e-6
LOG2_E = math.log2(math.e)
Q_SCALE = LOG2_E / math.sqrt(HEAD_DIM)
F32_TINY = float(jnp.finfo(jnp.float32).tiny)
GELU_C1 = math.sqrt(2.0 / math.pi)
GELU_C3 = GELU_C1 * 0.044715
F32_ZERO_LOG2 = -150.0
SUBLANES = 8
NORM_ROWS = 128
FFN_SLAB = 256
VMEM_LIMIT = 60 * 1024 * 1024


def _params(n_axes):
    return pltpu.CompilerParams(dimension_semantics=("arbitrary",) * n_axes,
                                vmem_limit_bytes=VMEM_LIMIT)


def _rms(x, gain):
    var = jnp.mean(x * x, axis=-1, keepdims=True)
    return (x * lax.rsqrt(var + RMS_EPS)) * gain


def _log_sigmoid(z):
    return jnp.minimum(z, 0.0) - jnp.log1p(jnp.exp(-jnp.abs(z)))


def _inproj_kernel(x_hbm, g_ref, w_ref, qkv_ref, lru_ref, hn_ref, xbuf, sem, *, n_q_tiles):
    i, j = pl.program_id(0), pl.program_id(1)
    tm = xbuf.shape[0]

    def fetch(tile):
        return pltpu.make_async_copy(x_hbm.at[pl.ds(tile * tm, tm), :], xbuf, sem)

    @pl.when(jnp.logical_and(i == 0, j == 0))
    def _():
        fetch(0).start()

    @pl.when(j == 0)
    def _():
        fetch(i).wait()
        for r0 in range(0, tm, NORM_ROWS):
            rows = slice(r0, r0 + NORM_ROWS)
            hn_ref[rows, :] = _rms(xbuf[rows, :], g_ref[...]).astype(BF16)

    @pl.when(jnp.logical_and(j == 1, i + 1 < pl.num_programs(0)))
    def _():
        fetch(i + 1).start()

    acc = jnp.dot(hn_ref[...], w_ref[...].astype(BF16), preferred_element_type=F32)
    qkv_ref[...] = (acc * jnp.where(j < n_q_tiles, Q_SCALE, 1.0)).astype(BF16)
    lru_ref[...] = acc


def _inproj(x2, gain, w, qkv_width, tm=2048, tn=512):
    m, d = x2.shape
    n = w.shape[1]
    lru_width = n - qkv_width
    n_qkv_tiles = qkv_width // tn
    n_q_tiles = qkv_width // (3 * tn)
    grid = (m // tm, n // tn)
    return pl.pallas_call(
        functools.partial(_inproj_kernel, n_q_tiles=n_q_tiles),
        grid=grid,
        in_specs=[
            pl.BlockSpec(memory_space=pl.ANY),
            pl.BlockSpec((1, d), lambda i, j: (0, 0)),
            pl.BlockSpec((d, tn), lambda i, j: (0, j)),
        ],
        out_specs=[
            pl.BlockSpec((tm, tn), lambda i, j: (i, jnp.minimum(j, n_qkv_tiles))),
            pl.BlockSpec((tm, tn), lambda i, j: (i, jnp.maximum(j - n_qkv_tiles, 0))),
        ],
        out_shape=[
            jax.ShapeDtypeStruct((m, qkv_width + tn), BF16),
            jax.ShapeDtypeStruct((m, lru_width), F32),
        ],
        scratch_shapes=[pltpu.VMEM((tm, d), BF16), pltpu.VMEM((tm, d), F32),
                        pltpu.SemaphoreType.DMA(())],
        compiler_params=_params(2),
        name="inproj",
    )(x2, gain, w)


def _attn_kernel(q_ref, k_ref, v_ref, *refs, tq, heads, n_cast):
    cast_in, o_ref = refs[:n_cast], refs[n_cast]
    cast_out, tail_ref = refs[n_cast + 1:2 * n_cast + 1], refs[2 * n_cast + 1]

    def cast_weights():
        for src, dst in zip(cast_in, cast_out):
            dst[...] = src[...].astype(BF16)

    tk = tq
    qi = pl.program_id(2)
    row = lax.broadcasted_iota(jnp.int32, (tk, tk), 0)
    col = lax.broadcasted_iota(jnp.int32, (tk, tk), 1)
    neg_later = jnp.where(row > col, -1.0, 0.0).astype(BF16)

    def tile(kj, first):
        start = pl.multiple_of(kj * tk, tk)
        log_betas = []
        softplus = []
        for h in range(heads):
            sl = slice(h * HEAD_DIM, (h + 1) * HEAD_DIM)
            k_t = k_ref[pl.ds(start, tk), sl]
            z2 = lax.dot_general(q_ref[:, sl], k_t, (((1,), (1,)), ((), ())),
                                 preferred_element_type=F32)
            pos = jnp.maximum(z2, 0.0)
            neg = z2 - pos
            log_term = jnp.log(1.0 + jnp.exp2(neg - pos)) * LOG2_E
            sp = pos + log_term
            if first:
                sp = jnp.where(col < row, sp, 0.0)
            softplus.append(sp)
            log_betas.append(neg - log_term)
        sp_all = jnp.concatenate(softplus, axis=0).astype(BF16)
        suffix_all = jnp.dot(sp_all, neg_later, preferred_element_type=F32)
        for h in range(heads):
            sl = slice(h * HEAD_DIM, (h + 1) * HEAD_DIM)
            v_t = v_ref[pl.ds(start, tk), sl]
            suffix = suffix_all[h * tq:(h + 1) * tq]
            expo = log_betas[h] + suffix
            if not first:
                expo = expo + jnp.concatenate([tail_ref[h]] * (tk // HEAD_DIM), axis=1)
            w = jnp.exp2(expo)
            if first:
                w = jnp.where(col < row, w, 0.0)
            pv = jnp.dot(w.astype(BF16), v_t, preferred_element_type=F32)
            total = jnp.broadcast_to(suffix[:, 0:1] - softplus[h][:, 0:1], (tq, HEAD_DIM))
            if first:
                o_ref[:, sl] = pv
                tail_ref[h] = total
            else:
                o_ref[:, sl] += pv
                tail_ref[h] += total

    def stick_left():
        return jnp.max(tail_ref[...]) > F32_ZERO_LOG2

    @pl.when(qi == 0)
    def _():
        cast_weights()
        tile(qi, True)

    @pl.when(qi > 0)
    def _():
        cast_weights()
        tile(qi, True)
        tile(qi - 1, False)

    def body(carry):
        it, _ = carry
        tile(qi - 1 - it, False)
        return it + 1, stick_left()

    lax.while_loop(lambda c: jnp.logical_and(c[0] < qi, c[1]), body,
                   (jnp.int32(1), stick_left()))


def _attention(qkv, batch, seq, n_heads, later_weights, tq=256, heads=8):
    m = qkv.shape[0]
    nq = seq // tq
    ng = n_heads // heads
    width = heads * HEAD_DIM
    n_steps = batch * ng * nq
    step = lambda b, g, i: ((b * ng + g) * nq + i, 0)
    slab_specs = [pl.BlockSpec((w.shape[0] // n_steps, w.shape[1]), step) for w in later_weights]
    outs = pl.pallas_call(
        functools.partial(_attn_kernel, tq=tq, heads=heads, n_cast=len(later_weights)),
        grid=(batch, ng, nq),
        in_specs=[
            pl.BlockSpec((tq, width), lambda b, g, i: (b * nq + i, g)),
            pl.BlockSpec((seq, width), lambda b, g, i: (b, ng + g)),
            pl.BlockSpec((seq, width), lambda b, g, i: (b, 2 * ng + g)),
            *slab_specs,
        ],
        out_specs=[pl.BlockSpec((tq, width), lambda b, g, i: (b * nq + i, g)), *slab_specs],
        out_shape=[jax.ShapeDtypeStruct((m, n_heads * HEAD_DIM), F32),
                   *[jax.ShapeDtypeStruct(w.shape, BF16) for w in later_weights]],
        scratch_shapes=[pltpu.VMEM((heads, tq, HEAD_DIM), F32)],
        compiler_params=_params(3),
        name="stickbreak_attn",
    )(qkv, qkv, qkv, *later_weights)
    return outs[0], outs[1:]


def _lru_kernel(x_ref, gate_ref, cw_ref, cb_ref, wr_ref, wi_ref, br_ref, bi_ref, lam_ref,
                gn_ref, later_w_ref, o_ref, later_w_bf_ref, xbuf, a_buf, b_buf, h_buf, hcarry,
                *, tt):
    t = pl.program_id(1)
    c = x_ref.shape[1]
    n_groups = tt // SUBLANES
    grouped = (n_groups, SUBLANES, LRU_BLOCK)

    @pl.when(t == 0)
    def _():
        xbuf[0:SUBLANES, :] = jnp.zeros((SUBLANES, c), F32)
        hcarry[...] = jnp.zeros((SUBLANES, c), F32)

    xbuf[SUBLANES:SUBLANES + tt, :] = x_ref[...]
    later_w_bf_ref[...] = later_w_ref[...].astype(BF16)

    row_in_vreg = lax.broadcasted_iota(jnp.int32, grouped, 1)
    log_sig_lam = _log_sigmoid(lam_ref[...])
    log2_a_coef = (0.5 * LRU_C * LOG2_E) * log_sig_lam
    neg_log_a_coef = (-0.5 * LRU_C) * log_sig_lam
    for n in range(c // LRU_BLOCK):
        sl = slice(n * LRU_BLOCK, (n + 1) * LRU_BLOCK)
        lane_row = lambda ref, k=0: ref[k:k + 1, sl].reshape(1, 1, LRU_BLOCK)
        window = xbuf[:, sl].reshape(n_groups + 1, SUBLANES, LRU_BLOCK)
        xc = lane_row(cb_ref) + lane_row(cw_ref, CONV_WIDTH - 1) * window[1:]
        for back in range(1, CONV_WIDTH):
            rolled = pltpu.roll(window, back, 1)
            shifted = jnp.where(row_in_vreg >= back, rolled[1:], rolled[:-1])
            xc = xc + lane_row(cw_ref, CONV_WIDTH - 1 - back) * shifted
        xc = xc.reshape(tt, LRU_BLOCK)
        xcb = xc.astype(BF16)
        tanh_r = jnp.tanh(jnp.dot(xcb, wr_ref[n], preferred_element_type=F32) + br_ref[:, sl])
        tanh_i = jnp.tanh(jnp.dot(xcb, wi_ref[n], preferred_element_type=F32) + bi_ref[:, sl])
        two_r = tanh_r + 1.0
        a = jnp.exp2(two_r * log2_a_coef[:, sl])
        one_m_a2 = jnp.tanh(two_r * neg_log_a_coef[:, sl]) * (1.0 + a * a)
        mult = one_m_a2 * lax.rsqrt(jnp.maximum(one_m_a2, F32_TINY))
        b = mult * ((tanh_i + 1.0) * (0.5 * xc))
        a = a.reshape(grouped)
        b = b.reshape(grouped)
        for shift in (1, 2, 4):
            keep = row_in_vreg >= shift
            a_prev = jnp.where(keep, pltpu.roll(a, shift, 1), 1.0)
            b_prev = jnp.where(keep, pltpu.roll(b, shift, 1), 0.0)
            b = a * b_prev + b
            a = a * a_prev
        a_buf[:, sl] = a.reshape(tt, LRU_BLOCK)
        b_buf[:, sl] = b.reshape(tt, LRU_BLOCK)

    xbuf[0:SUBLANES, :] = xbuf[tt:tt + SUBLANES, :]

    def group(g, hb):
        rows = pl.ds(pl.multiple_of(g * SUBLANES, SUBLANES), SUBLANES)
        h = a_buf[rows, :] * hb + b_buf[rows, :]
        h_buf[rows, :] = h
        return jnp.broadcast_to(h[SUBLANES - 1:SUBLANES, :], (SUBLANES, c))

    hcarry[...] = lax.fori_loop(0, n_groups, group, hcarry[...])

    g = gate_ref[...]
    inner = g * (GELU_C1 + GELU_C3 * (g * g))
    y = (h_buf[...] * (0.5 * g)) * (1.0 + jnp.tanh(inner))
    o_ref[...] = _rms(y, gn_ref[...]).astype(o_ref.dtype)


def _lru_branch(lru_in, conv_w, conv_b, w_r, w_i, b_r, b_i, lam, gain, later_w, batch, seq,
                tt=512):
    m = lru_in.shape[0]
    c = lru_in.shape[1] // 2
    nt = seq // tt
    n_blocks = c // LRU_BLOCK
    row = lambda b, t: (0, 0)
    slab = pl.BlockSpec((later_w.shape[0] // (batch * nt), later_w.shape[1]),
                        lambda b, t: (b * nt + t, 0))
    return pl.pallas_call(
        functools.partial(_lru_kernel, tt=tt),
        grid=(batch, nt),
        in_specs=[
            pl.BlockSpec((tt, c), lambda b, t: (b * nt + t, 0)),
            pl.BlockSpec((tt, c), lambda b, t: (b * nt + t, 1)),
            pl.BlockSpec((CONV_WIDTH, c), row),
            pl.BlockSpec((1, c), row),
            pl.BlockSpec((n_blocks, LRU_BLOCK, LRU_BLOCK), lambda b, t: (0, 0, 0)),
            pl.BlockSpec((n_blocks, LRU_BLOCK, LRU_BLOCK), lambda b, t: (0, 0, 0)),
            pl.BlockSpec((1, c), row),
            pl.BlockSpec((1, c), row),
            pl.BlockSpec((1, c), row),
            pl.BlockSpec((1, c), row),
            slab,
        ],
        out_specs=[pl.BlockSpec((tt, c), lambda b, t: (b * nt + t, 0)), slab],
        out_shape=[jax.ShapeDtypeStruct((m, c), BF16),
                   jax.ShapeDtypeStruct(later_w.shape, BF16)],
        scratch_shapes=[
            pltpu.VMEM((tt + SUBLANES, c), F32),
            pltpu.VMEM((tt, c), F32),
            pltpu.VMEM((tt, c), F32),
            pltpu.VMEM((tt, c), F32),
            pltpu.VMEM((SUBLANES, c), F32),
        ],
        compiler_params=_params(2),
        name="rglru",
    )(lru_in, lru_in, conv_w, conv_b, w_r, w_i, b_r, b_i, lam, gain, later_w)


def _outproj_kernel(ya_ref, yl_ref, ga_ref, w_ref, x_ref, gpost_ref, x1_ref):
    ya = _rms(ya_ref[...], ga_ref[...]).astype(BF16)
    y = jnp.concatenate([ya, yl_ref[...]], axis=1)
    mix = jnp.dot(y, w_ref[...], preferred_element_type=F32)
    x1_ref[...] = x_ref[...] + _rms(mix, gpost_ref[...])


def _outproj(y_attn, y_lru, attn_gain, w_bf, x2, post_gain, tm=512):
    m, d = x2.shape
    ca = y_attn.shape[1]
    cl = y_lru.shape[1]
    row = lambda i: (0, 0)
    return pl.pallas_call(
        _outproj_kernel,
        grid=(m // tm,),
        in_specs=[
            pl.BlockSpec((tm, ca), lambda i: (i, 0)),
            pl.BlockSpec((tm, cl), lambda i: (i, 0)),
            pl.BlockSpec((1, ca), row),
            pl.BlockSpec((ca + cl, d), row, pipeline_mode=pl.Buffered(1)),
            pl.BlockSpec((tm, d), lambda i: (i, 0)),
            pl.BlockSpec((1, d), row),
        ],
        out_specs=pl.BlockSpec((tm, d), lambda i: (i, 0)),
        out_shape=jax.ShapeDtypeStruct((m, d), F32),
        compiler_params=_params(1),
        name="outproj",
    )(y_attn, y_lru, attn_gain, w_bf, x2, post_gain)


def _ffn_kernel(x1_ref, gpre_ref, wg_ref, wu_ref, wd_ref, gpost_ref, o_ref, hn_ref):
    f = pl.program_id(1)

    row_chunks = [slice(r0, r0 + NORM_ROWS) for r0 in range(0, o_ref.shape[0], NORM_ROWS)]

    @pl.when(f == 0)
    def _():
        for rows in row_chunks:
            hn_ref[rows, :] = _rms(x1_ref[rows, :], gpre_ref[...]).astype(BF16)
        o_ref[...] = jnp.zeros(o_ref.shape, F32)

    hn = hn_ref[...]
    for c0 in range(0, wg_ref.shape[1], FFN_SLAB):
        cols = slice(c0, c0 + FFN_SLAB)
        gate = jnp.dot(hn, wg_ref[:, cols], preferred_element_type=F32)
        up = jnp.dot(hn, wu_ref[:, cols], preferred_element_type=F32)
        act = (jax.nn.silu(gate) * up).astype(BF16)
        o_ref[...] += jnp.dot(act, wd_ref[cols, :], preferred_element_type=F32)

    @pl.when(f == pl.num_programs(1) - 1)
    def _():
        for rows in row_chunks:
            o_ref[rows, :] = x1_ref[rows, :] + _rms(o_ref[rows, :], gpost_ref[...])


def _ffn(x1, pre_gain, wg, wu, wd, post_gain, tm=1024, tf=512):
    m, d = x1.shape
    dff = wg.shape[1]
    return pl.pallas_call(
        _ffn_kernel,
        grid=(m // tm, dff // tf),
        in_specs=[
            pl.BlockSpec((tm, d), lambda i, f: (i, 0)),
            pl.BlockSpec((1, d), lambda i, f: (0, 0)),
            pl.BlockSpec((d, tf), lambda i, f: (0, f)),
            pl.BlockSpec((d, tf), lambda i, f: (0, f)),
            pl.BlockSpec((tf, d), lambda i, f: (f, 0)),
            pl.BlockSpec((1, d), lambda i, f: (0, 0)),
        ],
        out_specs=pl.BlockSpec((tm, d), lambda i, f: (i, 0)),
        out_shape=jax.ShapeDtypeStruct((m, d), F32),
        scratch_shapes=[pltpu.VMEM((tm, d), BF16)],
        compiler_params=_params(2),
        name="swiglu_ffn",
    )(x1, pre_gain, wg, wu, wd, post_gain)


def _layer(x2, batch, seq, pre_mix, post_mix, pre_ffn, post_ffn, w_in, conv_w, conv_b, w_r, b_r,
           w_i, b_i, lam, attn_gain, lru_gain, w_out, w_g, w_u, w_d):
    row = lambda v: v.reshape(1, -1)
    lru_width = conv_w.shape[1]
    qkv_width = w_in.shape[1] - 2 * lru_width
    n_heads = qkv_width // (3 * HEAD_DIM)

    qkv, lru_in = _inproj(x2, row(pre_mix), w_in, qkv_width)
    y_attn, (w_out_bf, w_g_bf, w_u_bf) = _attention(qkv, batch, seq, n_heads, (w_out, w_g, w_u))
    y_lru, w_d_bf = _lru_branch(lru_in, conv_w, row(conv_b), (0.5 * w_r).astype(BF16),
                                (0.5 * w_i).astype(BF16), row(0.5 * b_r), row(0.5 * b_i),
                                row(lam), row(lru_gain), w_d, batch, seq)
    x1 = _outproj(y_attn, y_lru, row(attn_gain), w_out_bf, x2, row(post_mix))
    return _ffn(x1, row(pre_ffn), w_g_bf, w_u_bf, w_d_bf, row(post_ffn))


def kernel(x, pre_mix_norm, post_mix_norm, pre_ffn_norm, post_ffn_norm, w_in, conv_w, conv_b,
           w_rgate, b_rgate, w_igate, b_igate, lru_lambda, attn_out_norm, lru_out_norm, w_out,
           w_ffn_gate, w_ffn_up, w_ffn_down):
    batch, seq, d = x.shape
    x2 = x.reshape(batch * seq, d)
    for l in range(w_in.shape[0]):
        x2 = _layer(x2, batch, seq, pre_mix_norm[l], post_mix_norm[l], pre_ffn_norm[l],
                    post_ffn_norm[l], w_in[l], conv_w[l], conv_b[l], w_rgate[l], b_rgate[l],
                    w_igate[l], b_igate[l], lru_lambda[l], attn_out_norm[l], lru_out_norm[l],
                    w_out[l], w_ffn_gate[l], w_ffn_up[l], w_ffn_down[l])
    return x2.reshape(batch, seq, d)
```

```python
import functools
import math

import jax
import jax.numpy as jnp
from jax import lax
from jax.experimental import pallas as pl
from jax.experimental.pallas import tpu as pltpu

F32 = jnp.float32
BF16 = jnp.bfloat16

HEAD_DIM = 128
LRU_BLOCK = 128
CONV_WIDTH = 4
LRU_C = 8.0
RMS_EPS = 1e-6
LOG2_E = math.log2(math.e)
Q_SCALE = LOG2_E / math.sqrt(HEAD_DIM)
F32_TINY = float(jnp.finfo(jnp.float32).tiny)
GELU_C1 = math.sqrt(2.0 / math.pi)
GELU_C3 = GELU_C1 * 0.044715
F32_ZERO_LOG2 = -150.0
SUBLANES = 8
NORM_ROWS = 128
FFN_SLAB = 256
VMEM_LIMIT = 60 * 1024 * 1024


def _params(n_axes):
    return pltpu.CompilerParams(dimension_semantics=("arbitrary",) * n_axes,
                                vmem_limit_bytes=VMEM_LIMIT)


def _rms(x, gain):
    var = jnp.mean(x * x, axis=-1, keepdims=True)
    return (x * lax.rsqrt(var + RMS_EPS)) * gain


def _log_sigmoid(z):
    return jnp.minimum(z, 0.0) - jnp.log1p(jnp.exp(-jnp.abs(z)))


def _inproj_kernel(x_hbm, g_ref, w_ref, qkv_ref, lru_ref, hn_ref, xbuf, sem, *, n_q_tiles):
    i, j = pl.program_id(0), pl.program_id(1)
    tm = xbuf.shape[0]

    def fetch(tile):
        return pltpu.make_async_copy(x_hbm.at[pl.ds(tile * tm, tm), :], xbuf, sem)

    @pl.when(jnp.logical_and(i == 0, j == 0))
    def _():
        fetch(0).start()

    @pl.when(j == 0)
    def _():
        fetch(i).wait()
        for r0 in range(0, tm, NORM_ROWS):
            rows = slice(r0, r0 + NORM_ROWS)
            hn_ref[rows, :] = _rms(xbuf[rows, :], g_ref[...]).astype(BF16)

    @pl.when(jnp.logical_and(j == 1, i + 1 < pl.num_programs(0)))
    def _():
        fetch(i + 1).start()

    acc = jnp.dot(hn_ref[...], w_ref[...].astype(BF16), preferred_element_type=F32)
    qkv_ref[...] = (acc * jnp.where(j < n_q_tiles, Q_SCALE, 1.0)).astype(BF16)
    lru_ref[...] = acc


def _inproj(x2, gain, w, qkv_width, tm=2048, tn=512):
    m, d = x2.shape
    n = w.shape[1]
    lru_width = n - qkv_width
    n_qkv_tiles = qkv_width // tn
    n_q_tiles = qkv_width // (3 * tn)
    grid = (m // tm, n // tn)
    return pl.pallas_call(
        functools.partial(_inproj_kernel, n_q_tiles=n_q_tiles),
        grid=grid,
        in_specs=[
            pl.BlockSpec(memory_space=pl.ANY),
            pl.BlockSpec((1, d), lambda i, j: (0, 0)),
            pl.BlockSpec((d, tn), lambda i, j: (0, j)),
        ],
        out_specs=[
            pl.BlockSpec((tm, tn), lambda i, j: (i, jnp.minimum(j, n_qkv_tiles))),
            pl.BlockSpec((tm, tn), lambda i, j: (i, jnp.maximum(j - n_qkv_tiles, 0))),
        ],
        out_shape=[
            jax.ShapeDtypeStruct((m, qkv_width + tn), BF16),
            jax.ShapeDtypeStruct((m, lru_width), F32),
        ],
        scratch_shapes=[pltpu.VMEM((tm, d), BF16), pltpu.VMEM((tm, d), F32),
                        pltpu.SemaphoreType.DMA(())],
        compiler_params=_params(2),
        name="inproj",
    )(x2, gain, w)


def _attn_kernel(q_ref, k_ref, v_ref, *refs, tq, heads, n_cast):
    cast_in, o_ref = refs[:n_cast], refs[n_cast]
    cast_out, tail_ref = refs[n_cast + 1:2 * n_cast + 1], refs[2 * n_cast + 1]

    def cast_weights():
        for src, dst in zip(cast_in, cast_out):
            dst[...] = src[...].astype(BF16)

    tk = tq
    qi = pl.program_id(2)
    row = lax.broadcasted_iota(jnp.int32, (tk, tk), 0)
    col = lax.broadcasted_iota(jnp.int32, (tk, tk), 1)
    neg_later = jnp.where(row > col, -1.0, 0.0).astype(BF16)

    def tile(kj, first):
        start = pl.multiple_of(kj * tk, tk)
        log_betas = []
        softplus = []
        for h in range(heads):
            sl = slice(h * HEAD_DIM, (h + 1) * HEAD_DIM)
            k_t = k_ref[pl.ds(start, tk), sl]
            z2 = lax.dot_general(q_ref[:, sl], k_t, (((1,), (1,)), ((), ())),
                                 preferred_element_type=F32)
            pos = jnp.maximum(z2, 0.0)
            neg = z2 - pos
            log_term = jnp.log(1.0 + jnp.exp2(neg - pos)) * LOG2_E
            sp = pos + log_term
            if first:
                sp = jnp.where(col < row, sp, 0.0)
            softplus.append(sp)
            log_betas.append(neg - log_term)
        sp_all = jnp.concatenate(softplus, axis=0).astype(BF16)
        suffix_all = jnp.dot(sp_all, neg_later, preferred_element_type=F32)
        for h in range(heads):
            sl = slice(h * HEAD_DIM, (h + 1) * HEAD_DIM)
            v_t = v_ref[pl.ds(start, tk), sl]
            suffix = suffix_all[h * tq:(h + 1) * tq]
            expo = log_betas[h] + suffix
            if not first:
                expo = expo + jnp.concatenate([tail_ref[h]] * (tk // HEAD_DIM), axis=1)
            w = jnp.exp2(expo)
            if first:
                w = jnp.where(col < row, w, 0.0)
            pv = jnp.dot(w.astype(BF16), v_t, preferred_element_type=F32)
            total = jnp.broadcast_to(suffix[:, 0:1] - softplus[h][:, 0:1], (tq, HEAD_DIM))
            if first:
                o_ref[:, sl] = pv
                tail_ref[h] = total
            else:
                o_ref[:, sl] += pv
                tail_ref[h] += total

    def stick_left():
        return jnp.max(tail_ref[...]) > F32_ZERO_LOG2

    @pl.when(qi == 0)
    def _():
        cast_weights()
        tile(qi, True)

    @pl.when(qi > 0)
    def _():
        cast_weights()
        tile(qi, True)
        tile(qi - 1, False)

    def body(carry):
        it, _ = carry
        tile(qi - 1 - it, False)
        return it + 1, stick_left()

    lax.while_loop(lambda c: jnp.logical_and(c[0] < qi, c[1]), body,
                   (jnp.int32(1), stick_left()))


def _attention(qkv, batch, seq, n_heads, later_weights, tq=256, heads=8):
    m = qkv.shape[0]
    nq = seq // tq
    ng = n_heads // heads
    width = heads * HEAD_DIM
    n_steps = batch * ng * nq
    step = lambda b, g, i: ((b * ng + g) * nq + i, 0)
    slab_specs = [pl.BlockSpec((w.shape[0] // n_steps, w.shape[1]), step) for w in later_weights]
    outs = pl.pallas_call(
        functools.partial(_attn_kernel, tq=tq, heads=heads, n_cast=len(later_weights)),
        grid=(batch, ng, nq),
        in_specs=[
            pl.BlockSpec((tq, width), lambda b, g, i: (b * nq + i, g)),
            pl.BlockSpec((seq, width), lambda b, g, i: (b, ng + g)),
            pl.BlockSpec((seq, width), lambda b, g, i: (b, 2 * ng + g)),
            *slab_specs,
        ],
        out_specs=[pl.BlockSpec((tq, width), lambda b, g, i: (b * nq + i, g)), *slab_specs],
        out_shape=[jax.ShapeDtypeStruct((m, n_heads * HEAD_DIM), F32),
                   *[jax.ShapeDtypeStruct(w.shape, BF16) for w in later_weights]],
        scratch_shapes=[pltpu.VMEM((heads, tq, HEAD_DIM), F32)],
        compiler_params=_params(3),
        name="stickbreak_attn",
    )(qkv, qkv, qkv, *later_weights)
    return outs[0], outs[1:]


def _lru_kernel(x_ref, gate_ref, cw_ref, cb_ref, wr_ref, wi_ref, br_ref, bi_ref, lam_ref,
                gn_ref, later_w_ref, o_ref, later_w_bf_ref, xbuf, a_buf, b_buf, h_buf, hcarry,
                *, tt):
    t = pl.program_id(1)
    c = x_ref.shape[1]
    n_groups = tt // SUBLANES
    grouped = (n_groups, SUBLANES, LRU_BLOCK)

    @pl.when(t == 0)
    def _():
        xbuf[0:SUBLANES, :] = jnp.zeros((SUBLANES, c), F32)
        hcarry[...] = jnp.zeros((SUBLANES, c), F32)

    xbuf[SUBLANES:SUBLANES + tt, :] = x_ref[...]
    later_w_bf_ref[...] = later_w_ref[...].astype(BF16)

    row_in_vreg = lax.broadcasted_iota(jnp.int32, grouped, 1)
    log_sig_lam = _log_sigmoid(lam_ref[...])
    log2_a_coef = (0.5 * LRU_C * LOG2_E) * log_sig_lam
    neg_log_a_coef = (-0.5 * LRU_C) * log_sig_lam
    for n in range(c // LRU_BLOCK):
        sl = slice(n * LRU_BLOCK, (n + 1) * LRU_BLOCK)
        lane_row = lambda ref, k=0: ref[k:k + 1, sl].reshape(1, 1, LRU_BLOCK)
        window = xbuf[:, sl].reshape(n_groups + 1, SUBLANES, LRU_BLOCK)
        xc = lane_row(cb_ref) + lane_row(cw_ref, CONV_WIDTH - 1) * window[1:]
        for back in range(1, CONV_WIDTH):
            rolled = pltpu.roll(window, back, 1)
            shifted = jnp.where(row_in_vreg >= back, rolled[1:], rolled[:-1])
            xc = xc + lane_row(cw_ref, CONV_WIDTH - 1 - back) * shifted
        xc = xc.reshape(tt, LRU_BLOCK)
        xcb = xc.astype(BF16)
        tanh_r = jnp.tanh(jnp.dot(xcb, wr_ref[n], preferred_element_type=F32) + br_ref[:, sl])
        tanh_i = jnp.tanh(jnp.dot(xcb, wi_ref[n], preferred_element_type=F32) + bi_ref[:, sl])
        two_r = tanh_r + 1.0
        a = jnp.exp2(two_r * log2_a_coef[:, sl])
        one_m_a2 = jnp.tanh(two_r * neg_log_a_coef[:, sl]) * (1.0 + a * a)
        mult = one_m_a2 * lax.rsqrt(jnp.maximum(one_m_a2, F32_TINY))
        b = mult * ((tanh_i + 1.0) * (0.5 * xc))
        a = a.reshape(grouped)
        b = b.reshape(grouped)
        for shift in (1, 2, 4):
            keep = row_in_vreg >= shift
            a_prev = jnp.where(keep, pltpu.roll(a, shift, 1), 1.0)
            b_prev = jnp.where(keep, pltpu.roll(b, shift, 1), 0.0)
            b = a * b_prev + b
            a = a * a_prev
        a_buf[:, sl] = a.reshape(tt, LRU_BLOCK)
        b_buf[:, sl] = b.reshape(tt, LRU_BLOCK)

    xbuf[0:SUBLANES, :] = xbuf[tt:tt + SUBLANES, :]

    def group(g, hb):
        rows = pl.ds(pl.multiple_of(g * SUBLANES, SUBLANES), SUBLANES)
        h = a_buf[rows, :] * hb + b_buf[rows, :]
        h_buf[rows, :] = h
        return jnp.broadcast_to(h[SUBLANES - 1:SUBLANES, :], (SUBLANES, c))

    hcarry[...] = lax.fori_loop(0, n_groups, group, hcarry[...])

    g = gate_ref[...]
    inner = g * (GELU_C1 + GELU_C3 * (g * g))
    y = (h_buf[...] * (0.5 * g)) * (1.0 + jnp.tanh(inner))
    o_ref[...] = _rms(y, gn_ref[...]).astype(o_ref.dtype)


def _lru_branch(lru_in, conv_w, conv_b, w_r, w_i, b_r, b_i, lam, gain, later_w, batch, seq,
                tt=512):
    m = lru_in.shape[0]
    c = lru_in.shape[1] // 2
    nt = seq // tt
    n_blocks = c // LRU_BLOCK
    row = lambda b, t: (0, 0)
    slab = pl.BlockSpec((later_w.shape[0] // (batch * nt), later_w.shape[1]),
                        lambda b, t: (b * nt + t, 0))
    return pl.pallas_call(
        functools.partial(_lru_kernel, tt=tt),
        grid=(batch, nt),
        in_specs=[
            pl.BlockSpec((tt, c), lambda b, t: (b * nt + t, 0)),
            pl.BlockSpec((tt, c), lambda b, t: (b * nt + t, 1)),
            pl.BlockSpec((CONV_WIDTH, c), row),
            pl.BlockSpec((1, c), row),
            pl.BlockSpec((n_blocks, LRU_BLOCK, LRU_BLOCK), lambda b, t: (0, 0, 0)),
            pl.BlockSpec((n_blocks, LRU_BLOCK, LRU_BLOCK), lambda b, t: (0, 0, 0)),
            pl.BlockSpec((1, c), row),
            pl.BlockSpec((1, c), row),
            pl.BlockSpec((1, c), row),
            pl.BlockSpec((1, c), row),
            slab,
        ],
        out_specs=[pl.BlockSpec((tt, c), lambda b, t: (b * nt + t, 0)), slab],
        out_shape=[jax.ShapeDtypeStruct((m, c), BF16),
                   jax.ShapeDtypeStruct(later_w.shape, BF16)],
        scratch_shapes=[
            pltpu.VMEM((tt + SUBLANES, c), F32),
            pltpu.VMEM((tt, c), F32),
            pltpu.VMEM((tt, c), F32),
            pltpu.VMEM((tt, c), F32),
            pltpu.VMEM((SUBLANES, c), F32),
        ],
        compiler_params=_params(2),
        name="rglru",
    )(lru_in, lru_in, conv_w, conv_b, w_r, w_i, b_r, b_i, lam, gain, later_w)


def _outproj_kernel(ya_ref, yl_ref, ga_ref, w_ref, x_ref, gpost_ref, x1_ref):
    ya = _rms(ya_ref[...], ga_ref[...]).astype(BF16)
    y = jnp.concatenate([ya, yl_ref[...]], axis=1)
    mix = jnp.dot(y, w_ref[...], preferred_element_type=F32)
    x1_ref[...] = x_ref[...] + _rms(mix, gpost_ref[...])


def _outproj(y_attn, y_lru, attn_gain, w_bf, x2, post_gain, tm=512):
    m, d = x2.shape
    ca = y_attn.shape[1]
    cl = y_lru.shape[1]
    row = lambda i: (0, 0)
    return pl.pallas_call(
        _outproj_kernel,
        grid=(m // tm,),
        in_specs=[
            pl.BlockSpec((tm, ca), lambda i: (i, 0)),
            pl.BlockSpec((tm, cl), lambda i: (i, 0)),
            pl.BlockSpec((1, ca), row),
            pl.BlockSpec((ca + cl, d), row, pipeline_mode=pl.Buffered(1)),
            pl.BlockSpec((tm, d), lambda i: (i, 0)),
            pl.BlockSpec((1, d), row),
        ],
        out_specs=pl.BlockSpec((tm, d), lambda i: (i, 0)),
        out_shape=jax.ShapeDtypeStruct((m, d), F32),
        compiler_params=_params(1),
        name="outproj",
    )(y_attn, y_lru, attn_gain, w_bf, x2, post_gain)


def _ffn_kernel(x1_ref, gpre_ref, wg_ref, wu_ref, wd_ref, gpost_ref, o_ref, hn_ref):
    f = pl.program_id(1)
    last = pl.num_programs(1) - 1
    half = o_ref.shape[0] // 2
    halves = [slice(0, half), slice(half, 2 * half)]

    def chunks(rows):
        return [slice(r0, r0 + NORM_ROWS) for r0 in range(rows.start, rows.stop, NORM_ROWS)]

    def pre_norm(rows):
        for r in chunks(rows):
            hn_ref[r, :] = _rms(x1_ref[r, :], gpre_ref[...]).astype(BF16)

    def post_norm_residual(rows):
        for r in chunks(rows):
            o_ref[r, :] = x1_ref[r, :] + _rms(o_ref[r, :], gpost_ref[...])

    def ffn(rows, first):
        hn = hn_ref[rows, :]
        for c0 in range(0, wg_ref.shape[1], FFN_SLAB):
            cols = slice(c0, c0 + FFN_SLAB)
            gate = jnp.dot(hn, wg_ref[:, cols], preferred_element_type=F32)
            up = jnp.dot(hn, wu_ref[:, cols], preferred_element_type=F32)
            act = (jax.nn.silu(gate) * up).astype(BF16)
            part = jnp.dot(act, wd_ref[cols, :], preferred_element_type=F32)
            if first and c0 == 0:
                o_ref[rows, :] = part
            else:
                o_ref[rows, :] += part

    @pl.when(f == 0)
    def _():
        for rows in halves:
            pre_norm(rows)
            ffn(rows, True)

    @pl.when(jnp.logical_and(f > 0, f < last))
    def _():
        for rows in halves:
            ffn(rows, False)

    @pl.when(f == last)
    def _():
        for rows in halves:
            ffn(rows, False)
            post_norm_residual(rows)


def _ffn(x1, pre_gain, wg, wu, wd, post_gain, tm=1024, tf=512):
    m, d = x1.shape
    dff = wg.shape[1]
    return pl.pallas_call(
        _ffn_kernel,
        grid=(m // tm, dff // tf),
        in_specs=[
            pl.BlockSpec((tm, d), lambda i, f: (i, 0)),
            pl.BlockSpec((1, d), lambda i, f: (0, 0)),
            pl.BlockSpec((d, tf), lambda i, f: (0, f)),
            pl.BlockSpec((d, tf), lambda i, f: (0, f)),
            pl.BlockSpec((tf, d), lambda i, f: (f, 0)),
            pl.BlockSpec((1, d), lambda i, f: (0, 0)),
        ],
        out_specs=pl.BlockSpec((tm, d), lambda i, f: (i, 0)),
        out_shape=jax.ShapeDtypeStruct((m, d), F32),
        scratch_shapes=[pltpu.VMEM((tm, d), BF16)],
        compiler_params=_params(2),
        name="swiglu_ffn",
    )(x1, pre_gain, wg, wu, wd, post_gain)


def _layer(x2, batch, seq, pre_mix, post_mix, pre_ffn, post_ffn, w_in, conv_w, conv_b, w_r, b_r,
           w_i, b_i, lam, attn_gain, lru_gain, w_out, w_g, w_u, w_d):
    row = lambda v: v.reshape(1, -1)
    lru_width = conv_w.shape[1]
    qkv_width = w_in.shape[1] - 2 * lru_width
    n_heads = qkv_width // (3 * HEAD_DIM)

    qkv, lru_in = _inproj(x2, row(pre_mix), w_in, qkv_width)
    y_attn, (w_out_bf, w_g_bf, w_u_bf) = _attention(qkv, batch, seq, n_heads, (w_out, w_g, w_u))
    y_lru, w_d_bf = _lru_branch(lru_in, conv_w, row(conv_b), (0.5 * w_r).astype(BF16),
                                (0.5 * w_i).astype(BF16), row(0.5 * b_r), row(0.5 * b_i),
                                row(lam), row(lru_gain), w_d, batch, seq)
    x1 = _outproj(y_attn, y_lru, row(attn_gain), w_out_bf, x2, row(post_mix))
    return _ffn(x1, row(pre_ffn), w_g_bf, w_u_bf, w_d_bf, row(post_ffn))


def kernel(x, pre_mix_norm, post_mix_norm, pre_ffn_norm, post_ffn_norm, w_in, conv_w, conv_b,
           w_rgate, b_rgate, w_igate, b_igate, lru_lambda, attn_out_norm, lru_out_norm, w_out,
           w_ffn_gate, w_ffn_up, w_ffn_down):
    batch, seq, d = x.shape
    x2 = x.reshape(batch * seq, d)
    for l in range(w_in.shape[0]):
        x2 = _layer(x2, batch, seq, pre_mix_norm[l], post_mix_norm[l], pre_ffn_norm[l],
                    post_ffn_norm[l], w_in[l], conv_w[l], conv_b[l], w_rgate[l], b_rgate[l],
                    w_igate[l], b_igate[l], lru_lambda[l], attn_out_norm[l], lru_out_norm[l],
                    w_out[l], w_ffn_gate[l], w_ffn_up[l], w_ffn_down[l])
    return x2.reshape(batch, seq, d)
```

```python
import functools
import math

import jax
import jax.numpy as jnp
from jax import lax
from jax.experimental import pallas as pl
from jax.experimental.pallas import tpu as pltpu

F32 = jnp.float32
BF16 = jnp.bfloat16

HEAD_DIM = 128
LRU_BLOCK = 128
CONV_WIDTH = 4
LRU_C = 8.0
RMS_EPS = 1e-6
LOG2_E = math.log2(math.e)
Q_SCALE = LOG2_E / math.sqrt(HEAD_DIM)
F32_TINY = float(jnp.finfo(jnp.float32).tiny)
GELU_C1 = math.sqrt(2.0 / math.pi)
GELU_C3 = GELU_C1 * 0.044715
F32_ZERO_LOG2 = -150.0
SUBLANES = 8
NORM_ROWS = 128
FFN_SLAB = 256
VMEM_LIMIT = 60 * 1024 * 1024


def _params(n_axes):
    return pltpu.CompilerParams(dimension_semantics=("arbitrary",) * n_axes,
                                vmem_limit_bytes=VMEM_LIMIT)


def _rms(x, gain):
    var = jnp.mean(x * x, axis=-1, keepdims=True)
    return (x * lax.rsqrt(var + RMS_EPS)) * gain


def _log_sigmoid(z):
    return jnp.minimum(z, 0.0) - jnp.log1p(jnp.exp(-jnp.abs(z)))


def _inproj_kernel(x_hbm, g_ref, w_ref, qkv_ref, lru_ref, hn_ref, xbuf, sem, *, n_q_tiles):
    i, j = pl.program_id(0), pl.program_id(1)
    tm = xbuf.shape[0]

    def fetch(tile):
        return pltpu.make_async_copy(x_hbm.at[pl.ds(tile * tm, tm), :], xbuf, sem)

    @pl.when(jnp.logical_and(i == 0, j == 0))
    def _():
        fetch(0).start()

    def project(rows):
        acc = jnp.dot(hn_ref[rows, :], w_ref[...].astype(BF16), preferred_element_type=F32)
        qkv_ref[rows, :] = (acc * jnp.where(j < n_q_tiles, Q_SCALE, 1.0)).astype(BF16)
        lru_ref[rows, :] = acc

    @pl.when(j == 0)
    def _():
        fetch(i).wait()
        for h0 in range(0, tm, tm // 2):
            for r0 in range(h0, h0 + tm // 2, NORM_ROWS):
                rows = slice(r0, r0 + NORM_ROWS)
                hn_ref[rows, :] = _rms(xbuf[rows, :], g_ref[...]).astype(BF16)
            project(slice(h0, h0 + tm // 2))

    @pl.when(jnp.logical_and(j == 1, i + 1 < pl.num_programs(0)))
    def _():
        fetch(i + 1).start()

    @pl.when(j > 0)
    def _():
        project(slice(0, tm))


def _inproj(x2, gain, w, qkv_width, tm=2048, tn=512):
    m, d = x2.shape
    n = w.shape[1]
    lru_width = n - qkv_width
    n_qkv_tiles = qkv_width // tn
    n_q_tiles = qkv_width // (3 * tn)
    grid = (m // tm, n // tn)
    return pl.pallas_call(
        functools.partial(_inproj_kernel, n_q_tiles=n_q_tiles),
        grid=grid,
        in_specs=[
            pl.BlockSpec(memory_space=pl.ANY),
            pl.BlockSpec((1, d), lambda i, j: (0, 0)),
            pl.BlockSpec((d, tn), lambda i, j: (0, j)),
        ],
        out_specs=[
            pl.BlockSpec((tm, tn), lambda i, j: (i, jnp.minimum(j, n_qkv_tiles))),
            pl.BlockSpec((tm, tn), lambda i, j: (i, jnp.maximum(j - n_qkv_tiles, 0))),
        ],
        out_shape=[
            jax.ShapeDtypeStruct((m, qkv_width + tn), BF16),
            jax.ShapeDtypeStruct((m, lru_width), F32),
        ],
        scratch_shapes=[pltpu.VMEM((tm, d), BF16), pltpu.VMEM((tm, d), F32),
                        pltpu.SemaphoreType.DMA(())],
        compiler_params=_params(2),
        name="inproj",
    )(x2, gain, w)


def _attn_kernel(q_ref, k_ref, v_ref, *refs, tq, heads, n_cast):
    cast_in, o_ref = refs[:n_cast], refs[n_cast]
    cast_out, tail_ref = refs[n_cast + 1:2 * n_cast + 1], refs[2 * n_cast + 1]

    def cast_weights():
        for src, dst in zip(cast_in, cast_out):
            dst[...] = src[...].astype(BF16)

    tk = tq
    qi = pl.program_id(2)
    row = lax.broadcasted_iota(jnp.int32, (tk, tk), 0)
    col = lax.broadcasted_iota(jnp.int32, (tk, tk), 1)
    neg_later = jnp.where(row > col, -1.0, 0.0).astype(BF16)

    def tile(kj, first):
        start = pl.multiple_of(kj * tk, tk)
        log_betas = []
        softplus = []
        for h in range(heads):
            sl = slice(h * HEAD_DIM, (h + 1) * HEAD_DIM)
            k_t = k_ref[pl.ds(start, tk), sl]
            z2 = lax.dot_general(q_ref[:, sl], k_t, (((1,), (1,)), ((), ())),
                                 preferred_element_type=F32)
            pos = jnp.maximum(z2, 0.0)
            neg = z2 - pos
            log_term = jnp.log(1.0 + jnp.exp2(neg - pos)) * LOG2_E
            sp = pos + log_term
            if first:
                sp = jnp.where(col < row, sp, 0.0)
            softplus.append(sp)
            log_betas.append(neg - log_term)
        sp_all = jnp.concatenate(softplus, axis=0).astype(BF16)
        suffix_all = jnp.dot(sp_all, neg_later, preferred_element_type=F32)
        for h in range(heads):
            sl = slice(h * HEAD_DIM, (h + 1) * HEAD_DIM)
            v_t = v_ref[pl.ds(start, tk), sl]
            suffix = suffix_all[h * tq:(h + 1) * tq]
            expo = log_betas[h] + suffix
            if not first:
                expo = expo + jnp.concatenate([tail_ref[h]] * (tk // HEAD_DIM), axis=1)
            w = jnp.exp2(expo)
            if first:
                w = jnp.where(col < row, w, 0.0)
            pv = jnp.dot(w.astype(BF16), v_t, preferred_element_type=F32)
            total = jnp.broadcast_to(suffix[:, 0:1] - softplus[h][:, 0:1], (tq, HEAD_DIM))
            if first:
                o_ref[:, sl] = pv
                tail_ref[h] = total
            else:
                o_ref[:, sl] += pv
                tail_ref[h] += total

    def stick_left():
        return jnp.max(tail_ref[...]) > F32_ZERO_LOG2

    @pl.when(qi == 0)
    def _():
        cast_weights()
        tile(qi, True)

    @pl.when(qi > 0)
    def _():
        cast_weights()
        tile(qi, True)
        tile(qi - 1, False)

    def body(carry):
        it, _ = carry
        tile(qi - 1 - it, False)
        return it + 1, stick_left()

    lax.while_loop(lambda c: jnp.logical_and(c[0] < qi, c[1]), body,
                   (jnp.int32(1), stick_left()))


def _attention(qkv, batch, seq, n_heads, later_weights, tq=256, heads=8):
    m = qkv.shape[0]
    nq = seq // tq
    ng = n_heads // heads
    width = heads * HEAD_DIM
    n_steps = batch * ng * nq
    step = lambda b, g, i: ((b * ng + g) * nq + i, 0)
    slab_specs = [pl.BlockSpec((w.shape[0] // n_steps, w.shape[1]), step) for w in later_weights]
    outs = pl.pallas_call(
        functools.partial(_attn_kernel, tq=tq, heads=heads, n_cast=len(later_weights)),
        grid=(batch, ng, nq),
        in_specs=[
            pl.BlockSpec((tq, width), lambda b, g, i: (b * nq + i, g)),
            pl.BlockSpec((seq, width), lambda b, g, i: (b, ng + g)),
            pl.BlockSpec((seq, width), lambda b, g, i: (b, 2 * ng + g)),
            *slab_specs,
        ],
        out_specs=[pl.BlockSpec((tq, width), lambda b, g, i: (b * nq + i, g)), *slab_specs],
        out_shape=[jax.ShapeDtypeStruct((m, n_heads * HEAD_DIM), F32),
                   *[jax.ShapeDtypeStruct(w.shape, BF16) for w in later_weights]],
        scratch_shapes=[pltpu.VMEM((heads, tq, HEAD_DIM), F32)],
        compiler_params=_params(3),
        name="stickbreak_attn",
    )(qkv, qkv, qkv, *later_weights)
    return outs[0], outs[1:]


def _lru_kernel(x_ref, gate_ref, cw_ref, cb_ref, wr_ref, wi_ref, br_ref, bi_ref, lam_ref,
                gn_ref, later_w_ref, o_ref, later_w_bf_ref, xbuf, a_buf, b_buf, h_buf, hcarry,
                *, tt):
    t = pl.program_id(1)
    c = x_ref.shape[1]
    n_groups = tt // SUBLANES
    grouped = (n_groups, SUBLANES, LRU_BLOCK)

    @pl.when(t == 0)
    def _():
        xbuf[0:SUBLANES, :] = jnp.zeros((SUBLANES, c), F32)
        hcarry[...] = jnp.zeros((SUBLANES, c), F32)

    xbuf[SUBLANES:SUBLANES + tt, :] = x_ref[...]
    later_w_bf_ref[...] = later_w_ref[...].astype(BF16)

    row_in_vreg = lax.broadcasted_iota(jnp.int32, grouped, 1)
    log_sig_lam = _log_sigmoid(lam_ref[...])
    log2_a_coef = (0.5 * LRU_C * LOG2_E) * log_sig_lam
    neg_log_a_coef = (-0.5 * LRU_C) * log_sig_lam
    for n in range(c // LRU_BLOCK):
        sl = slice(n * LRU_BLOCK, (n + 1) * LRU_BLOCK)
        lane_row = lambda ref, k=0: ref[k:k + 1, sl].reshape(1, 1, LRU_BLOCK)
        window = xbuf[:, sl].reshape(n_groups + 1, SUBLANES, LRU_BLOCK)
        xc = lane_row(cb_ref) + lane_row(cw_ref, CONV_WIDTH - 1) * window[1:]
        for back in range(1, CONV_WIDTH):
            rolled = pltpu.roll(window, back, 1)
            shifted = jnp.where(row_in_vreg >= back, rolled[1:], rolled[:-1])
            xc = xc + lane_row(cw_ref, CONV_WIDTH - 1 - back) * shifted
        xc = xc.reshape(tt, LRU_BLOCK)
        xcb = xc.astype(BF16)
        tanh_r = jnp.tanh(jnp.dot(xcb, wr_ref[n], preferred_element_type=F32) + br_ref[:, sl])
        tanh_i = jnp.tanh(jnp.dot(xcb, wi_ref[n], preferred_element_type=F32) + bi_ref[:, sl])
        two_r = tanh_r + 1.0
        a = jnp.exp2(two_r * log2_a_coef[:, sl])
        one_m_a2 = jnp.tanh(two_r * neg_log_a_coef[:, sl]) * (1.0 + a * a)
        mult = one_m_a2 * lax.rsqrt(jnp.maximum(one_m_a2, F32_TINY))
        b = mult * ((tanh_i + 1.0) * (0.5 * xc))
        a = a.reshape(grouped)
        b = b.reshape(grouped)
        for shift in (1, 2, 4):
            keep = row_in_vreg >= shift
            a_prev = jnp.where(keep, pltpu.roll(a, shift, 1), 1.0)
            b_prev = jnp.where(keep, pltpu.roll(b, shift, 1), 0.0)
            b = a * b_prev + b
            a = a * a_prev
        a_buf[:, sl] = a.reshape(tt, LRU_BLOCK)
        b_buf[:, sl] = b.reshape(tt, LRU_BLOCK)

    xbuf[0:SUBLANES, :] = xbuf[tt:tt + SUBLANES, :]

    def group(g, hb):
        rows = pl.ds(pl.multiple_of(g * SUBLANES, SUBLANES), SUBLANES)
        h = a_buf[rows, :] * hb + b_buf[rows, :]
        h_buf[rows, :] = h
        return jnp.broadcast_to(h[SUBLANES - 1:SUBLANES, :], (SUBLANES, c))

    hcarry[...] = lax.fori_loop(0, n_groups, group, hcarry[...])

    g = gate_ref[...]
    inner = g * (GELU_C1 + GELU_C3 * (g * g))
    y = (h_buf[...] * (0.5 * g)) * (1.0 + jnp.tanh(inner))
    o_ref[...] = _rms(y, gn_ref[...]).astype(o_ref.dtype)


def _lru_branch(lru_in, conv_w, conv_b, w_r, w_i, b_r, b_i, lam, gain, later_w, batch, seq,
                tt=512):
    m = lru_in.shape[0]
    c = lru_in.shape[1] // 2
    nt = seq // tt
    n_blocks = c // LRU_BLOCK
    row = lambda b, t: (0, 0)
    slab = pl.BlockSpec((later_w.shape[0] // (batch * nt), later_w.shape[1]),
                        lambda b, t: (b * nt + t, 0))
    return pl.pallas_call(
        functools.partial(_lru_kernel, tt=tt),
        grid=(batch, nt),
        in_specs=[
            pl.BlockSpec((tt, c), lambda b, t: (b * nt + t, 0)),
            pl.BlockSpec((tt, c), lambda b, t: (b * nt + t, 1)),
            pl.BlockSpec((CONV_WIDTH, c), row),
            pl.BlockSpec((1, c), row),
            pl.BlockSpec((n_blocks, LRU_BLOCK, LRU_BLOCK), lambda b, t: (0, 0, 0)),
            pl.BlockSpec((n_blocks, LRU_BLOCK, LRU_BLOCK), lambda b, t: (0, 0, 0)),
            pl.BlockSpec((1, c), row),
            pl.BlockSpec((1, c), row),
            pl.BlockSpec((1, c), row),
            pl.BlockSpec((1, c), row),
            slab,
        ],
        out_specs=[pl.BlockSpec((tt, c), lambda b, t: (b * nt + t, 0)), slab],
        out_shape=[jax.ShapeDtypeStruct((m, c), BF16),
                   jax.ShapeDtypeStruct(later_w.shape, BF16)],
        scratch_shapes=[
            pltpu.VMEM((tt + SUBLANES, c), F32),
            pltpu.VMEM((tt, c), F32),
            pltpu.VMEM((tt, c), F32),
            pltpu.VMEM((tt, c), F32),
            pltpu.VMEM((SUBLANES, c), F32),
        ],
        compiler_params=_params(2),
        name="rglru",
    )(lru_in, lru_in, conv_w, conv_b, w_r, w_i, b_r, b_i, lam, gain, later_w)


def _outproj_kernel(ya_ref, yl_ref, ga_ref, w_ref, x_ref, gpost_ref, x1_ref):
    ya = _rms(ya_ref[...], ga_ref[...]).astype(BF16)
    y = jnp.concatenate([ya, yl_ref[...]], axis=1)
    mix = jnp.dot(y, w_ref[...], preferred_element_type=F32)
    x1_ref[...] = x_ref[...] + _rms(mix, gpost_ref[...])


def _outproj(y_attn, y_lru, attn_gain, w_bf, x2, post_gain, tm=512):
    m, d = x2.shape
    ca = y_attn.shape[1]
    cl = y_lru.shape[1]
    row = lambda i: (0, 0)
    return pl.pallas_call(
        _outproj_kernel,
        grid=(m // tm,),
        in_specs=[
            pl.BlockSpec((tm, ca), lambda i: (i, 0)),
            pl.BlockSpec((tm, cl), lambda i: (i, 0)),
            pl.BlockSpec((1, ca), row),
            pl.BlockSpec((ca + cl, d), row, pipeline_mode=pl.Buffered(1)),
            pl.BlockSpec((tm, d), lambda i: (i, 0)),
            pl.BlockSpec((1, d), row),
        ],
        out_specs=pl.BlockSpec((tm, d), lambda i: (i, 0)),
        out_shape=jax.ShapeDtypeStruct((m, d), F32),
        compiler_params=_params(1),
        name="outproj",
    )(y_attn, y_lru, attn_gain, w_bf, x2, post_gain)


def _ffn_kernel(x1_ref, gpre_ref, wg_ref, wu_ref, wd_ref, gpost_ref, o_ref, hn_ref):
    f = pl.program_id(1)
    last = pl.num_programs(1) - 1
    half = o_ref.shape[0] // 2
    halves = [slice(0, half), slice(half, 2 * half)]

    def chunks(rows):
        return [slice(r0, r0 + NORM_ROWS) for r0 in range(rows.start, rows.stop, NORM_ROWS)]

    def pre_norm(rows):
        for r in chunks(rows):
            hn_ref[r, :] = _rms(x1_ref[r, :], gpre_ref[...]).astype(BF16)

    def post_norm_residual(rows):
        for r in chunks(rows):
            o_ref[r, :] = x1_ref[r, :] + _rms(o_ref[r, :], gpost_ref[...])

    def ffn(rows, first):
        hn = hn_ref[rows, :]
        for c0 in range(0, wg_ref.shape[1], FFN_SLAB):
            cols = slice(c0, c0 + FFN_SLAB)
            gate = jnp.dot(hn, wg_ref[:, cols], preferred_element_type=F32)
            up = jnp.dot(hn, wu_ref[:, cols], preferred_element_type=F32)
            act = (jax.nn.silu(gate) * up).astype(BF16)
            part = jnp.dot(act, wd_ref[cols, :], preferred_element_type=F32)
            if first and c0 == 0:
                o_ref[rows, :] = part
            else:
                o_ref[rows, :] += part

    @pl.when(f == 0)
    def _():
        for rows in halves:
            pre_norm(rows)
            ffn(rows, True)

    @pl.when(jnp.logical_and(f > 0, f < last))
    def _():
        for rows in halves:
            ffn(rows, False)

    @pl.when(f == last)
    def _():
        for rows in halves:
            ffn(rows, False)
            post_norm_residual(rows)


def _ffn(x1, pre_gain, wg, wu, wd, post_gain, tm=1024, tf=512):
    m, d = x1.shape
    dff = wg.shape[1]
    return pl.pallas_call(
        _ffn_kernel,
        grid=(m // tm, dff // tf),
        in_specs=[
            pl.BlockSpec((tm, d), lambda i, f: (i, 0)),
            pl.BlockSpec((1, d), lambda i, f: (0, 0)),
            pl.BlockSpec((d, tf), lambda i, f: (0, f)),
            pl.BlockSpec((d, tf), lambda i, f: (0, f)),
            pl.BlockSpec((tf, d), lambda i, f: (f, 0)),
            pl.BlockSpec((1, d), lambda i, f: (0, 0)),
        ],
        out_specs=pl.BlockSpec((tm, d), lambda i, f: (i, 0)),
        out_shape=jax.ShapeDtypeStruct((m, d), F32),
        scratch_shapes=[pltpu.VMEM((tm, d), BF16)],
        compiler_params=_params(2),
        name="swiglu_ffn",
    )(x1, pre_gain, wg, wu, wd, post_gain)


def _layer(x2, batch, seq, pre_mix, post_mix, pre_ffn, post_ffn, w_in, conv_w, conv_b, w_r, b_r,
           w_i, b_i, lam, attn_gain, lru_gain, w_out, w_g, w_u, w_d):
    row = lambda v: v.reshape(1, -1)
    lru_width = conv_w.shape[1]
    qkv_width = w_in.shape[1] - 2 * lru_width
    n_heads = qkv_width // (3 * HEAD_DIM)

    qkv, lru_in = _inproj(x2, row(pre_mix), w_in, qkv_width)
    y_attn, (w_out_bf, w_g_bf, w_u_bf) = _attention(qkv, batch, seq, n_heads, (w_out, w_g, w_u))
    y_lru, w_d_bf = _lru_branch(lru_in, conv_w, row(conv_b), (0.5 * w_r).astype(BF16),
                                (0.5 * w_i).astype(BF16), row(0.5 * b_r), row(0.5 * b_i),
                                row(lam), row(lru_gain), w_d, batch, seq)
    x1 = _outproj(y_attn, y_lru, row(attn_gain), w_out_bf, x2, row(post_mix))
    return _ffn(x1, row(pre_ffn), w_g_bf, w_u_bf, w_d_bf, row(post_ffn))


def kernel(x, pre_mix_norm, post_mix_norm, pre_ffn_norm, post_ffn_norm, w_in, conv_w, conv_b,
           w_rgate, b_rgate, w_igate, b_igate, lru_lambda, attn_out_norm, lru_out_norm, w_out,
           w_ffn_gate, w_ffn_up, w_ffn_down):
    batch, seq, d = x.shape
    x2 = x.reshape(batch * seq, d)
    for l in range(w_in.shape[0]):
        x2 = _layer(x2, batch, seq, pre_mix_norm[l], post_mix_norm[l], pre_ffn_norm[l],
                    post_ffn_norm[l], w_in[l], conv_w[l], conv_b[l], w_rgate[l], b_rgate[l],
                    w_igate[l], b_igate[l], lru_lambda[l], attn_out_norm[l], lru_out_norm[l],
                    w_out[l], w_ffn_gate[l], w_ffn_up[l], w_ffn_down[l])
    return x2.reshape(batch, seq, d)
```

```python
import functools
import math

import jax
import jax.numpy as jnp
from jax import lax
from jax.experimental import pallas as pl
from jax.experimental.pallas import tpu as pltpu

F32 = jnp.float32
BF16 = jnp.bfloat16

HEAD_DIM = 128
LRU_BLOCK = 128
LRU_SEG_PAD = 4
CONV_WIDTH = 4
LRU_C = 8.0
RMS_EPS = 1e-6
LOG2_E = math.log2(math.e)
Q_SCALE = LOG2_E / math.sqrt(HEAD_DIM)
F32_TINY = float(jnp.finfo(jnp.float32).tiny)
GELU_C1 = math.sqrt(2.0 / math.pi)
GELU_C3 = GELU_C1 * 0.044715
F32_ZERO_LOG2 = -150.0
SUBLANES = 8
NORM_ROWS = 128
FFN_SLAB = 256
VMEM_LIMIT = 60 * 1024 * 1024


def _params(n_axes):
    return pltpu.CompilerParams(dimension_semantics=("arbitrary",) * n_axes,
                                vmem_limit_bytes=VMEM_LIMIT)


def _rms(x, gain):
    var = jnp.mean(x * x, axis=-1, keepdims=True)
    return (x * lax.rsqrt(var + RMS_EPS)) * gain


def _log_sigmoid(z):
    return jnp.minimum(z, 0.0) - jnp.log1p(jnp.exp(-jnp.abs(z)))


def _inproj_kernel(x_hbm, g_ref, w_ref, qkv_ref, lru_ref, hn_ref, xbuf, sem, *, n_q_tiles):
    i, j = pl.program_id(0), pl.program_id(1)
    tm = xbuf.shape[0]

    def fetch(tile):
        return pltpu.make_async_copy(x_hbm.at[pl.ds(tile * tm, tm), :], xbuf, sem)

    @pl.when(jnp.logical_and(i == 0, j == 0))
    def _():
        fetch(0).start()

    def project(rows):
        acc = jnp.dot(hn_ref[rows, :], w_ref[...].astype(BF16), preferred_element_type=F32)
        qkv_ref[rows, :] = (acc * jnp.where(j < n_q_tiles, Q_SCALE, 1.0)).astype(BF16)
        lru_ref[rows, :] = acc

    @pl.when(j == 0)
    def _():
        fetch(i).wait()
        for h0 in range(0, tm, tm // 2):
            for r0 in range(h0, h0 + tm // 2, NORM_ROWS):
                rows = slice(r0, r0 + NORM_ROWS)
                hn_ref[rows, :] = _rms(xbuf[rows, :], g_ref[...]).astype(BF16)
            project(slice(h0, h0 + tm // 2))

    @pl.when(jnp.logical_and(j == 1, i + 1 < pl.num_programs(0)))
    def _():
        fetch(i + 1).start()

    @pl.when(j > 0)
    def _():
        project(slice(0, tm))


def _inproj(x2, gain, w, qkv_width, tm=2048, tn=512):
    m, d = x2.shape
    n = w.shape[1]
    lru_width = n - qkv_width
    n_qkv_tiles = qkv_width // tn
    n_q_tiles = qkv_width // (3 * tn)
    grid = (m // tm, n // tn)
    return pl.pallas_call(
        functools.partial(_inproj_kernel, n_q_tiles=n_q_tiles),
        grid=grid,
        in_specs=[
            pl.BlockSpec(memory_space=pl.ANY),
            pl.BlockSpec((1, d), lambda i, j: (0, 0)),
            pl.BlockSpec((d, tn), lambda i, j: (0, j)),
        ],
        out_specs=[
            pl.BlockSpec((tm, tn), lambda i, j: (i, jnp.minimum(j, n_qkv_tiles))),
            pl.BlockSpec((tm, tn), lambda i, j: (i, jnp.maximum(j - n_qkv_tiles, 0))),
        ],
        out_shape=[
            jax.ShapeDtypeStruct((m, qkv_width + tn), BF16),
            jax.ShapeDtypeStruct((m, lru_width), F32),
        ],
        scratch_shapes=[pltpu.VMEM((tm, d), BF16), pltpu.VMEM((tm, d), F32),
                        pltpu.SemaphoreType.DMA(())],
        compiler_params=_params(2),
        name="inproj",
    )(x2, gain, w)


def _attn_kernel(q_ref, k_ref, v_ref, *refs, tq, heads, n_cast):
    cast_in, o_ref = refs[:n_cast], refs[n_cast]
    cast_out, tail_ref = refs[n_cast + 1:2 * n_cast + 1], refs[2 * n_cast + 1]

    def cast_weights():
        for src, dst in zip(cast_in, cast_out):
            dst[...] = src[...].astype(BF16)

    tk = tq
    qi = pl.program_id(2)
    row = lax.broadcasted_iota(jnp.int32, (tk, tk), 0)
    col = lax.broadcasted_iota(jnp.int32, (tk, tk), 1)
    neg_later = jnp.where(row > col, -1.0, 0.0).astype(BF16)

    def tile(kj, first):
        start = pl.multiple_of(kj * tk, tk)
        log_betas = []
        softplus = []
        for h in range(heads):
            sl = slice(h * HEAD_DIM, (h + 1) * HEAD_DIM)
            k_t = k_ref[pl.ds(start, tk), sl]
            z2 = lax.dot_general(q_ref[:, sl], k_t, (((1,), (1,)), ((), ())),
                                 preferred_element_type=F32)
            pos = jnp.maximum(z2, 0.0)
            neg = z2 - pos
            log_term = jnp.log(1.0 + jnp.exp2(neg - pos)) * LOG2_E
            sp = pos + log_term
            if first:
                sp = jnp.where(col < row, sp, 0.0)
            softplus.append(sp)
            log_betas.append(neg - log_term)
        sp_all = jnp.concatenate(softplus, axis=0).astype(BF16)
        suffix_all = jnp.dot(sp_all, neg_later, preferred_element_type=F32)
        for h in range(heads):
            sl = slice(h * HEAD_DIM, (h + 1) * HEAD_DIM)
            v_t = v_ref[pl.ds(start, tk), sl]
            suffix = suffix_all[h * tq:(h + 1) * tq]
            expo = log_betas[h] + suffix
            if not first:
                expo = expo + jnp.concatenate([tail_ref[h]] * (tk // HEAD_DIM), axis=1)
            w = jnp.exp2(expo)
            if first:
                w = jnp.where(col < row, w, 0.0)
            pv = jnp.dot(w.astype(BF16), v_t, preferred_element_type=F32)
            total = jnp.broadcast_to(suffix[:, 0:1] - softplus[h][:, 0:1], (tq, HEAD_DIM))
            if first:
                o_ref[:, sl] = pv
                tail_ref[h] = total
            else:
                o_ref[:, sl] += pv
                tail_ref[h] += total

    def stick_left():
        return jnp.max(tail_ref[...]) > F32_ZERO_LOG2

    @pl.when(qi == 0)
    def _():
        cast_weights()
        tile(qi, True)

    @pl.when(qi > 0)
    def _():
        cast_weights()
        tile(qi, True)
        tile(qi - 1, False)

    def body(carry):
        it, _ = carry
        tile(qi - 1 - it, False)
        return it + 1, stick_left()

    lax.while_loop(lambda c: jnp.logical_and(c[0] < qi, c[1]), body,
                   (jnp.int32(1), stick_left()))


def _attention(qkv, batch, seq, n_heads, later_weights, tq=256, heads=8):
    m = qkv.shape[0]
    nq = seq // tq
    ng = n_heads // heads
    width = heads * HEAD_DIM
    n_steps = batch * ng * nq
    step = lambda b, g, i: ((b * ng + g) * nq + i, 0)
    slab_specs = [pl.BlockSpec((w.shape[0] // n_steps, w.shape[1]), step) for w in later_weights]
    outs = pl.pallas_call(
        functools.partial(_attn_kernel, tq=tq, heads=heads, n_cast=len(later_weights)),
        grid=(batch, ng, nq),
        in_specs=[
            pl.BlockSpec((tq, width), lambda b, g, i: (b * nq + i, g)),
            pl.BlockSpec((seq, width), lambda b, g, i: (b, ng + g)),
            pl.BlockSpec((seq, width), lambda b, g, i: (b, 2 * ng + g)),
            *slab_specs,
        ],
        out_specs=[pl.BlockSpec((tq, width), lambda b, g, i: (b * nq + i, g)), *slab_specs],
        out_shape=[jax.ShapeDtypeStruct((m, n_heads * HEAD_DIM), F32),
                   *[jax.ShapeDtypeStruct(w.shape, BF16) for w in later_weights]],
        scratch_shapes=[pltpu.VMEM((heads, tq, HEAD_DIM), F32)],
        compiler_params=_params(3),
        name="stickbreak_attn",
    )(qkv, qkv, qkv, *later_weights)
    return outs[0], outs[1:]


def _lru_kernel(x_ref, gate_ref, cw_ref, cb_ref, wr_ref, wi_ref, br_ref, bi_ref, lam_ref,
                gn_ref, later_w_ref, o_ref, later_w_bf_ref, xbuf, stage, h_buf, hcarry, *, tt):
    t = pl.program_id(1)
    c = x_ref.shape[1]
    n_groups = tt // SUBLANES
    grouped = (n_groups, SUBLANES, LRU_BLOCK)
    seg = tt // SUBLANES
    pitch = stage.shape[1] // SUBLANES

    @pl.when(t == 0)
    def _():
        xbuf[0:SUBLANES, :] = jnp.zeros((SUBLANES, c), F32)
        hcarry[...] = jnp.zeros((SUBLANES, c), F32)

    xbuf[SUBLANES:SUBLANES + tt, :] = x_ref[...]
    later_w_bf_ref[...] = later_w_ref[...].astype(BF16)

    row_in_vreg = lax.broadcasted_iota(jnp.int32, grouped, 1)
    log_sig_lam = _log_sigmoid(lam_ref[...])
    log2_a_coef = (0.5 * LRU_C * LOG2_E) * log_sig_lam
    neg_log_a_coef = (-0.5 * LRU_C) * log_sig_lam
    for n in range(c // LRU_BLOCK):
        sl = slice(n * LRU_BLOCK, (n + 1) * LRU_BLOCK)
        lane_row = lambda ref, k=0: ref[k:k + 1, sl].reshape(1, 1, LRU_BLOCK)
        window = xbuf[:, sl].reshape(n_groups + 1, SUBLANES, LRU_BLOCK)
        xc = lane_row(cb_ref) + lane_row(cw_ref, CONV_WIDTH - 1) * window[1:]
        for back in range(1, CONV_WIDTH):
            rolled = pltpu.roll(window, back, 1)
            shifted = jnp.where(row_in_vreg >= back, rolled[1:], rolled[:-1])
            xc = xc + lane_row(cw_ref, CONV_WIDTH - 1 - back) * shifted
        xc = xc.reshape(tt, LRU_BLOCK)
        for k in range(SUBLANES):
            stage[n, k * pitch:k * pitch + seg, :] = xc[k * seg:(k + 1) * seg, :]
        xc = jnp.concatenate([stage[n, pl.ds(j, SUBLANES, stride=pitch), :] for j in range(seg)],
                             axis=0)
        xcb = xc.astype(BF16)
        tanh_r = jnp.tanh(jnp.dot(xcb, wr_ref[n], preferred_element_type=F32) + br_ref[:, sl])
        tanh_i = jnp.tanh(jnp.dot(xcb, wi_ref[n], preferred_element_type=F32) + bi_ref[:, sl])
        two_r = tanh_r + 1.0
        a = jnp.exp2(two_r * log2_a_coef[:, sl])
        one_m_a2 = jnp.tanh(two_r * neg_log_a_coef[:, sl]) * (1.0 + a * a)
        mult = one_m_a2 * lax.rsqrt(jnp.maximum(one_m_a2, F32_TINY))
        b = mult * ((tanh_i + 1.0) * (0.5 * xc))

        a = a.reshape(seg, SUBLANES, LRU_BLOCK)
        b = b.reshape(seg, SUBLANES, LRU_BLOCK)
        hs, ps = [b[0]], [a[0]]
        for j in range(1, seg):
            hs.append(a[j] * hs[-1] + b[j])
            ps.append(a[j] * ps[-1])
        entering = [hcarry[0:1, sl]]
        for k in range(SUBLANES):
            entering.append(hs[-1][k:k + 1, :] + ps[-1][k:k + 1, :] * entering[-1])
        hcarry[:, sl] = jnp.broadcast_to(entering[-1], (SUBLANES, LRU_BLOCK))
        enter = jnp.concatenate(entering[:-1], axis=0)
        for j in range(seg):
            stage[n, pl.ds(j, SUBLANES, stride=pitch), :] = hs[j] + ps[j] * enter
        for k in range(SUBLANES):
            h_buf[k * seg:(k + 1) * seg, sl] = stage[n, k * pitch:k * pitch + seg, :]

    xbuf[0:SUBLANES, :] = xbuf[tt:tt + SUBLANES, :]

    g = gate_ref[...]
    inner = g * (GELU_C1 + GELU_C3 * (g * g))
    y = (h_buf[...] * (0.5 * g)) * (1.0 + jnp.tanh(inner))
    o_ref[...] = _rms(y, gn_ref[...]).astype(o_ref.dtype)


def _lru_branch(lru_in, conv_w, conv_b, w_r, w_i, b_r, b_i, lam, gain, later_w, batch, seq,
                tt=512):
    m = lru_in.shape[0]
    c = lru_in.shape[1] // 2
    nt = seq // tt
    n_blocks = c // LRU_BLOCK
    row = lambda b, t: (0, 0)
    slab = pl.BlockSpec((later_w.shape[0] // (batch * nt), later_w.shape[1]),
                        lambda b, t: (b * nt + t, 0))
    return pl.pallas_call(
        functools.partial(_lru_kernel, tt=tt),
        grid=(batch, nt),
        in_specs=[
            pl.BlockSpec((tt, c), lambda b, t: (b * nt + t, 0)),
            pl.BlockSpec((tt, c), lambda b, t: (b * nt + t, 1)),
            pl.BlockSpec((CONV_WIDTH, c), row),
            pl.BlockSpec((1, c), row),
            pl.BlockSpec((n_blocks, LRU_BLOCK, LRU_BLOCK), lambda b, t: (0, 0, 0)),
            pl.BlockSpec((n_blocks, LRU_BLOCK, LRU_BLOCK), lambda b, t: (0, 0, 0)),
            pl.BlockSpec((1, c), row),
            pl.BlockSpec((1, c), row),
            pl.BlockSpec((1, c), row),
            pl.BlockSpec((1, c), row),
            slab,
        ],
        out_specs=[pl.BlockSpec((tt, c), lambda b, t: (b * nt + t, 0)), slab],
        out_shape=[jax.ShapeDtypeStruct((m, c), BF16),
                   jax.ShapeDtypeStruct(later_w.shape, BF16)],
        scratch_shapes=[
            pltpu.VMEM((tt + SUBLANES, c), F32),
            pltpu.VMEM((n_blocks, SUBLANES * (tt // SUBLANES + LRU_SEG_PAD), LRU_BLOCK), F32),
            pltpu.VMEM((tt, c), F32),
            pltpu.VMEM((SUBLANES, c), F32),
        ],
        compiler_params=_params(2),
        name="rglru",
    )(lru_in, lru_in, conv_w, conv_b, w_r, w_i, b_r, b_i, lam, gain, later_w)


def _outproj_kernel(ya_ref, yl_ref, ga_ref, w_ref, x_ref, gpost_ref, x1_ref):
    ya = _rms(ya_ref[...], ga_ref[...]).astype(BF16)
    y = jnp.concatenate([ya, yl_ref[...]], axis=1)
    mix = jnp.dot(y, w_ref[...], preferred_element_type=F32)
    x1_ref[...] = x_ref[...] + _rms(mix, gpost_ref[...])


def _outproj(y_attn, y_lru, attn_gain, w_bf, x2, post_gain, tm=512):
    m, d = x2.shape
    ca = y_attn.shape[1]
    cl = y_lru.shape[1]
    row = lambda i: (0, 0)
    return pl.pallas_call(
        _outproj_kernel,
        grid=(m // tm,),
        in_specs=[
            pl.BlockSpec((tm, ca), lambda i: (i, 0)),
            pl.BlockSpec((tm, cl), lambda i: (i, 0)),
            pl.BlockSpec((1, ca), row),
            pl.BlockSpec((ca + cl, d), row, pipeline_mode=pl.Buffered(1)),
            pl.BlockSpec((tm, d), lambda i: (i, 0)),
            pl.BlockSpec((1, d), row),
        ],
        out_specs=pl.BlockSpec((tm, d), lambda i: (i, 0)),
        out_shape=jax.ShapeDtypeStruct((m, d), F32),
        compiler_params=_params(1),
        name="outproj",
    )(y_attn, y_lru, attn_gain, w_bf, x2, post_gain)


def _ffn_kernel(x1_ref, gpre_ref, wg_ref, wu_ref, wd_ref, gpost_ref, o_ref, hn_ref):
    f = pl.program_id(1)
    last = pl.num_programs(1) - 1
    half = o_ref.shape[0] // 2
    halves = [slice(0, half), slice(half, 2 * half)]

    def chunks(rows):
        return [slice(r0, r0 + NORM_ROWS) for r0 in range(rows.start, rows.stop, NORM_ROWS)]

    def pre_norm(rows):
        for r in chunks(rows):
            hn_ref[r, :] = _rms(x1_ref[r, :], gpre_ref[...]).astype(BF16)

    def post_norm_residual(rows):
        for r in chunks(rows):
            o_ref[r, :] = x1_ref[r, :] + _rms(o_ref[r, :], gpost_ref[...])

    def ffn(rows, first):
        hn = hn_ref[rows, :]
        for c0 in range(0, wg_ref.shape[1], FFN_SLAB):
            cols = slice(c0, c0 + FFN_SLAB)
            gate = jnp.dot(hn, wg_ref[:, cols], preferred_element_type=F32)
            up = jnp.dot(hn, wu_ref[:, cols], preferred_element_type=F32)
            act = (jax.nn.silu(gate) * up).astype(BF16)
            part = jnp.dot(act, wd_ref[cols, :], preferred_element_type=F32)
            if first and c0 == 0:
                o_ref[rows, :] = part
            else:
                o_ref[rows, :] += part

    @pl.when(f == 0)
    def _():
        for rows in halves:
            pre_norm(rows)
            ffn(rows, True)

    @pl.when(jnp.logical_and(f > 0, f < last))
    def _():
        for rows in halves:
            ffn(rows, False)

    @pl.when(f == last)
    def _():
        for rows in halves:
            ffn(rows, False)
            post_norm_residual(rows)


def _ffn(x1, pre_gain, wg, wu, wd, post_gain, tm=1024, tf=512):
    m, d = x1.shape
    dff = wg.shape[1]
    return pl.pallas_call(
        _ffn_kernel,
        grid=(m // tm, dff // tf),
        in_specs=[
            pl.BlockSpec((tm, d), lambda i, f: (i, 0)),
            pl.BlockSpec((1, d), lambda i, f: (0, 0)),
            pl.BlockSpec((d, tf), lambda i, f: (0, f)),
            pl.BlockSpec((d, tf), lambda i, f: (0, f)),
            pl.BlockSpec((tf, d), lambda i, f: (f, 0)),
            pl.BlockSpec((1, d), lambda i, f: (0, 0)),
        ],
        out_specs=pl.BlockSpec((tm, d), lambda i, f: (i, 0)),
        out_shape=jax.ShapeDtypeStruct((m, d), F32),
        scratch_shapes=[pltpu.VMEM((tm, d), BF16)],
        compiler_params=_params(2),
        name="swiglu_ffn",
    )(x1, pre_gain, wg, wu, wd, post_gain)


def _layer(x2, batch, seq, pre_mix, post_mix, pre_ffn, post_ffn, w_in, conv_w, conv_b, w_r, b_r,
           w_i, b_i, lam, attn_gain, lru_gain, w_out, w_g, w_u, w_d):
    row = lambda v: v.reshape(1, -1)
    lru_width = conv_w.shape[1]
    qkv_width = w_in.shape[1] - 2 * lru_width
    n_heads = qkv_width // (3 * HEAD_DIM)

    qkv, lru_in = _inproj(x2, row(pre_mix), w_in, qkv_width)
    y_attn, (w_out_bf, w_g_bf, w_u_bf) = _attention(qkv, batch, seq, n_heads, (w_out, w_g, w_u))
    y_lru, w_d_bf = _lru_branch(lru_in, conv_w, row(conv_b), (0.5 * w_r).astype(BF16),
                                (0.5 * w_i).astype(BF16), row(0.5 * b_r), row(0.5 * b_i),
                                row(lam), row(lru_gain), w_d, batch, seq)
    x1 = _outproj(y_attn, y_lru, row(attn_gain), w_out_bf, x2, row(post_mix))
    return _ffn(x1, row(pre_ffn), w_g_bf, w_u_bf, w_d_bf, row(post_ffn))


def kernel(x, pre_mix_norm, post_mix_norm, pre_ffn_norm, post_ffn_norm, w_in, conv_w, conv_b,
           w_rgate, b_rgate, w_igate, b_igate, lru_lambda, attn_out_norm, lru_out_norm, w_out,
           w_ffn_gate, w_ffn_up, w_ffn_down):
    batch, seq, d = x.shape
    x2 = x.reshape(batch * seq, d)
    for l in range(w_in.shape[0]):
        x2 = _layer(x2, batch, seq, pre_mix_norm[l], post_mix_norm[l], pre_ffn_norm[l],
                    post_ffn_norm[l], w_in[l], conv_w[l], conv_b[l], w_rgate[l], b_rgate[l],
                    w_igate[l], b_igate[l], lru_lambda[l], attn_out_norm[l], lru_out_norm[l],
                    w_out[l], w_ffn_gate[l], w_ffn_up[l], w_ffn_down[l])
    return x2.reshape(batch, seq, d)
```

```python
import functools
import math

import jax
import jax.numpy as jnp
from jax import lax
from jax.experimental import pallas as pl
from jax.experimental.pallas import tpu as pltpu

F32 = jnp.float32
BF16 = jnp.bfloat16

HEAD_DIM = 128
LRU_BLOCK = 128
LRU_SEG_PAD = 4
CONV_WIDTH = 4
LRU_C = 8.0
RMS_EPS = 1e-6
LOG2_E = math.log2(math.e)
Q_SCALE = LOG2_E / math.sqrt(HEAD_DIM)
F32_TINY = float(jnp.finfo(jnp.float32).tiny)
GELU_C1 = math.sqrt(2.0 / math.pi)
GELU_C3 = GELU_C1 * 0.044715
F32_ZERO_LOG2 = -150.0
SUBLANES = 8
NORM_ROWS = 128
FFN_SLAB = 256
VMEM_LIMIT = 60 * 1024 * 1024


def _params(n_axes):
    return pltpu.CompilerParams(dimension_semantics=("arbitrary",) * n_axes,
                                vmem_limit_bytes=VMEM_LIMIT)


def _rms(x, gain):
    var = jnp.mean(x * x, axis=-1, keepdims=True)
    return (x * lax.rsqrt(var + RMS_EPS)) * gain


def _log_sigmoid(z):
    return jnp.minimum(z, 0.0) - jnp.log1p(jnp.exp(-jnp.abs(z)))


def _inproj_kernel(x_hbm, g_ref, w_ref, qkv_ref, lru_ref, hn_ref, xbuf, sem, *, n_q_tiles):
    i, j = pl.program_id(0), pl.program_id(1)
    tm = xbuf.shape[0]

    def fetch(tile):
        return pltpu.make_async_copy(x_hbm.at[pl.ds(tile * tm, tm), :], xbuf, sem)

    @pl.when(jnp.logical_and(i == 0, j == 0))
    def _():
        fetch(0).start()

    def project(rows):
        acc = jnp.dot(hn_ref[rows, :], w_ref[...].astype(BF16), preferred_element_type=F32)
        qkv_ref[rows, :] = (acc * jnp.where(j < n_q_tiles, Q_SCALE, 1.0)).astype(BF16)
        lru_ref[rows, :] = acc

    @pl.when(j == 0)
    def _():
        fetch(i).wait()
        for h0 in range(0, tm, tm // 2):
            for r0 in range(h0, h0 + tm // 2, NORM_ROWS):
                rows = slice(r0, r0 + NORM_ROWS)
                hn_ref[rows, :] = _rms(xbuf[rows, :], g_ref[...]).astype(BF16)
            project(slice(h0, h0 + tm // 2))

    @pl.when(jnp.logical_and(j == 1, i + 1 < pl.num_programs(0)))
    def _():
        fetch(i + 1).start()

    @pl.when(j > 0)
    def _():
        project(slice(0, tm))


def _inproj(x2, gain, w, qkv_width, tm=2048, tn=512):
    m, d = x2.shape
    n = w.shape[1]
    lru_width = n - qkv_width
    n_qkv_tiles = qkv_width // tn
    n_q_tiles = qkv_width // (3 * tn)
    grid = (m // tm, n // tn)
    return pl.pallas_call(
        functools.partial(_inproj_kernel, n_q_tiles=n_q_tiles),
        grid=grid,
        in_specs=[
            pl.BlockSpec(memory_space=pl.ANY),
            pl.BlockSpec((1, d), lambda i, j: (0, 0)),
            pl.BlockSpec((d, tn), lambda i, j: (0, j)),
        ],
        out_specs=[
            pl.BlockSpec((tm, tn), lambda i, j: (i, jnp.minimum(j, n_qkv_tiles))),
            pl.BlockSpec((tm, tn), lambda i, j: (i, jnp.maximum(j - n_qkv_tiles, 0))),
        ],
        out_shape=[
            jax.ShapeDtypeStruct((m, qkv_width + tn), BF16),
            jax.ShapeDtypeStruct((m, lru_width), F32),
        ],
        scratch_shapes=[pltpu.VMEM((tm, d), BF16), pltpu.VMEM((tm, d), F32),
                        pltpu.SemaphoreType.DMA(())],
        compiler_params=_params(2),
        name="inproj",
    )(x2, gain, w)


def _attn_kernel(q_ref, k_ref, v_ref, *refs, tq, heads, n_cast):
    cast_in, o_ref = refs[:n_cast], refs[n_cast]
    cast_out, tail_ref = refs[n_cast + 1:2 * n_cast + 1], refs[2 * n_cast + 1]

    def cast_weights():
        for src, dst in zip(cast_in, cast_out):
            dst[...] = src[...].astype(BF16)

    tk = tq
    qi = pl.program_id(2)
    row = lax.broadcasted_iota(jnp.int32, (tk, tk), 0)
    col = lax.broadcasted_iota(jnp.int32, (tk, tk), 1)
    neg_later = jnp.where(row > col, -1.0, 0.0).astype(BF16)

    def tile(kj, first):
        start = pl.multiple_of(kj * tk, tk)
        log_betas = []
        softplus = []
        for h in range(heads):
            sl = slice(h * HEAD_DIM, (h + 1) * HEAD_DIM)
            k_t = k_ref[pl.ds(start, tk), sl]
            z2 = lax.dot_general(q_ref[:, sl], k_t, (((1,), (1,)), ((), ())),
                                 preferred_element_type=F32)
            pos = jnp.maximum(z2, 0.0)
            neg = z2 - pos
            log_term = jnp.log(1.0 + jnp.exp2(neg - pos)) * LOG2_E
            sp = pos + log_term
            if first:
                sp = jnp.where(col < row, sp, 0.0)
            softplus.append(sp)
            log_betas.append(neg - log_term)
        sp_all = jnp.concatenate(softplus, axis=0).astype(BF16)
        suffix_all = jnp.dot(sp_all, neg_later, preferred_element_type=F32)
        for h in range(heads):
            sl = slice(h * HEAD_DIM, (h + 1) * HEAD_DIM)
            v_t = v_ref[pl.ds(start, tk), sl]
            suffix = suffix_all[h * tq:(h + 1) * tq]
            expo = log_betas[h] + suffix
            if not first:
                expo = expo + jnp.concatenate([tail_ref[h]] * (tk // HEAD_DIM), axis=1)
            w = jnp.exp2(expo)
            if first:
                w = jnp.where(col < row, w, 0.0)
            pv = jnp.dot(w.astype(BF16), v_t, preferred_element_type=F32)
            total = jnp.broadcast_to(suffix[:, 0:1] - softplus[h][:, 0:1], (tq, HEAD_DIM))
            if first:
                o_ref[:, sl] = pv
                tail_ref[h] = total
            else:
                o_ref[:, sl] += pv
                tail_ref[h] += total

    def stick_left():
        return jnp.max(tail_ref[...]) > F32_ZERO_LOG2

    @pl.when(qi == 0)
    def _():
        cast_weights()
        tile(qi, True)

    @pl.when(qi > 0)
    def _():
        cast_weights()
        tile(qi, True)
        tile(qi - 1, False)

    def body(carry):
        it, _ = carry
        tile(qi - 1 - it, False)
        return it + 1, stick_left()

    lax.while_loop(lambda c: jnp.logical_and(c[0] < qi, c[1]), body,
                   (jnp.int32(1), stick_left()))


def _attention(qkv, batch, seq, n_heads, later_weights, tq=256, heads=8):
    m = qkv.shape[0]
    nq = seq // tq
    ng = n_heads // heads
    width = heads * HEAD_DIM
    n_steps = batch * ng * nq
    step = lambda b, g, i: ((b * ng + g) * nq + i, 0)
    slab_specs = [pl.BlockSpec((w.shape[0] // n_steps, w.shape[1]), step) for w in later_weights]
    outs = pl.pallas_call(
        functools.partial(_attn_kernel, tq=tq, heads=heads, n_cast=len(later_weights)),
        grid=(batch, ng, nq),
        in_specs=[
            pl.BlockSpec((tq, width), lambda b, g, i: (b * nq + i, g)),
            pl.BlockSpec((seq, width), lambda b, g, i: (b, ng + g)),
            pl.BlockSpec((seq, width), lambda b, g, i: (b, 2 * ng + g)),
            *slab_specs,
        ],
        out_specs=[pl.BlockSpec((tq, width), lambda b, g, i: (b * nq + i, g)), *slab_specs],
        out_shape=[jax.ShapeDtypeStruct((m, n_heads * HEAD_DIM), F32),
                   *[jax.ShapeDtypeStruct(w.shape, BF16) for w in later_weights]],
        scratch_shapes=[pltpu.VMEM((heads, tq, HEAD_DIM), F32)],
        compiler_params=_params(3),
        name="stickbreak_attn",
    )(qkv, qkv, qkv, *later_weights)
    return outs[0], outs[1:]


def _lru_kernel(x_ref, gate_ref, cw_ref, cb_ref, wr_ref, wi_ref, br_ref, bi_ref, lam_ref,
                gn_ref, o_ref, xbuf, stage, h_buf, hcarry, *, tt):
    t = pl.program_id(1)
    c = x_ref.shape[1]
    n_groups = tt // SUBLANES
    grouped = (n_groups, SUBLANES, LRU_BLOCK)
    seg = tt // SUBLANES
    pitch = stage.shape[1] // SUBLANES

    @pl.when(t == 0)
    def _():
        xbuf[0:SUBLANES, :] = jnp.zeros((SUBLANES, c), F32)
        hcarry[...] = jnp.zeros((SUBLANES, c), F32)

    xbuf[SUBLANES:SUBLANES + tt, :] = x_ref[...]

    row_in_vreg = lax.broadcasted_iota(jnp.int32, grouped, 1)
    log_sig_lam = _log_sigmoid(lam_ref[...])
    log2_a_coef = (0.5 * LRU_C * LOG2_E) * log_sig_lam
    neg_log_a_coef = (-0.5 * LRU_C) * log_sig_lam
    for n in range(c // LRU_BLOCK):
        sl = slice(n * LRU_BLOCK, (n + 1) * LRU_BLOCK)
        lane_row = lambda ref, k=0: ref[k:k + 1, sl].reshape(1, 1, LRU_BLOCK)
        window = xbuf[:, sl].reshape(n_groups + 1, SUBLANES, LRU_BLOCK)
        xc = lane_row(cb_ref) + lane_row(cw_ref, CONV_WIDTH - 1) * window[1:]
        for back in range(1, CONV_WIDTH):
            rolled = pltpu.roll(window, back, 1)
            shifted = jnp.where(row_in_vreg >= back, rolled[1:], rolled[:-1])
            xc = xc + lane_row(cw_ref, CONV_WIDTH - 1 - back) * shifted
        xc = xc.reshape(tt, LRU_BLOCK)
        for k in range(SUBLANES):
            stage[n, k * pitch:k * pitch + seg, :] = xc[k * seg:(k + 1) * seg, :]
        xc = jnp.concatenate([stage[n, pl.ds(j, SUBLANES, stride=pitch), :] for j in range(seg)],
                             axis=0)
        xcb = xc.astype(BF16)
        tanh_r = jnp.tanh(jnp.dot(xcb, wr_ref[n], preferred_element_type=F32) + br_ref[:, sl])
        tanh_i = jnp.tanh(jnp.dot(xcb, wi_ref[n], preferred_element_type=F32) + bi_ref[:, sl])
        two_r = tanh_r + 1.0
        a = jnp.exp2(two_r * log2_a_coef[:, sl])
        one_m_a2 = jnp.tanh(two_r * neg_log_a_coef[:, sl]) * (1.0 + a * a)
        mult = one_m_a2 * lax.rsqrt(jnp.maximum(one_m_a2, F32_TINY))
        b = mult * ((tanh_i + 1.0) * (0.5 * xc))

        a = a.reshape(seg, SUBLANES, LRU_BLOCK)
        b = b.reshape(seg, SUBLANES, LRU_BLOCK)
        hs, ps = [b[0]], [a[0]]
        for j in range(1, seg):
            hs.append(a[j] * hs[-1] + b[j])
            ps.append(a[j] * ps[-1])
        entering = [hcarry[0:1, sl]]
        for k in range(SUBLANES):
            entering.append(hs[-1][k:k + 1, :] + ps[-1][k:k + 1, :] * entering[-1])
        hcarry[:, sl] = jnp.broadcast_to(entering[-1], (SUBLANES, LRU_BLOCK))
        enter = jnp.concatenate(entering[:-1], axis=0)
        for j in range(seg):
            stage[n, pl.ds(j, SUBLANES, stride=pitch), :] = hs[j] + ps[j] * enter
        for k in range(SUBLANES):
            h_buf[k * seg:(k + 1) * seg, sl] = stage[n, k * pitch:k * pitch + seg, :]

    xbuf[0:SUBLANES, :] = xbuf[tt:tt + SUBLANES, :]

    g = gate_ref[...]
    inner = g * (GELU_C1 + GELU_C3 * (g * g))
    y = (h_buf[...] * (0.5 * g)) * (1.0 + jnp.tanh(inner))
    o_ref[...] = _rms(y, gn_ref[...]).astype(o_ref.dtype)


def _lru_branch(lru_in, conv_w, conv_b, w_r, w_i, b_r, b_i, lam, gain, batch, seq, tt=512):
    m = lru_in.shape[0]
    c = lru_in.shape[1] // 2
    nt = seq // tt
    n_blocks = c // LRU_BLOCK
    row = lambda b, t: (0, 0)
    return pl.pallas_call(
        functools.partial(_lru_kernel, tt=tt),
        grid=(batch, nt),
        in_specs=[
            pl.BlockSpec((tt, c), lambda b, t: (b * nt + t, 0)),
            pl.BlockSpec((tt, c), lambda b, t: (b * nt + t, 1)),
            pl.BlockSpec((CONV_WIDTH, c), row),
            pl.BlockSpec((1, c), row),
            pl.BlockSpec((n_blocks, LRU_BLOCK, LRU_BLOCK), lambda b, t: (0, 0, 0)),
            pl.BlockSpec((n_blocks, LRU_BLOCK, LRU_BLOCK), lambda b, t: (0, 0, 0)),
            pl.BlockSpec((1, c), row),
            pl.BlockSpec((1, c), row),
            pl.BlockSpec((1, c), row),
            pl.BlockSpec((1, c), row),
        ],
        out_specs=pl.BlockSpec((tt, c), lambda b, t: (b * nt + t, 0)),
        out_shape=jax.ShapeDtypeStruct((m, c), BF16),
        scratch_shapes=[
            pltpu.VMEM((tt + SUBLANES, c), F32),
            pltpu.VMEM((n_blocks, SUBLANES * (tt // SUBLANES + LRU_SEG_PAD), LRU_BLOCK), F32),
            pltpu.VMEM((tt, c), F32),
            pltpu.VMEM((SUBLANES, c), F32),
        ],
        compiler_params=_params(2),
        name="rglru",
    )(lru_in, lru_in, conv_w, conv_b, w_r, w_i, b_r, b_i, lam, gain)


def _outproj_kernel(ya_ref, yl_ref, ga_ref, w_ref, x_ref, gpost_ref, x1_ref):
    ya = _rms(ya_ref[...], ga_ref[...]).astype(BF16)
    y = jnp.concatenate([ya, yl_ref[...]], axis=1)
    mix = jnp.dot(y, w_ref[...], preferred_element_type=F32)
    x1_ref[...] = x_ref[...] + _rms(mix, gpost_ref[...])


def _outproj(y_attn, y_lru, attn_gain, w_bf, x2, post_gain, tm=512):
    m, d = x2.shape
    ca = y_attn.shape[1]
    cl = y_lru.shape[1]
    row = lambda i: (0, 0)
    return pl.pallas_call(
        _outproj_kernel,
        grid=(m // tm,),
        in_specs=[
            pl.BlockSpec((tm, ca), lambda i: (i, 0)),
            pl.BlockSpec((tm, cl), lambda i: (i, 0)),
            pl.BlockSpec((1, ca), row),
            pl.BlockSpec((ca + cl, d), row, pipeline_mode=pl.Buffered(1)),
            pl.BlockSpec((tm, d), lambda i: (i, 0)),
            pl.BlockSpec((1, d), row),
        ],
        out_specs=pl.BlockSpec((tm, d), lambda i: (i, 0)),
        out_shape=jax.ShapeDtypeStruct((m, d), F32),
        compiler_params=_params(1),
        name="outproj",
    )(y_attn, y_lru, attn_gain, w_bf, x2, post_gain)


def _ffn_kernel(x1_ref, gpre_ref, wg_ref, wu_ref, wd_ref, gpost_ref, o_ref, hn_ref):
    f = pl.program_id(1)
    last = pl.num_programs(1) - 1
    half = o_ref.shape[0] // 2
    halves = [slice(0, half), slice(half, 2 * half)]

    def chunks(rows):
        return [slice(r0, r0 + NORM_ROWS) for r0 in range(rows.start, rows.stop, NORM_ROWS)]

    def pre_norm(rows):
        for r in chunks(rows):
            hn_ref[r, :] = _rms(x1_ref[r, :], gpre_ref[...]).astype(BF16)

    def post_norm_residual(rows):
        for r in chunks(rows):
            o_ref[r, :] = x1_ref[r, :] + _rms(o_ref[r, :], gpost_ref[...])

    def ffn(rows, first):
        hn = hn_ref[rows, :]
        for c0 in range(0, wg_ref.shape[1], FFN_SLAB):
            cols = slice(c0, c0 + FFN_SLAB)
            gate = jnp.dot(hn, wg_ref[:, cols], preferred_element_type=F32)
            up = jnp.dot(hn, wu_ref[:, cols], preferred_element_type=F32)
            act = (jax.nn.silu(gate) * up).astype(BF16)
            part = jnp.dot(act, wd_ref[cols, :].astype(BF16), preferred_element_type=F32)
            if first and c0 == 0:
                o_ref[rows, :] = part
            else:
                o_ref[rows, :] += part

    @pl.when(f == 0)
    def _():
        for rows in halves:
            pre_norm(rows)
            ffn(rows, True)

    @pl.when(jnp.logical_and(f > 0, f < last))
    def _():
        for rows in halves:
            ffn(rows, False)

    @pl.when(f == last)
    def _():
        for rows in halves:
            ffn(rows, False)
            post_norm_residual(rows)


def _ffn(x1, pre_gain, wg, wu, wd, post_gain, tm=1024, tf=512):
    m, d = x1.shape
    dff = wg.shape[1]
    return pl.pallas_call(
        _ffn_kernel,
        grid=(m // tm, dff // tf),
        in_specs=[
            pl.BlockSpec((tm, d), lambda i, f: (i, 0)),
            pl.BlockSpec((1, d), lambda i, f: (0, 0)),
            pl.BlockSpec((d, tf), lambda i, f: (0, f)),
            pl.BlockSpec((d, tf), lambda i, f: (0, f)),
            pl.BlockSpec((tf, d), lambda i, f: (f, 0)),
            pl.BlockSpec((1, d), lambda i, f: (0, 0)),
        ],
        out_specs=pl.BlockSpec((tm, d), lambda i, f: (i, 0)),
        out_shape=jax.ShapeDtypeStruct((m, d), F32),
        scratch_shapes=[pltpu.VMEM((tm, d), BF16)],
        compiler_params=_params(2),
        name="swiglu_ffn",
    )(x1, pre_gain, wg, wu, wd, post_gain)


def _layer(x2, batch, seq, pre_mix, post_mix, pre_ffn, post_ffn, w_in, conv_w, conv_b, w_r, b_r,
           w_i, b_i, lam, attn_gain, lru_gain, w_out, w_g, w_u, w_d):
    row = lambda v: v.reshape(1, -1)
    lru_width = conv_w.shape[1]
    qkv_width = w_in.shape[1] - 2 * lru_width
    n_heads = qkv_width // (3 * HEAD_DIM)

    qkv, lru_in = _inproj(x2, row(pre_mix), w_in, qkv_width)
    y_attn, (w_out_bf, w_g_bf, w_u_bf) = _attention(qkv, batch, seq, n_heads, (w_out, w_g, w_u))
    y_lru = _lru_branch(lru_in, conv_w, row(conv_b), (0.5 * w_r).astype(BF16),
                        (0.5 * w_i).astype(BF16), row(0.5 * b_r), row(0.5 * b_i), row(lam),
                        row(lru_gain), batch, seq)
    x1 = _outproj(y_attn, y_lru, row(attn_gain), w_out_bf, x2, row(post_mix))
    return _ffn(x1, row(pre_ffn), w_g_bf, w_u_bf, w_d, row(post_ffn))


def kernel(x, pre_mix_norm, post_mix_norm, pre_ffn_norm, post_ffn_norm, w_in, conv_w, conv_b,
           w_rgate, b_rgate, w_igate, b_igate, lru_lambda, attn_out_norm, lru_out_norm, w_out,
           w_ffn_gate, w_ffn_up, w_ffn_down):
    batch, seq, d = x.shape
    x2 = x.reshape(batch * seq, d)
    for l in range(w_in.shape[0]):
        x2 = _layer(x2, batch, seq, pre_mix_norm[l], post_mix_norm[l], pre_ffn_norm[l],
                    post_ffn_norm[l], w_in[l], conv_w[l], conv_b[l], w_rgate[l], b_rgate[l],
                    w_igate[l], b_igate[l], lru_lambda[l], attn_out_norm[l], lru_out_norm[l],
                    w_out[l], w_ffn_gate[l], w_ffn_up[l], w_ffn_down[l])
    return x2.reshape(batch, seq, d)
```

```python
import functools
import math

import jax
import jax.numpy as jnp
from jax import lax
from jax.experimental import pallas as pl
from jax.experimental.pallas import tpu as pltpu

F32 = jnp.float32
BF16 = jnp.bfloat16

HEAD_DIM = 128
LRU_BLOCK = 128
LRU_SEG_PAD = 4
CONV_WIDTH = 4
LRU_C = 8.0
RMS_EPS = 1e-6
LOG2_E = math.log2(math.e)
Q_SCALE = LOG2_E / math.sqrt(HEAD_DIM)
F32_TINY = float(jnp.finfo(jnp.float32).tiny)
GELU_C1 = math.sqrt(2.0 / math.pi)
GELU_C3 = GELU_C1 * 0.044715
F32_ZERO_LOG2 = -150.0
SUBLANES = 8
NORM_ROWS = 128
FFN_SLAB = 256
VMEM_LIMIT = 60 * 1024 * 1024


def _params(n_axes):
    return pltpu.CompilerParams(dimension_semantics=("arbitrary",) * n_axes,
                                vmem_limit_bytes=VMEM_LIMIT)


def _rms(x, gain):
    var = jnp.mean(x * x, axis=-1, keepdims=True)
    return (x * lax.rsqrt(var + RMS_EPS)) * gain


def _log_sigmoid(z):
    return jnp.minimum(z, 0.0) - jnp.log1p(jnp.exp(-jnp.abs(z)))


def _inproj_kernel(x_hbm, g_ref, w_ref, qkv_ref, lru_ref, hn_ref, xbuf, sem, *, n_q_tiles):
    i, j = pl.program_id(0), pl.program_id(1)
    tm = xbuf.shape[0]

    def fetch(tile):
        return pltpu.make_async_copy(x_hbm.at[pl.ds(tile * tm, tm), :], xbuf, sem)

    @pl.when(jnp.logical_and(i == 0, j == 0))
    def _():
        fetch(0).start()

    def project(rows):
        acc = jnp.dot(hn_ref[rows, :], w_ref[...].astype(BF16), preferred_element_type=F32)
        qkv_ref[rows, :] = (acc * jnp.where(j < n_q_tiles, Q_SCALE, 1.0)).astype(BF16)
        lru_ref[rows, :] = acc

    @pl.when(j == 0)
    def _():
        fetch(i).wait()
        for h0 in range(0, tm, tm // 2):
            for r0 in range(h0, h0 + tm // 2, NORM_ROWS):
                rows = slice(r0, r0 + NORM_ROWS)
                hn_ref[rows, :] = _rms(xbuf[rows, :], g_ref[...]).astype(BF16)
            project(slice(h0, h0 + tm // 2))

    @pl.when(jnp.logical_and(j == 1, i + 1 < pl.num_programs(0)))
    def _():
        fetch(i + 1).start()

    @pl.when(j > 0)
    def _():
        project(slice(0, tm))


def _inproj(x2, gain, w, qkv_width, tm=2048, tn=512):
    m, d = x2.shape
    n = w.shape[1]
    lru_width = n - qkv_width
    n_qkv_tiles = qkv_width // tn
    n_q_tiles = qkv_width // (3 * tn)
    grid = (m // tm, n // tn)
    return pl.pallas_call(
        functools.partial(_inproj_kernel, n_q_tiles=n_q_tiles),
        grid=grid,
        in_specs=[
            pl.BlockSpec(memory_space=pl.ANY),
            pl.BlockSpec((1, d), lambda i, j: (0, 0)),
            pl.BlockSpec((d, tn), lambda i, j: (0, j)),
        ],
        out_specs=[
            pl.BlockSpec((tm, tn), lambda i, j: (i, jnp.minimum(j, n_qkv_tiles))),
            pl.BlockSpec((tm, tn), lambda i, j: (i, jnp.maximum(j - n_qkv_tiles, 0))),
        ],
        out_shape=[
            jax.ShapeDtypeStruct((m, qkv_width + tn), BF16),
            jax.ShapeDtypeStruct((m, lru_width), F32),
        ],
        scratch_shapes=[pltpu.VMEM((tm, d), BF16), pltpu.VMEM((tm, d), F32),
                        pltpu.SemaphoreType.DMA(())],
        compiler_params=_params(2),
        name="inproj",
    )(x2, gain, w)


def _attn_kernel(q_ref, k_ref, v_ref, *refs, tq, subs, heads, n_cast):
    cast_in, o_ref = refs[:n_cast], refs[n_cast]
    cast_out, tail_ref = refs[n_cast + 1:2 * n_cast + 1], refs[2 * n_cast + 1]

    def cast_weights():
        for src, dst in zip(cast_in, cast_out):
            dst[...] = src[...].astype(BF16)

    tk = tq
    step = pl.program_id(2)
    row = lax.broadcasted_iota(jnp.int32, (tk, tk), 0)
    col = lax.broadcasted_iota(jnp.int32, (tk, tk), 1)
    neg_later = jnp.where(row > col, -1.0, 0.0).astype(BF16)

    def tile(s, kj, first):
        rows = slice(s * tq, (s + 1) * tq)
        start = pl.multiple_of(kj * tk, tk)
        log_betas = []
        softplus = []
        for h in range(heads):
            sl = slice(h * HEAD_DIM, (h + 1) * HEAD_DIM)
            k_t = k_ref[pl.ds(start, tk), sl]
            z2 = lax.dot_general(q_ref[rows, sl], k_t, (((1,), (1,)), ((), ())),
                                 preferred_element_type=F32)
            pos = jnp.maximum(z2, 0.0)
            neg = z2 - pos
            log_term = jnp.log(1.0 + jnp.exp2(neg - pos)) * LOG2_E
            sp = pos + log_term
            if first:
                sp = jnp.where(col < row, sp, 0.0)
            softplus.append(sp)
            log_betas.append(neg - log_term)
        sp_all = jnp.concatenate(softplus, axis=0).astype(BF16)
        suffix_all = jnp.dot(sp_all, neg_later, preferred_element_type=F32)
        for h in range(heads):
            sl = slice(h * HEAD_DIM, (h + 1) * HEAD_DIM)
            v_t = v_ref[pl.ds(start, tk), sl]
            suffix = suffix_all[h * tq:(h + 1) * tq]
            expo = log_betas[h] + suffix
            if not first:
                expo = expo + jnp.concatenate([tail_ref[s, h]] * (tk // HEAD_DIM), axis=1)
            w = jnp.exp2(expo)
            if first:
                w = jnp.where(col < row, w, 0.0)
            pv = jnp.dot(w.astype(BF16), v_t, preferred_element_type=F32)
            total = jnp.broadcast_to(suffix[:, 0:1] - softplus[h][:, 0:1], (tq, HEAD_DIM))
            if first:
                o_ref[rows, sl] = pv
                tail_ref[s, h] = total
            else:
                o_ref[rows, sl] += pv
                tail_ref[s, h] += total

    def stick_left(s):
        return jnp.max(tail_ref[s]) > F32_ZERO_LOG2

    @pl.when(step == 0)
    def _():
        cast_weights()
        for s in range(subs):
            tile(s, s, True)
            if s > 0:
                tile(s, s - 1, False)

    @pl.when(step > 0)
    def _():
        cast_weights()
        for s in range(subs):
            tile(s, step * subs + s, True)
            tile(s, step * subs + s - 1, False)

    for s in range(subs):
        qb = step * subs + s

        def body(carry, s=s, qb=qb):
            it, _ = carry
            tile(s, qb - 1 - it, False)
            return it + 1, stick_left(s)

        lax.while_loop(lambda c, qb=qb: jnp.logical_and(c[0] < qb, c[1]), body,
                       (jnp.int32(1), stick_left(s)))


def _attention(qkv, batch, seq, n_heads, later_weights, tq=256, subs=2, heads=8):
    m = qkv.shape[0]
    rows = tq * subs
    nq = seq // rows
    ng = n_heads // heads
    width = heads * HEAD_DIM
    n_steps = batch * ng * nq
    step = lambda b, g, i: ((b * ng + g) * nq + i, 0)
    slab_specs = [pl.BlockSpec((w.shape[0] // n_steps, w.shape[1]), step) for w in later_weights]
    outs = pl.pallas_call(
        functools.partial(_attn_kernel, tq=tq, subs=subs, heads=heads,
                          n_cast=len(later_weights)),
        grid=(batch, ng, nq),
        in_specs=[
            pl.BlockSpec((rows, width), lambda b, g, i: (b * nq + i, g)),
            pl.BlockSpec((seq, width), lambda b, g, i: (b, ng + g), pipeline_mode=pl.Buffered(1)),
            pl.BlockSpec((seq, width), lambda b, g, i: (b, 2 * ng + g),
                         pipeline_mode=pl.Buffered(1)),
            *slab_specs,
        ],
        out_specs=[pl.BlockSpec((rows, width), lambda b, g, i: (b * nq + i, g)), *slab_specs],
        out_shape=[jax.ShapeDtypeStruct((m, n_heads * HEAD_DIM), F32),
                   *[jax.ShapeDtypeStruct(w.shape, BF16) for w in later_weights]],
        scratch_shapes=[pltpu.VMEM((subs, heads, tq, HEAD_DIM), F32)],
        compiler_params=_params(3),
        name="stickbreak_attn",
    )(qkv, qkv, qkv, *later_weights)
    return outs[0], outs[1:]


def _lru_kernel(x_ref, gate_ref, cw_ref, cb_ref, wr_ref, wi_ref, br_ref, bi_ref, lam_ref,
                gn_ref, o_ref, xtail, stage, h_buf, hcarry, *, tt):
    t = pl.program_id(1)
    c = x_ref.shape[1]
    seg = tt // SUBLANES
    pitch = stage.shape[1] // SUBLANES

    @pl.when(t == 0)
    def _():
        xtail[...] = jnp.zeros((SUBLANES, c), F32)
        hcarry[...] = jnp.zeros((SUBLANES, c), F32)

    first_sublane = lax.broadcasted_iota(jnp.int32, (SUBLANES, LRU_BLOCK), 0) == 0
    log_sig_lam = _log_sigmoid(lam_ref[...])
    log2_a_coef = (0.5 * LRU_C * LOG2_E) * log_sig_lam
    neg_log_a_coef = (-0.5 * LRU_C) * log_sig_lam
    for n in range(c // LRU_BLOCK):
        sl = slice(n * LRU_BLOCK, (n + 1) * LRU_BLOCK)
        for k in range(SUBLANES):
            stage[n, k * pitch:k * pitch + seg, :] = x_ref[k * seg:(k + 1) * seg, sl]
        xs = [stage[n, pl.ds(j, SUBLANES, stride=pitch), :] for j in range(seg)]
        before = [jnp.where(first_sublane, xtail[SUBLANES - back:SUBLANES - back + 1, sl],
                            pltpu.roll(xs[seg - back], 1, 0))
                  for back in range(CONV_WIDTH - 1, 0, -1)]
        xs = before + xs
        taps = [cw_ref[k:k + 1, sl] for k in range(CONV_WIDTH)]
        bias = cb_ref[:, sl]
        conv = []
        for j in range(seg):
            acc = bias + taps[0] * xs[j]
            for k in range(1, CONV_WIDTH):
                acc = acc + taps[k] * xs[j + k]
            conv.append(acc)
        xc = jnp.concatenate(conv, axis=0)
        xcb = xc.astype(BF16)
        tanh_r = jnp.tanh(jnp.dot(xcb, wr_ref[n], preferred_element_type=F32) + br_ref[:, sl])
        tanh_i = jnp.tanh(jnp.dot(xcb, wi_ref[n], preferred_element_type=F32) + bi_ref[:, sl])
        two_r = tanh_r + 1.0
        a = jnp.exp2(two_r * log2_a_coef[:, sl])
        one_m_a2 = jnp.tanh(two_r * neg_log_a_coef[:, sl]) * (1.0 + a * a)
        mult = one_m_a2 * lax.rsqrt(jnp.maximum(one_m_a2, F32_TINY))
        b = mult * ((tanh_i + 1.0) * (0.5 * xc))

        a = a.reshape(seg, SUBLANES, LRU_BLOCK)
        b = b.reshape(seg, SUBLANES, LRU_BLOCK)
        hs, ps = [b[0]], [a[0]]
        for j in range(1, seg):
            hs.append(a[j] * hs[-1] + b[j])
            ps.append(a[j] * ps[-1])
        entering = [hcarry[0:1, sl]]
        for k in range(SUBLANES):
            entering.append(hs[-1][k:k + 1, :] + ps[-1][k:k + 1, :] * entering[-1])
        hcarry[:, sl] = jnp.broadcast_to(entering[-1], (SUBLANES, LRU_BLOCK))
        enter = jnp.concatenate(entering[:-1], axis=0)
        for j in range(seg):
            stage[n, pl.ds(j, SUBLANES, stride=pitch), :] = hs[j] + ps[j] * enter
        for k in range(SUBLANES):
            h_buf[k * seg:(k + 1) * seg, sl] = stage[n, k * pitch:k * pitch + seg, :]

    xtail[...] = x_ref[tt - SUBLANES:tt, :]

    g = gate_ref[...]
    inner = g * (GELU_C1 + GELU_C3 * (g * g))
    y = (h_buf[...] * (0.5 * g)) * (1.0 + jnp.tanh(inner))
    o_ref[...] = _rms(y, gn_ref[...]).astype(o_ref.dtype)


def _lru_branch(lru_in, conv_w, conv_b, w_r, w_i, b_r, b_i, lam, gain, batch, seq, tt=512):
    m = lru_in.shape[0]
    c = lru_in.shape[1] // 2
    nt = seq // tt
    n_blocks = c // LRU_BLOCK
    row = lambda b, t: (0, 0)
    return pl.pallas_call(
        functools.partial(_lru_kernel, tt=tt),
        grid=(batch, nt),
        in_specs=[
            pl.BlockSpec((tt, c), lambda b, t: (b * nt + t, 0)),
            pl.BlockSpec((tt, c), lambda b, t: (b * nt + t, 1)),
            pl.BlockSpec((CONV_WIDTH, c), row),
            pl.BlockSpec((1, c), row),
            pl.BlockSpec((n_blocks, LRU_BLOCK, LRU_BLOCK), lambda b, t: (0, 0, 0)),
            pl.BlockSpec((n_blocks, LRU_BLOCK, LRU_BLOCK), lambda b, t: (0, 0, 0)),
            pl.BlockSpec((1, c), row),
            pl.BlockSpec((1, c), row),
            pl.BlockSpec((1, c), row),
            pl.BlockSpec((1, c), row),
        ],
        out_specs=pl.BlockSpec((tt, c), lambda b, t: (b * nt + t, 0)),
        out_shape=jax.ShapeDtypeStruct((m, c), BF16),
        scratch_shapes=[
            pltpu.VMEM((SUBLANES, c), F32),
            pltpu.VMEM((n_blocks, SUBLANES * (tt // SUBLANES + LRU_SEG_PAD), LRU_BLOCK), F32),
            pltpu.VMEM((tt, c), F32),
            pltpu.VMEM((SUBLANES, c), F32),
        ],
        compiler_params=_params(2),
        name="rglru",
    )(lru_in, lru_in, conv_w, conv_b, w_r, w_i, b_r, b_i, lam, gain)


def _outproj_kernel(ya_ref, yl_ref, ga_ref, w_ref, x_ref, gpost_ref, x1_ref):
    ya = _rms(ya_ref[...], ga_ref[...]).astype(BF16)
    y = jnp.concatenate([ya, yl_ref[...]], axis=1)
    mix = jnp.dot(y, w_ref[...], preferred_element_type=F32)
    x1_ref[...] = x_ref[...] + _rms(mix, gpost_ref[...])


def _outproj(y_attn, y_lru, attn_gain, w_bf, x2, post_gain, tm=512):
    m, d = x2.shape
    ca = y_attn.shape[1]
    cl = y_lru.shape[1]
    row = lambda i: (0, 0)
    return pl.pallas_call(
        _outproj_kernel,
        grid=(m // tm,),
        in_specs=[
            pl.BlockSpec((tm, ca), lambda i: (i, 0)),
            pl.BlockSpec((tm, cl), lambda i: (i, 0)),
            pl.BlockSpec((1, ca), row),
            pl.BlockSpec((ca + cl, d), row, pipeline_mode=pl.Buffered(1)),
            pl.BlockSpec((tm, d), lambda i: (i, 0)),
            pl.BlockSpec((1, d), row),
        ],
        out_specs=pl.BlockSpec((tm, d), lambda i: (i, 0)),
        out_shape=jax.ShapeDtypeStruct((m, d), F32),
        compiler_params=_params(1),
        name="outproj",
    )(y_attn, y_lru, attn_gain, w_bf, x2, post_gain)


def _ffn_kernel(x1_ref, gpre_ref, wg_ref, wu_ref, wd_ref, gpost_ref, o_ref, hn_ref):
    f = pl.program_id(1)
    last = pl.num_programs(1) - 1
    half = o_ref.shape[0] // 2
    halves = [slice(0, half), slice(half, 2 * half)]

    def chunks(rows):
        return [slice(r0, r0 + NORM_ROWS) for r0 in range(rows.start, rows.stop, NORM_ROWS)]

    def pre_norm(rows):
        for r in chunks(rows):
            hn_ref[r, :] = _rms(x1_ref[r, :], gpre_ref[...]).astype(BF16)

    def post_norm_residual(rows):
        for r in chunks(rows):
            o_ref[r, :] = x1_ref[r, :] + _rms(o_ref[r, :], gpost_ref[...])

    def ffn(rows, first):
        hn = hn_ref[rows, :]
        for c0 in range(0, wg_ref.shape[1], FFN_SLAB):
            cols = slice(c0, c0 + FFN_SLAB)
            gate = jnp.dot(hn, wg_ref[:, cols], preferred_element_type=F32)
            up = jnp.dot(hn, wu_ref[:, cols], preferred_element_type=F32)
            act = (jax.nn.silu(gate) * up).astype(BF16)
            part = jnp.dot(act, wd_ref[cols, :].astype(BF16), preferred_element_type=F32)
            if first and c0 == 0:
                o_ref[rows, :] = part
            else:
                o_ref[rows, :] += part

    @pl.when(f == 0)
    def _():
        for rows in halves:
            pre_norm(rows)
            ffn(rows, True)

    @pl.when(jnp.logical_and(f > 0, f < last))
    def _():
        for rows in halves:
            ffn(rows, False)

    @pl.when(f == last)
    def _():
        for rows in halves:
            ffn(rows, False)
            post_norm_residual(rows)


def _ffn(x1, pre_gain, wg, wu, wd, post_gain, tm=1024, tf=512):
    m, d = x1.shape
    dff = wg.shape[1]
    return pl.pallas_call(
        _ffn_kernel,
        grid=(m // tm, dff // tf),
        in_specs=[
            pl.BlockSpec((tm, d), lambda i, f: (i, 0)),
            pl.BlockSpec((1, d), lambda i, f: (0, 0)),
            pl.BlockSpec((d, tf), lambda i, f: (0, f)),
            pl.BlockSpec((d, tf), lambda i, f: (0, f)),
            pl.BlockSpec((tf, d), lambda i, f: (f, 0)),
            pl.BlockSpec((1, d), lambda i, f: (0, 0)),
        ],
        out_specs=pl.BlockSpec((tm, d), lambda i, f: (i, 0)),
        out_shape=jax.ShapeDtypeStruct((m, d), F32),
        scratch_shapes=[pltpu.VMEM((tm, d), BF16)],
        compiler_params=_params(2),
        name="swiglu_ffn",
    )(x1, pre_gain, wg, wu, wd, post_gain)


def _layer(x2, batch, seq, pre_mix, post_mix, pre_ffn, post_ffn, w_in, conv_w, conv_b, w_r, b_r,
           w_i, b_i, lam, attn_gain, lru_gain, w_out, w_g, w_u, w_d):
    row = lambda v: v.reshape(1, -1)
    lru_width = conv_w.shape[1]
    qkv_width = w_in.shape[1] - 2 * lru_width
    n_heads = qkv_width // (3 * HEAD_DIM)

    qkv, lru_in = _inproj(x2, row(pre_mix), w_in, qkv_width)
    y_attn, (w_out_bf, w_g_bf, w_u_bf) = _attention(qkv, batch, seq, n_heads, (w_out, w_g, w_u))
    y_lru = _lru_branch(lru_in, conv_w, row(conv_b), (0.5 * w_r).astype(BF16),
                        (0.5 * w_i).astype(BF16), row(0.5 * b_r), row(0.5 * b_i), row(lam),
                        row(lru_gain), batch, seq)
    x1 = _outproj(y_attn, y_lru, row(attn_gain), w_out_bf, x2, row(post_mix))
    return _ffn(x1, row(pre_ffn), w_g_bf, w_u_bf, w_d, row(post_ffn))


def kernel(x, pre_mix_norm, post_mix_norm, pre_ffn_norm, post_ffn_norm, w_in, conv_w, conv_b,
           w_rgate, b_rgate, w_igate, b_igate, lru_lambda, attn_out_norm, lru_out_norm, w_out,
           w_ffn_gate, w_ffn_up, w_ffn_down):
    batch, seq, d = x.shape
    x2 = x.reshape(batch * seq, d)
    for l in range(w_in.shape[0]):
        x2 = _layer(x2, batch, seq, pre_mix_norm[l], post_mix_norm[l], pre_ffn_norm[l],
                    post_ffn_norm[l], w_in[l], conv_w[l], conv_b[l], w_rgate[l], b_rgate[l],
                    w_igate[l], b_igate[l], lru_lambda[l], attn_out_norm[l], lru_out_norm[l],
                    w_out[l], w_ffn_gate[l], w_ffn_up[l], w_ffn_down[l])
    return x2.reshape(batch, seq, d)
```

```python
import functools
import math

import jax
import jax.numpy as jnp
from jax import lax
from jax.experimental import pallas as pl
from jax.experimental.pallas import tpu as pltpu

F32 = jnp.float32
BF16 = jnp.bfloat16

HEAD_DIM = 128
LRU_BLOCK = 128
LRU_SEG_PAD = 4
CONV_WIDTH = 4
LRU_C = 8.0
RMS_EPS = 1e-6
LOG2_E = math.log2(math.e)
Q_SCALE = LOG2_E / math.sqrt(HEAD_DIM)
F32_TINY = float(jnp.finfo(jnp.float32).tiny)
GELU_C1 = math.sqrt(2.0 / math.pi)
GELU_C3 = GELU_C1 * 0.044715
F32_ZERO_LOG2 = -150.0
SUBLANES = 8
NORM_ROWS = 128
FFN_SLAB = 256
VMEM_LIMIT = 60 * 1024 * 1024


def _params(n_axes):
    return pltpu.CompilerParams(dimension_semantics=("arbitrary",) * n_axes,
                                vmem_limit_bytes=VMEM_LIMIT)


def _rms(x, gain):
    var = jnp.mean(x * x, axis=-1, keepdims=True)
    return (x * lax.rsqrt(var + RMS_EPS)) * gain


def _log_sigmoid(z):
    return jnp.minimum(z, 0.0) - jnp.log1p(jnp.exp(-jnp.abs(z)))


def _inproj_kernel(x_hbm, g_ref, w_ref, qkv_ref, lru_ref, hn_ref, xbuf, sem, *, n_q_tiles):
    i, j = pl.program_id(0), pl.program_id(1)
    tm = xbuf.shape[0]

    def fetch(tile):
        return pltpu.make_async_copy(x_hbm.at[pl.ds(tile * tm, tm), :], xbuf, sem)

    @pl.when(jnp.logical_and(i == 0, j == 0))
    def _():
        fetch(0).start()

    def project(rows):
        acc = jnp.dot(hn_ref[rows, :], w_ref[...].astype(BF16), preferred_element_type=F32)
        qkv_ref[rows, :] = (acc * jnp.where(j < n_q_tiles, Q_SCALE, 1.0)).astype(BF16)
        lru_ref[rows, :] = acc

    @pl.when(j == 0)
    def _():
        fetch(i).wait()
        for h0 in range(0, tm, tm // 2):
            for r0 in range(h0, h0 + tm // 2, NORM_ROWS):
                rows = slice(r0, r0 + NORM_ROWS)
                hn_ref[rows, :] = _rms(xbuf[rows, :], g_ref[...]).astype(BF16)
            project(slice(h0, h0 + tm // 2))

    @pl.when(jnp.logical_and(j == 1, i + 1 < pl.num_programs(0)))
    def _():
        fetch(i + 1).start()

    @pl.when(j > 0)
    def _():
        project(slice(0, tm))


def _inproj(x2, gain, w, qkv_width, tm=2048, tn=512):
    m, d = x2.shape
    n = w.shape[1]
    lru_width = n - qkv_width
    n_qkv_tiles = qkv_width // tn
    n_q_tiles = qkv_width // (3 * tn)
    grid = (m // tm, n // tn)
    return pl.pallas_call(
        functools.partial(_inproj_kernel, n_q_tiles=n_q_tiles),
        grid=grid,
        in_specs=[
            pl.BlockSpec(memory_space=pl.ANY),
            pl.BlockSpec((1, d), lambda i, j: (0, 0)),
            pl.BlockSpec((d, tn), lambda i, j: (0, j)),
        ],
        out_specs=[
            pl.BlockSpec((tm, tn), lambda i, j: (i, jnp.minimum(j, n_qkv_tiles))),
            pl.BlockSpec((tm, tn), lambda i, j: (i, jnp.maximum(j - n_qkv_tiles, 0))),
        ],
        out_shape=[
            jax.ShapeDtypeStruct((m, qkv_width + tn), BF16),
            jax.ShapeDtypeStruct((m, lru_width), F32),
        ],
        scratch_shapes=[pltpu.VMEM((tm, d), BF16), pltpu.VMEM((tm, d), F32),
                        pltpu.SemaphoreType.DMA(())],
        compiler_params=_params(2),
        name="inproj",
    )(x2, gain, w)


def _attn_kernel(q_ref, k_ref, v_ref, *refs, tq, subs, heads, n_cast):
    cast_in, o_ref = refs[:n_cast], refs[n_cast]
    cast_out, tail_ref, live_ref = refs[n_cast + 1:2 * n_cast + 1], *refs[2 * n_cast + 1:]

    def cast_weights():
        for src, dst in zip(cast_in, cast_out):
            dst[...] = src[...].astype(BF16)

    tk = tq
    step = pl.program_id(2)
    row = lax.broadcasted_iota(jnp.int32, (tk, tk), 0)
    col = lax.broadcasted_iota(jnp.int32, (tk, tk), 1)
    neg_later = jnp.where(row > col, -1.0, 0.0).astype(BF16)

    def tile(s, kj, first):
        rows = slice(s * tq, (s + 1) * tq)
        start = pl.multiple_of(kj * tk, tk)
        log_betas = []
        softplus = []
        for h in range(heads):
            sl = slice(h * HEAD_DIM, (h + 1) * HEAD_DIM)
            k_t = k_ref[pl.ds(start, tk), sl]
            z2 = lax.dot_general(q_ref[rows, sl], k_t, (((1,), (1,)), ((), ())),
                                 preferred_element_type=F32)
            pos = jnp.maximum(z2, 0.0)
            neg = z2 - pos
            log_term = jnp.log(1.0 + jnp.exp2(neg - pos)) * LOG2_E
            sp = pos + log_term
            if first:
                sp = jnp.where(col < row, sp, 0.0)
            softplus.append(sp)
            log_betas.append(neg - log_term)
        sp_all = jnp.concatenate(softplus, axis=0).astype(BF16)
        suffix_all = jnp.dot(sp_all, neg_later, preferred_element_type=F32)
        for h in range(heads):
            sl = slice(h * HEAD_DIM, (h + 1) * HEAD_DIM)
            v_t = v_ref[pl.ds(start, tk), sl]
            suffix = suffix_all[h * tq:(h + 1) * tq]
            expo = log_betas[h] + suffix
            if not first:
                expo = expo + jnp.concatenate([tail_ref[s, h]] * (tk // HEAD_DIM), axis=1)
            w = jnp.exp2(expo)
            if first:
                w = jnp.where(col < row, w, 0.0)
            pv = jnp.dot(w.astype(BF16), v_t, preferred_element_type=F32)
            total = jnp.broadcast_to(suffix[:, 0:1] - softplus[h][:, 0:1], (tq, HEAD_DIM))
            if first:
                o_ref[rows, sl] = pv
                tail_ref[s, h] = total
            else:
                o_ref[rows, sl] += pv
                tail_ref[s, h] += total

    def stick_left(s):
        return jnp.max(tail_ref[s]) > F32_ZERO_LOG2

    @pl.when(step == 0)
    def _():
        cast_weights()
        for s in range(subs):
            tile(s, s, True)
            if s > 0:
                tile(s, s - 1, False)
            live_ref[s] = stick_left(s).astype(jnp.int32)

    @pl.when(step > 0)
    def _():
        cast_weights()
        for s in range(subs):
            tile(s, step * subs + s, True)
            tile(s, step * subs + s - 1, False)
            live_ref[s] = stick_left(s).astype(jnp.int32)

    for s in range(subs):
        qb = step * subs + s

        def body(carry, s=s, qb=qb):
            it, _ = carry
            tile(s, qb - 1 - it, False)
            return it + 1, stick_left(s)

        lax.while_loop(lambda c, qb=qb: jnp.logical_and(c[0] < qb, c[1]), body,
                       (jnp.int32(1), live_ref[s] > 0))


def _attention(qkv, batch, seq, n_heads, later_weights, tq=256, subs=2, heads=8):
    m = qkv.shape[0]
    rows = tq * subs
    nq = seq // rows
    ng = n_heads // heads
    width = heads * HEAD_DIM
    n_steps = batch * ng * nq
    step = lambda b, g, i: ((b * ng + g) * nq + i, 0)
    slab_specs = [pl.BlockSpec((w.shape[0] // n_steps, w.shape[1]), step) for w in later_weights]
    outs = pl.pallas_call(
        functools.partial(_attn_kernel, tq=tq, subs=subs, heads=heads,
                          n_cast=len(later_weights)),
        grid=(batch, ng, nq),
        in_specs=[
            pl.BlockSpec((rows, width), lambda b, g, i: (b * nq + i, g)),
            pl.BlockSpec((seq, width), lambda b, g, i: (b, ng + g), pipeline_mode=pl.Buffered(1)),
            pl.BlockSpec((seq, width), lambda b, g, i: (b, 2 * ng + g),
                         pipeline_mode=pl.Buffered(1)),
            *slab_specs,
        ],
        out_specs=[pl.BlockSpec((rows, width), lambda b, g, i: (b * nq + i, g)), *slab_specs],
        out_shape=[jax.ShapeDtypeStruct((m, n_heads * HEAD_DIM), F32),
                   *[jax.ShapeDtypeStruct(w.shape, BF16) for w in later_weights]],
        scratch_shapes=[pltpu.VMEM((subs, heads, tq, HEAD_DIM), F32),
                        pltpu.SMEM((subs,), jnp.int32)],
        compiler_params=_params(3),
        name="stickbreak_attn",
    )(qkv, qkv, qkv, *later_weights)
    return outs[0], outs[1:]


def _lru_kernel(x_ref, gate_ref, cw_ref, cb_ref, wr_ref, wi_ref, br_ref, bi_ref, lam_ref,
                gn_ref, o_ref, xtail, stage, h_buf, hcarry, *, tt):
    t = pl.program_id(1)
    c = x_ref.shape[1]
    seg = tt // SUBLANES
    pitch = stage.shape[1] // SUBLANES

    @pl.when(t == 0)
    def _():
        xtail[...] = jnp.zeros((SUBLANES, c), F32)
        hcarry[...] = jnp.zeros((SUBLANES, c), F32)

    first_sublane = lax.broadcasted_iota(jnp.int32, (SUBLANES, LRU_BLOCK), 0) == 0
    log_sig_lam = _log_sigmoid(lam_ref[...])
    log2_a_coef = (0.5 * LRU_C * LOG2_E) * log_sig_lam
    neg_log_a_coef = (-0.5 * LRU_C) * log_sig_lam
    for n in range(c // LRU_BLOCK):
        sl = slice(n * LRU_BLOCK, (n + 1) * LRU_BLOCK)
        for k in range(SUBLANES):
            stage[n, k * pitch:k * pitch + seg, :] = x_ref[k * seg:(k + 1) * seg, sl]
        xs = [stage[n, pl.ds(j, SUBLANES, stride=pitch), :] for j in range(seg)]
        before = [jnp.where(first_sublane, xtail[SUBLANES - back:SUBLANES - back + 1, sl],
                            pltpu.roll(xs[seg - back], 1, 0))
                  for back in range(CONV_WIDTH - 1, 0, -1)]
        xs = before + xs
        taps = [cw_ref[k:k + 1, sl] for k in range(CONV_WIDTH)]
        bias = cb_ref[:, sl]
        conv = []
        for j in range(seg):
            acc = bias + taps[0] * xs[j]
            for k in range(1, CONV_WIDTH):
                acc = acc + taps[k] * xs[j + k]
            conv.append(acc)
        xc = jnp.concatenate(conv, axis=0)
        xcb = xc.astype(BF16)
        half_wr = (0.5 * wr_ref[n]).astype(BF16)
        half_wi = (0.5 * wi_ref[n]).astype(BF16)
        tanh_r = jnp.tanh(jnp.dot(xcb, half_wr, preferred_element_type=F32)
                          + 0.5 * br_ref[:, sl])
        tanh_i = jnp.tanh(jnp.dot(xcb, half_wi, preferred_element_type=F32)
                          + 0.5 * bi_ref[:, sl])
        two_r = tanh_r + 1.0
        a = jnp.exp2(two_r * log2_a_coef[:, sl])
        one_m_a2 = jnp.tanh(two_r * neg_log_a_coef[:, sl]) * (1.0 + a * a)
        mult = one_m_a2 * lax.rsqrt(jnp.maximum(one_m_a2, F32_TINY))
        b = mult * ((tanh_i + 1.0) * (0.5 * xc))

        a = a.reshape(seg, SUBLANES, LRU_BLOCK)
        b = b.reshape(seg, SUBLANES, LRU_BLOCK)
        hs, ps = [b[0]], [a[0]]
        for j in range(1, seg):
            hs.append(a[j] * hs[-1] + b[j])
            ps.append(a[j] * ps[-1])
        entering = [hcarry[0:1, sl]]
        for k in range(SUBLANES):
            entering.append(hs[-1][k:k + 1, :] + ps[-1][k:k + 1, :] * entering[-1])
        hcarry[:, sl] = jnp.broadcast_to(entering[-1], (SUBLANES, LRU_BLOCK))
        enter = jnp.concatenate(entering[:-1], axis=0)
        for j in range(seg):
            stage[n, pl.ds(j, SUBLANES, stride=pitch), :] = hs[j] + ps[j] * enter
        for k in range(SUBLANES):
            h_buf[k * seg:(k + 1) * seg, sl] = stage[n, k * pitch:k * pitch + seg, :]

    xtail[...] = x_ref[tt - SUBLANES:tt, :]

    g = gate_ref[...]
    inner = g * (GELU_C1 + GELU_C3 * (g * g))
    y = (h_buf[...] * (0.5 * g)) * (1.0 + jnp.tanh(inner))
    o_ref[...] = _rms(y, gn_ref[...]).astype(o_ref.dtype)


def _lru_branch(lru_in, conv_w, conv_b, w_r, w_i, b_r, b_i, lam, gain, batch, seq, tt=512):
    m = lru_in.shape[0]
    c = lru_in.shape[1] // 2
    nt = seq // tt
    n_blocks = c // LRU_BLOCK
    row = lambda b, t: (0, 0)
    return pl.pallas_call(
        functools.partial(_lru_kernel, tt=tt),
        grid=(batch, nt),
        in_specs=[
            pl.BlockSpec((tt, c), lambda b, t: (b * nt + t, 0)),
            pl.BlockSpec((tt, c), lambda b, t: (b * nt + t, 1)),
            pl.BlockSpec((CONV_WIDTH, c), row),
            pl.BlockSpec((1, c), row),
            pl.BlockSpec((n_blocks, LRU_BLOCK, LRU_BLOCK), lambda b, t: (0, 0, 0)),
            pl.BlockSpec((n_blocks, LRU_BLOCK, LRU_BLOCK), lambda b, t: (0, 0, 0)),
            pl.BlockSpec((1, c), row),
            pl.BlockSpec((1, c), row),
            pl.BlockSpec((1, c), row),
            pl.BlockSpec((1, c), row),
        ],
        out_specs=pl.BlockSpec((tt, c), lambda b, t: (b * nt + t, 0)),
        out_shape=jax.ShapeDtypeStruct((m, c), BF16),
        scratch_shapes=[
            pltpu.VMEM((SUBLANES, c), F32),
            pltpu.VMEM((n_blocks, SUBLANES * (tt // SUBLANES + LRU_SEG_PAD), LRU_BLOCK), F32),
            pltpu.VMEM((tt, c), F32),
            pltpu.VMEM((SUBLANES, c), F32),
        ],
        compiler_params=_params(2),
        name="rglru",
    )(lru_in, lru_in, conv_w, conv_b, w_r, w_i, b_r, b_i, lam, gain)


def _outproj_kernel(ya_ref, yl_ref, ga_ref, w_ref, x_ref, gpost_ref, x1_ref):
    ya = _rms(ya_ref[...], ga_ref[...]).astype(BF16)
    y = jnp.concatenate([ya, yl_ref[...]], axis=1)
    mix = jnp.dot(y, w_ref[...], preferred_element_type=F32)
    x1_ref[...] = x_ref[...] + _rms(mix, gpost_ref[...])


def _outproj(y_attn, y_lru, attn_gain, w_bf, x2, post_gain, tm=512):
    m, d = x2.shape
    ca = y_attn.shape[1]
    cl = y_lru.shape[1]
    row = lambda i: (0, 0)
    return pl.pallas_call(
        _outproj_kernel,
        grid=(m // tm,),
        in_specs=[
            pl.BlockSpec((tm, ca), lambda i: (i, 0)),
            pl.BlockSpec((tm, cl), lambda i: (i, 0)),
            pl.BlockSpec((1, ca), row),
            pl.BlockSpec((ca + cl, d), row, pipeline_mode=pl.Buffered(1)),
            pl.BlockSpec((tm, d), lambda i: (i, 0)),
            pl.BlockSpec((1, d), row),
        ],
        out_specs=pl.BlockSpec((tm, d), lambda i: (i, 0)),
        out_shape=jax.ShapeDtypeStruct((m, d), F32),
        compiler_params=_params(1),
        name="outproj",
    )(y_attn, y_lru, attn_gain, w_bf, x2, post_gain)


def _ffn_kernel(x1_ref, gpre_ref, wg_ref, wu_ref, wd_ref, gpost_ref, o_ref, hn_ref):
    f = pl.program_id(1)
    last = pl.num_programs(1) - 1
    half = o_ref.shape[0] // 2
    halves = [slice(0, half), slice(half, 2 * half)]

    def chunks(rows):
        return [slice(r0, r0 + NORM_ROWS) for r0 in range(rows.start, rows.stop, NORM_ROWS)]

    def pre_norm(rows):
        for r in chunks(rows):
            hn_ref[r, :] = _rms(x1_ref[r, :], gpre_ref[...]).astype(BF16)

    def post_norm_residual(rows):
        for r in chunks(rows):
            o_ref[r, :] = x1_ref[r, :] + _rms(o_ref[r, :], gpost_ref[...])

    def ffn(rows, first):
        hn = hn_ref[rows, :]
        for c0 in range(0, wg_ref.shape[1], FFN_SLAB):
            cols = slice(c0, c0 + FFN_SLAB)
            gate = jnp.dot(hn, wg_ref[:, cols], preferred_element_type=F32)
            up = jnp.dot(hn, wu_ref[:, cols], preferred_element_type=F32)
            act = (jax.nn.silu(gate) * up).astype(BF16)
            part = jnp.dot(act, wd_ref[cols, :].astype(BF16), preferred_element_type=F32)
            if first and c0 == 0:
                o_ref[rows, :] = part
            else:
                o_ref[rows, :] += part

    @pl.when(f == 0)
    def _():
        for rows in halves:
            pre_norm(rows)
            ffn(rows, True)

    @pl.when(jnp.logical_and(f > 0, f < last))
    def _():
        for rows in halves:
            ffn(rows, False)

    @pl.when(f == last)
    def _():
        for rows in halves:
            ffn(rows, False)
            post_norm_residual(rows)


def _ffn(x1, pre_gain, wg, wu, wd, post_gain, tm=1024, tf=512):
    m, d = x1.shape
    dff = wg.shape[1]
    return pl.pallas_call(
        _ffn_kernel,
        grid=(m // tm, dff // tf),
        in_specs=[
            pl.BlockSpec((tm, d), lambda i, f: (i, 0)),
            pl.BlockSpec((1, d), lambda i, f: (0, 0)),
            pl.BlockSpec((d, tf), lambda i, f: (0, f)),
            pl.BlockSpec((d, tf), lambda i, f: (0, f)),
            pl.BlockSpec((tf, d), lambda i, f: (f, 0)),
            pl.BlockSpec((1, d), lambda i, f: (0, 0)),
        ],
        out_specs=pl.BlockSpec((tm, d), lambda i, f: (i, 0)),
        out_shape=jax.ShapeDtypeStruct((m, d), F32),
        scratch_shapes=[pltpu.VMEM((tm, d), BF16)],
        compiler_params=_params(2),
        name="swiglu_ffn",
    )(x1, pre_gain, wg, wu, wd, post_gain)


def _layer(x2, batch, seq, pre_mix, post_mix, pre_ffn, post_ffn, w_in, conv_w, conv_b, w_r, b_r,
           w_i, b_i, lam, attn_gain, lru_gain, w_out, w_g, w_u, w_d):
    row = lambda v: v.reshape(1, -1)
    lru_width = conv_w.shape[1]
    qkv_width = w_in.shape[1] - 2 * lru_width
    n_heads = qkv_width // (3 * HEAD_DIM)

    qkv, lru_in = _inproj(x2, row(pre_mix), w_in, qkv_width)
    y_attn, (w_out_bf, w_g_bf, w_u_bf) = _attention(qkv, batch, seq, n_heads, (w_out, w_g, w_u))
    y_lru = _lru_branch(lru_in, conv_w, row(conv_b), w_r, w_i, row(b_r), row(b_i), row(lam),
                        row(lru_gain), batch, seq)
    x1 = _outproj(y_attn, y_lru, row(attn_gain), w_out_bf, x2, row(post_mix))
    return _ffn(x1, row(pre_ffn), w_g_bf, w_u_bf, w_d, row(post_ffn))


def kernel(x, pre_mix_norm, post_mix_norm, pre_ffn_norm, post_ffn_norm, w_in, conv_w, conv_b,
           w_rgate, b_rgate, w_igate, b_igate, lru_lambda, attn_out_norm, lru_out_norm, w_out,
           w_ffn_gate, w_ffn_up, w_ffn_down):
    batch, seq, d = x.shape
    x2 = x.reshape(batch * seq, d)
    for l in range(w_in.shape[0]):
        x2 = _layer(x2, batch, seq, pre_mix_norm[l], post_mix_norm[l], pre_ffn_norm[l],
                    post_ffn_norm[l], w_in[l], conv_w[l], conv_b[l], w_rgate[l], b_rgate[l],
                    w_igate[l], b_igate[l], lru_lambda[l], attn_out_norm[l], lru_out_norm[l],
                    w_out[l], w_ffn_gate[l], w_ffn_up[l], w_ffn_down[l])
    return x2.reshape(batch, seq, d)
```

```python
import functools
import math

import jax
import jax.numpy as jnp
from jax import lax
from jax.experimental import pallas as pl
from jax.experimental.pallas import tpu as pltpu

F32 = jnp.float32
BF16 = jnp.bfloat16

HEAD_DIM = 128
LRU_BLOCK = 128
LRU_SEG_PAD = 4
CONV_WIDTH = 4
LRU_C = 8.0
RMS_EPS = 1e-6
LOG2_E = math.log2(math.e)
Q_SCALE = LOG2_E / math.sqrt(HEAD_DIM)
F32_TINY = float(jnp.finfo(jnp.float32).tiny)
GELU_C1 = math.sqrt(2.0 / math.pi)
GELU_C3 = GELU_C1 * 0.044715
F32_ZERO_LOG2 = -150.0
SUBLANES = 8
NORM_ROWS = 128
FFN_SLAB = 256
VMEM_LIMIT = 60 * 1024 * 1024


def _params(n_axes):
    return pltpu.CompilerParams(dimension_semantics=("arbitrary",) * n_axes,
                                vmem_limit_bytes=VMEM_LIMIT)


def _rms(x, gain):
    var = jnp.mean(x * x, axis=-1, keepdims=True)
    return (x * lax.rsqrt(var + RMS_EPS)) * gain


def _log_sigmoid(z):
    return jnp.minimum(z, 0.0) - jnp.log1p(jnp.exp(-jnp.abs(z)))


def _inproj_kernel(x_hbm, g_ref, w_ref, qkv_ref, lru_ref, hn_ref, xbuf, sem, *, n_q_tiles):
    i, j = pl.program_id(0), pl.program_id(1)
    tm = xbuf.shape[0]

    def fetch(tile):
        return pltpu.make_async_copy(x_hbm.at[pl.ds(tile * tm, tm), :], xbuf, sem)

    @pl.when(jnp.logical_and(i == 0, j == 0))
    def _():
        fetch(0).start()

    def project(rows):
        acc = jnp.dot(hn_ref[rows, :], w_ref[...].astype(BF16), preferred_element_type=F32)
        qkv_ref[rows, :] = (acc * jnp.where(j < n_q_tiles, Q_SCALE, 1.0)).astype(BF16)
        lru_ref[rows, :] = acc

    @pl.when(j == 0)
    def _():
        fetch(i).wait()
        for h0 in range(0, tm, tm // 2):
            for r0 in range(h0, h0 + tm // 2, NORM_ROWS):
                rows = slice(r0, r0 + NORM_ROWS)
                hn_ref[rows, :] = _rms(xbuf[rows, :], g_ref[...]).astype(BF16)
            project(slice(h0, h0 + tm // 2))

    @pl.when(jnp.logical_and(j == 1, i + 1 < pl.num_programs(0)))
    def _():
        fetch(i + 1).start()

    @pl.when(j > 0)
    def _():
        project(slice(0, tm))


def _inproj(x2, gain, w, qkv_width, tm=2048, tn=512):
    m, d = x2.shape
    n = w.shape[1]
    lru_width = n - qkv_width
    n_qkv_tiles = qkv_width // tn
    n_q_tiles = qkv_width // (3 * tn)
    grid = (m // tm, n // tn)
    return pl.pallas_call(
        functools.partial(_inproj_kernel, n_q_tiles=n_q_tiles),
        grid=grid,
        in_specs=[
            pl.BlockSpec(memory_space=pl.ANY),
            pl.BlockSpec((1, d), lambda i, j: (0, 0)),
            pl.BlockSpec((d, tn), lambda i, j: (0, j)),
        ],
        out_specs=[
            pl.BlockSpec((tm, tn), lambda i, j: (i, jnp.minimum(j, n_qkv_tiles))),
            pl.BlockSpec((tm, tn), lambda i, j: (i, jnp.maximum(j - n_qkv_tiles, 0))),
        ],
        out_shape=[
            jax.ShapeDtypeStruct((m, qkv_width + tn), BF16),
            jax.ShapeDtypeStruct((m, lru_width), F32),
        ],
        scratch_shapes=[pltpu.VMEM((tm, d), BF16), pltpu.VMEM((tm, d), F32),
                        pltpu.SemaphoreType.DMA(())],
        compiler_params=_params(2),
        name="inproj",
    )(x2, gain, w)


def _attn_kernel(q_ref, qkv_hbm, *refs, tq, subs, heads, n_cast, k_col, v_col):
    cast_in, o_ref = refs[:n_cast], refs[n_cast]
    cast_out = refs[n_cast + 1:2 * n_cast + 1]
    tail_ref, live_ref, k_ref, v_ref, kv_sems = refs[2 * n_cast + 1:]

    def cast_weights():
        for src, dst in zip(cast_in, cast_out):
            dst[...] = src[...].astype(BF16)

    tk = tq
    step = pl.program_id(2)
    rows_per_step = subs * tq
    seq = k_ref.shape[0]
    width = k_ref.shape[1]

    def kv_fetch(chunk):
        src_rows = pl.ds(pl.program_id(0) * seq + chunk * rows_per_step, rows_per_step)
        dst_rows = pl.ds(chunk * rows_per_step, rows_per_step)
        group = pl.program_id(1)
        return [pltpu.make_async_copy(
                    qkv_hbm.at[src_rows, pl.ds(col0 + group * width, width)],
                    buf.at[dst_rows, :], kv_sems.at[which])
                for which, (col0, buf) in enumerate(((k_col, k_ref), (v_col, v_ref)))]

    @pl.when(step == 0)
    def _():
        for copy in kv_fetch(0):
            copy.start()

    for copy in kv_fetch(step):
        copy.wait()

    @pl.when(step + 1 < pl.num_programs(2))
    def _():
        for copy in kv_fetch(step + 1):
            copy.start()

    row = lax.broadcasted_iota(jnp.int32, (tk, tk), 0)
    col = lax.broadcasted_iota(jnp.int32, (tk, tk), 1)
    neg_later = jnp.where(row > col, -1.0, 0.0).astype(BF16)

    def tile(s, kj, first):
        rows = slice(s * tq, (s + 1) * tq)
        start = pl.multiple_of(kj * tk, tk)
        log_betas = []
        softplus = []
        for h in range(heads):
            sl = slice(h * HEAD_DIM, (h + 1) * HEAD_DIM)
            k_t = k_ref[pl.ds(start, tk), sl]
            z2 = lax.dot_general(q_ref[rows, sl], k_t, (((1,), (1,)), ((), ())),
                                 preferred_element_type=F32)
            pos = jnp.maximum(z2, 0.0)
            neg = z2 - pos
            log_term = jnp.log(1.0 + jnp.exp2(neg - pos)) * LOG2_E
            sp = pos + log_term
            if first:
                sp = jnp.where(col < row, sp, 0.0)
            softplus.append(sp)
            log_betas.append(neg - log_term)
        sp_all = jnp.concatenate(softplus, axis=0).astype(BF16)
        suffix_all = jnp.dot(sp_all, neg_later, preferred_element_type=F32)
        for h in range(heads):
            sl = slice(h * HEAD_DIM, (h + 1) * HEAD_DIM)
            v_t = v_ref[pl.ds(start, tk), sl]
            suffix = suffix_all[h * tq:(h + 1) * tq]
            expo = log_betas[h] + suffix
            if not first:
                expo = expo + jnp.concatenate([tail_ref[s, h]] * (tk // HEAD_DIM), axis=1)
            w = jnp.exp2(expo)
            if first:
                w = jnp.where(col < row, w, 0.0)
            pv = jnp.dot(w.astype(BF16), v_t, preferred_element_type=F32)
            total = jnp.broadcast_to(suffix[:, 0:1] - softplus[h][:, 0:1], (tq, HEAD_DIM))
            if first:
                o_ref[rows, sl] = pv
                tail_ref[s, h] = total
            else:
                o_ref[rows, sl] += pv
                tail_ref[s, h] += total

    def stick_left(s):
        return jnp.max(tail_ref[s]) > F32_ZERO_LOG2

    @pl.when(step == 0)
    def _():
        cast_weights()
        for s in range(subs):
            tile(s, s, True)
            if s > 0:
                tile(s, s - 1, False)
            live_ref[s] = stick_left(s).astype(jnp.int32)

    @pl.when(step > 0)
    def _():
        cast_weights()
        for s in range(subs):
            tile(s, step * subs + s, True)
            tile(s, step * subs + s - 1, False)
            live_ref[s] = stick_left(s).astype(jnp.int32)

    for s in range(subs):
        qb = step * subs + s

        def body(carry, s=s, qb=qb):
            it, _ = carry
            tile(s, qb - 1 - it, False)
            return it + 1, stick_left(s)

        lax.while_loop(lambda c, qb=qb: jnp.logical_and(c[0] < qb, c[1]), body,
                       (jnp.int32(1), live_ref[s] > 0))


def _attention(qkv, batch, seq, n_heads, later_weights, tq=256, subs=2, heads=8):
    m = qkv.shape[0]
    rows = tq * subs
    nq = seq // rows
    ng = n_heads // heads
    width = heads * HEAD_DIM
    n_steps = batch * ng * nq
    step = lambda b, g, i: ((b * ng + g) * nq + i, 0)
    slab_specs = [pl.BlockSpec((w.shape[0] // n_steps, w.shape[1]), step) for w in later_weights]
    outs = pl.pallas_call(
        functools.partial(_attn_kernel, tq=tq, subs=subs, heads=heads,
                          n_cast=len(later_weights), k_col=ng * width, v_col=2 * ng * width),
        grid=(batch, ng, nq),
        in_specs=[
            pl.BlockSpec((rows, width), lambda b, g, i: (b * nq + i, g)),
            pl.BlockSpec(memory_space=pl.ANY),
            *slab_specs,
        ],
        out_specs=[pl.BlockSpec((rows, width), lambda b, g, i: (b * nq + i, g)), *slab_specs],
        out_shape=[jax.ShapeDtypeStruct((m, n_heads * HEAD_DIM), F32),
                   *[jax.ShapeDtypeStruct(w.shape, BF16) for w in later_weights]],
        scratch_shapes=[pltpu.VMEM((subs, heads, tq, HEAD_DIM), F32),
                        pltpu.SMEM((subs,), jnp.int32),
                        pltpu.VMEM((seq, width), BF16),
                        pltpu.VMEM((seq, width), BF16),
                        pltpu.SemaphoreType.DMA((2,))],
        compiler_params=_params(3),
        name="stickbreak_attn",
    )(qkv, qkv, *later_weights)
    return outs[0], outs[1:]


def _lru_kernel(x_ref, gate_ref, cw_ref, cb_ref, wr_ref, wi_ref, br_ref, bi_ref, lam_ref,
                gn_ref, o_ref, xtail, stage, h_buf, hcarry, *, tt):
    t = pl.program_id(1)
    c = x_ref.shape[1]
    seg = tt // SUBLANES
    pitch = stage.shape[1] // SUBLANES

    @pl.when(t == 0)
    def _():
        xtail[...] = jnp.zeros((SUBLANES, c), F32)
        hcarry[...] = jnp.zeros((SUBLANES, c), F32)

    first_sublane = lax.broadcasted_iota(jnp.int32, (SUBLANES, LRU_BLOCK), 0) == 0
    log_sig_lam = _log_sigmoid(lam_ref[...])
    log2_a_coef = (0.5 * LRU_C * LOG2_E) * log_sig_lam
    neg_log_a_coef = (-0.5 * LRU_C) * log_sig_lam
    for n in range(c // LRU_BLOCK):
        sl = slice(n * LRU_BLOCK, (n + 1) * LRU_BLOCK)
        for k in range(SUBLANES):
            stage[n, k * pitch:k * pitch + seg, :] = x_ref[k * seg:(k + 1) * seg, sl]
        xs = [stage[n, pl.ds(j, SUBLANES, stride=pitch), :] for j in range(seg)]
        before = [jnp.where(first_sublane, xtail[SUBLANES - back:SUBLANES - back + 1, sl],
                            pltpu.roll(xs[seg - back], 1, 0))
                  for back in range(CONV_WIDTH - 1, 0, -1)]
        xs = before + xs
        taps = [cw_ref[k:k + 1, sl] for k in range(CONV_WIDTH)]
        bias = cb_ref[:, sl]
        conv = []
        for j in range(seg):
            acc = bias + taps[0] * xs[j]
            for k in range(1, CONV_WIDTH):
                acc = acc + taps[k] * xs[j + k]
            conv.append(acc)
        xc = jnp.concatenate(conv, axis=0)
        xcb = xc.astype(BF16)
        half_wr = (0.5 * wr_ref[n]).astype(BF16)
        half_wi = (0.5 * wi_ref[n]).astype(BF16)
        tanh_r = jnp.tanh(jnp.dot(xcb, half_wr, preferred_element_type=F32)
                          + 0.5 * br_ref[:, sl])
        tanh_i = jnp.tanh(jnp.dot(xcb, half_wi, preferred_element_type=F32)
                          + 0.5 * bi_ref[:, sl])
        two_r = tanh_r + 1.0
        a = jnp.exp2(two_r * log2_a_coef[:, sl])
        one_m_a2 = jnp.tanh(two_r * neg_log_a_coef[:, sl]) * (1.0 + a * a)
        mult = one_m_a2 * lax.rsqrt(jnp.maximum(one_m_a2, F32_TINY))
        b = mult * ((tanh_i + 1.0) * (0.5 * xc))

        a = a.reshape(seg, SUBLANES, LRU_BLOCK)
        b = b.reshape(seg, SUBLANES, LRU_BLOCK)
        hs, ps = [b[0]], [a[0]]
        for j in range(1, seg):
            hs.append(a[j] * hs[-1] + b[j])
            ps.append(a[j] * ps[-1])
        entering = [hcarry[0:1, sl]]
        for k in range(SUBLANES):
            entering.append(hs[-1][k:k + 1, :] + ps[-1][k:k + 1, :] * entering[-1])
        hcarry[:, sl] = jnp.broadcast_to(entering[-1], (SUBLANES, LRU_BLOCK))
        enter = jnp.concatenate(entering[:-1], axis=0)
        for j in range(seg):
            stage[n, pl.ds(j, SUBLANES, stride=pitch), :] = hs[j] + ps[j] * enter
        for k in range(SUBLANES):
            h_buf[k * seg:(k + 1) * seg, sl] = stage[n, k * pitch:k * pitch + seg, :]

    xtail[...] = x_ref[tt - SUBLANES:tt, :]

    g = gate_ref[...]
    inner = g * (GELU_C1 + GELU_C3 * (g * g))
    y = (h_buf[...] * (0.5 * g)) * (1.0 + jnp.tanh(inner))
    o_ref[...] = _rms(y, gn_ref[...]).astype(o_ref.dtype)


def _lru_branch(lru_in, conv_w, conv_b, w_r, w_i, b_r, b_i, lam, gain, batch, seq, tt=512):
    m = lru_in.shape[0]
    c = lru_in.shape[1] // 2
    nt = seq // tt
    n_blocks = c // LRU_BLOCK
    row = lambda b, t: (0, 0)
    return pl.pallas_call(
        functools.partial(_lru_kernel, tt=tt),
        grid=(batch, nt),
        in_specs=[
            pl.BlockSpec((tt, c), lambda b, t: (b * nt + t, 0)),
            pl.BlockSpec((tt, c), lambda b, t: (b * nt + t, 1)),
            pl.BlockSpec((CONV_WIDTH, c), row),
            pl.BlockSpec((1, c), row),
            pl.BlockSpec((n_blocks, LRU_BLOCK, LRU_BLOCK), lambda b, t: (0, 0, 0)),
            pl.BlockSpec((n_blocks, LRU_BLOCK, LRU_BLOCK), lambda b, t: (0, 0, 0)),
            pl.BlockSpec((1, c), row),
            pl.BlockSpec((1, c), row),
            pl.BlockSpec((1, c), row),
            pl.BlockSpec((1, c), row),
        ],
        out_specs=pl.BlockSpec((tt, c), lambda b, t: (b * nt + t, 0)),
        out_shape=jax.ShapeDtypeStruct((m, c), BF16),
        scratch_shapes=[
            pltpu.VMEM((SUBLANES, c), F32),
            pltpu.VMEM((n_blocks, SUBLANES * (tt // SUBLANES + LRU_SEG_PAD), LRU_BLOCK), F32),
            pltpu.VMEM((tt, c), F32),
            pltpu.VMEM((SUBLANES, c), F32),
        ],
        compiler_params=_params(2),
        name="rglru",
    )(lru_in, lru_in, conv_w, conv_b, w_r, w_i, b_r, b_i, lam, gain)


def _outproj_kernel(ya_ref, yl_ref, ga_ref, w_ref, x_ref, gpost_ref, x1_ref):
    ya = _rms(ya_ref[...], ga_ref[...]).astype(BF16)
    y = jnp.concatenate([ya, yl_ref[...]], axis=1)
    mix = jnp.dot(y, w_ref[...], preferred_element_type=F32)
    x1_ref[...] = x_ref[...] + _rms(mix, gpost_ref[...])


def _outproj(y_attn, y_lru, attn_gain, w_bf, x2, post_gain, tm=512):
    m, d = x2.shape
    ca = y_attn.shape[1]
    cl = y_lru.shape[1]
    row = lambda i: (0, 0)
    return pl.pallas_call(
        _outproj_kernel,
        grid=(m // tm,),
        in_specs=[
            pl.BlockSpec((tm, ca), lambda i: (i, 0)),
            pl.BlockSpec((tm, cl), lambda i: (i, 0)),
            pl.BlockSpec((1, ca), row),
            pl.BlockSpec((ca + cl, d), row, pipeline_mode=pl.Buffered(1)),
            pl.BlockSpec((tm, d), lambda i: (i, 0)),
            pl.BlockSpec((1, d), row),
        ],
        out_specs=pl.BlockSpec((tm, d), lambda i: (i, 0)),
        out_shape=jax.ShapeDtypeStruct((m, d), F32),
        compiler_params=_params(1),
        name="outproj",
    )(y_attn, y_lru, attn_gain, w_bf, x2, post_gain)


def _ffn_kernel(x1_ref, gpre_ref, wg_ref, wu_ref, wd_ref, gpost_ref, o_ref, hn_ref):
    f = pl.program_id(1)
    last = pl.num_programs(1) - 1
    half = o_ref.shape[0] // 2
    halves = [slice(0, half), slice(half, 2 * half)]

    def chunks(rows):
        return [slice(r0, r0 + NORM_ROWS) for r0 in range(rows.start, rows.stop, NORM_ROWS)]

    def pre_norm(rows):
        for r in chunks(rows):
            hn_ref[r, :] = _rms(x1_ref[r, :], gpre_ref[...]).astype(BF16)

    def post_norm_residual(rows):
        for r in chunks(rows):
            o_ref[r, :] = x1_ref[r, :] + _rms(o_ref[r, :], gpost_ref[...])

    def ffn(rows, first):
        hn = hn_ref[rows, :]
        for c0 in range(0, wg_ref.shape[1], FFN_SLAB):
            cols = slice(c0, c0 + FFN_SLAB)
            gate = jnp.dot(hn, wg_ref[:, cols], preferred_element_type=F32)
            up = jnp.dot(hn, wu_ref[:, cols], preferred_element_type=F32)
            act = (jax.nn.silu(gate) * up).astype(BF16)
            part = jnp.dot(act, wd_ref[cols, :].astype(BF16), preferred_element_type=F32)
            if first and c0 == 0:
                o_ref[rows, :] = part
            else:
                o_ref[rows, :] += part

    @pl.when(f == 0)
    def _():
        for rows in halves:
            pre_norm(rows)
            ffn(rows, True)

    @pl.when(jnp.logical_and(f > 0, f < last))
    def _():
        for rows in halves:
            ffn(rows, False)

    @pl.when(f == last)
    def _():
        for rows in halves:
            ffn(rows, False)
            post_norm_residual(rows)


def _ffn(x1, pre_gain, wg, wu, wd, post_gain, tm=1024, tf=512):
    m, d = x1.shape
    dff = wg.shape[1]
    return pl.pallas_call(
        _ffn_kernel,
        grid=(m // tm, dff // tf),
        in_specs=[
            pl.BlockSpec((tm, d), lambda i, f: (i, 0)),
            pl.BlockSpec((1, d), lambda i, f: (0, 0)),
            pl.BlockSpec((d, tf), lambda i, f: (0, f)),
            pl.BlockSpec((d, tf), lambda i, f: (0, f)),
            pl.BlockSpec((tf, d), lambda i, f: (f, 0)),
            pl.BlockSpec((1, d), lambda i, f: (0, 0)),
        ],
        out_specs=pl.BlockSpec((tm, d), lambda i, f: (i, 0)),
        out_shape=jax.ShapeDtypeStruct((m, d), F32),
        scratch_shapes=[pltpu.VMEM((tm, d), BF16)],
        compiler_params=_params(2),
        name="swiglu_ffn",
    )(x1, pre_gain, wg, wu, wd, post_gain)


def _layer(x2, batch, seq, pre_mix, post_mix, pre_ffn, post_ffn, w_in, conv_w, conv_b, w_r, b_r,
           w_i, b_i, lam, attn_gain, lru_gain, w_out, w_g, w_u, w_d):
    row = lambda v: v.reshape(1, -1)
    lru_width = conv_w.shape[1]
    qkv_width = w_in.shape[1] - 2 * lru_width
    n_heads = qkv_width // (3 * HEAD_DIM)

    qkv, lru_in = _inproj(x2, row(pre_mix), w_in, qkv_width)
    y_attn, (w_out_bf, w_g_bf, w_u_bf) = _attention(qkv, batch, seq, n_heads, (w_out, w_g, w_u))
    y_lru = _lru_branch(lru_in, conv_w, row(conv_b), w_r, w_i, row(b_r), row(b_i), row(lam),
                        row(lru_gain), batch, seq)
    x1 = _outproj(y_attn, y_lru, row(attn_gain), w_out_bf, x2, row(post_mix))
    return _ffn(x1, row(pre_ffn), w_g_bf, w_u_bf, w_d, row(post_ffn))


def kernel(x, pre_mix_norm, post_mix_norm, pre_ffn_norm, post_ffn_norm, w_in, conv_w, conv_b,
           w_rgate, b_rgate, w_igate, b_igate, lru_lambda, attn_out_norm, lru_out_norm, w_out,
           w_ffn_gate, w_ffn_up, w_ffn_down):
    batch, seq, d = x.shape
    x2 = x.reshape(batch * seq, d)
    for l in range(w_in.shape[0]):
        x2 = _layer(x2, batch, seq, pre_mix_norm[l], post_mix_norm[l], pre_ffn_norm[l],
                    post_ffn_norm[l], w_in[l], conv_w[l], conv_b[l], w_rgate[l], b_rgate[l],
                    w_igate[l], b_igate[l], lru_lambda[l], attn_out_norm[l], lru_out_norm[l],
                    w_out[l], w_ffn_gate[l], w_ffn_up[l], w_ffn_down[l])
    return x2.reshape(batch, seq, d)
```

```python
import functools
import math

import jax
import jax.numpy as jnp
from jax import lax
from jax.experimental import pallas as pl
from jax.experimental.pallas import tpu as pltpu

F32 = jnp.float32
BF16 = jnp.bfloat16

HEAD_DIM = 128
LRU_BLOCK = 128
LRU_SEG_PAD = 4
CONV_WIDTH = 4
LRU_C = 8.0
RMS_EPS = 1e-6
LOG2_E = math.log2(math.e)
Q_SCALE = LOG2_E / math.sqrt(HEAD_DIM)
F32_TINY = float(jnp.finfo(jnp.float32).tiny)
GELU_C1 = math.sqrt(2.0 / math.pi)
GELU_C3 = GELU_C1 * 0.044715
F32_ZERO_LOG2 = -150.0
SUBLANES = 8
NORM_ROWS = 128
FFN_SLAB = 256
VMEM_LIMIT = 60 * 1024 * 1024


def _params(n_axes):
    return pltpu.CompilerParams(dimension_semantics=("arbitrary",) * n_axes,
                                vmem_limit_bytes=VMEM_LIMIT)


def _rms(x, gain):
    var = jnp.mean(x * x, axis=-1, keepdims=True)
    return (x * lax.rsqrt(var + RMS_EPS)) * gain


def _log_sigmoid(z):
    return jnp.minimum(z, 0.0) - jnp.log1p(jnp.exp(-jnp.abs(z)))


def _inproj_kernel(x_hbm, g_ref, w_ref, qkv_ref, lru_ref, hn_ref, xbuf, sem, *, n_q_tiles):
    i, j = pl.program_id(0), pl.program_id(1)
    tm = xbuf.shape[0]

    def fetch(tile):
        return pltpu.make_async_copy(x_hbm.at[pl.ds(tile * tm, tm), :], xbuf, sem)

    @pl.when(jnp.logical_and(i == 0, j == 0))
    def _():
        fetch(0).start()

    def project(rows):
        acc = jnp.dot(hn_ref[rows, :], w_ref[...].astype(BF16), preferred_element_type=F32)
        qkv_ref[rows, :] = (acc * jnp.where(j < n_q_tiles, Q_SCALE, 1.0)).astype(BF16)
        lru_ref[rows, :] = acc

    @pl.when(j == 0)
    def _():
        fetch(i).wait()
        for h0 in range(0, tm, tm // 2):
            for r0 in range(h0, h0 + tm // 2, NORM_ROWS):
                rows = slice(r0, r0 + NORM_ROWS)
                hn_ref[rows, :] = _rms(xbuf[rows, :], g_ref[...]).astype(BF16)
            project(slice(h0, h0 + tm // 2))

    @pl.when(jnp.logical_and(j == 1, i + 1 < pl.num_programs(0)))
    def _():
        fetch(i + 1).start()

    @pl.when(j > 0)
    def _():
        project(slice(0, tm))


def _inproj(x2, gain, w, qkv_width, tm=2048, tn=512):
    m, d = x2.shape
    n = w.shape[1]
    lru_width = n - qkv_width
    assert m % tm == 0 and n % tn == 0 and qkv_width % (3 * tn) == 0 and tm % (2 * NORM_ROWS) == 0
    n_qkv_tiles = qkv_width // tn
    n_q_tiles = qkv_width // (3 * tn)
    grid = (m // tm, n // tn)
    return pl.pallas_call(
        functools.partial(_inproj_kernel, n_q_tiles=n_q_tiles),
        grid=grid,
        in_specs=[
            pl.BlockSpec(memory_space=pl.ANY),
            pl.BlockSpec((1, d), lambda i, j: (0, 0)),
            pl.BlockSpec((d, tn), lambda i, j: (0, j)),
        ],
        out_specs=[
            pl.BlockSpec((tm, tn), lambda i, j: (i, jnp.minimum(j, n_qkv_tiles))),
            pl.BlockSpec((tm, tn), lambda i, j: (i, jnp.maximum(j - n_qkv_tiles, 0))),
        ],
        out_shape=[
            jax.ShapeDtypeStruct((m, qkv_width + tn), BF16),
            jax.ShapeDtypeStruct((m, lru_width), F32),
        ],
        scratch_shapes=[pltpu.VMEM((tm, d), BF16), pltpu.VMEM((tm, d), F32),
                        pltpu.SemaphoreType.DMA(())],
        compiler_params=_params(2),
        name="inproj",
    )(x2, gain, w)


def _attn_kernel(q_ref, qkv_hbm, *refs, tq, subs, heads, n_cast, k_col, v_col):
    cast_in, o_ref = refs[:n_cast], refs[n_cast]
    cast_out = refs[n_cast + 1:2 * n_cast + 1]
    tail_ref, live_ref, k_ref, v_ref, kv_sems = refs[2 * n_cast + 1:]

    def cast_weights():
        for src, dst in zip(cast_in, cast_out):
            dst[...] = src[...].astype(BF16)

    tk = tq
    step = pl.program_id(2)
    rows_per_step = subs * tq
    seq = k_ref.shape[0]
    width = k_ref.shape[1]

    def kv_fetch(chunk):
        src_rows = pl.ds(pl.program_id(0) * seq + chunk * rows_per_step, rows_per_step)
        dst_rows = pl.ds(chunk * rows_per_step, rows_per_step)
        group = pl.program_id(1)
        return [pltpu.make_async_copy(
                    qkv_hbm.at[src_rows, pl.ds(col0 + group * width, width)],
                    buf.at[dst_rows, :], kv_sems.at[which])
                for which, (col0, buf) in enumerate(((k_col, k_ref), (v_col, v_ref)))]

    @pl.when(step == 0)
    def _():
        for copy in kv_fetch(0):
            copy.start()

    for copy in kv_fetch(step):
        copy.wait()

    @pl.when(step + 1 < pl.num_programs(2))
    def _():
        for copy in kv_fetch(step + 1):
            copy.start()

    row = lax.broadcasted_iota(jnp.int32, (tk, tk), 0)
    col = lax.broadcasted_iota(jnp.int32, (tk, tk), 1)
    neg_later = jnp.where(row > col, -1.0, 0.0).astype(BF16)

    def tile(s, kj, first):
        rows = slice(s * tq, (s + 1) * tq)
        start = pl.multiple_of(kj * tk, tk)
        log_betas = []
        softplus = []
        for h in range(heads):
            sl = slice(h * HEAD_DIM, (h + 1) * HEAD_DIM)
            k_t = k_ref[pl.ds(start, tk), sl]
            z2 = lax.dot_general(q_ref[rows, sl], k_t, (((1,), (1,)), ((), ())),
                                 preferred_element_type=F32)
            pos = jnp.maximum(z2, 0.0)
            neg = z2 - pos
            log_term = jnp.log(1.0 + jnp.exp2(neg - pos)) * LOG2_E
            sp = pos + log_term
            if first:
                sp = jnp.where(col < row, sp, 0.0)
            softplus.append(sp)
            log_betas.append(neg - log_term)
        sp_all = jnp.concatenate(softplus, axis=0).astype(BF16)
        suffix_all = jnp.dot(sp_all, neg_later, preferred_element_type=F32)
        for h in range(heads):
            sl = slice(h * HEAD_DIM, (h + 1) * HEAD_DIM)
            v_t = v_ref[pl.ds(start, tk), sl]
            suffix = suffix_all[h * tq:(h + 1) * tq]
            expo = log_betas[h] + suffix
            if not first:
                expo = expo + jnp.concatenate([tail_ref[s, h]] * (tk // HEAD_DIM), axis=1)
            w = jnp.exp2(expo)
            if first:
                w = jnp.where(col < row, w, 0.0)
            pv = jnp.dot(w.astype(BF16), v_t, preferred_element_type=F32)
            total = jnp.broadcast_to(suffix[:, 0:1] - softplus[h][:, 0:1], (tq, HEAD_DIM))
            if first:
                o_ref[rows, sl] = pv
                tail_ref[s, h] = total
            else:
                o_ref[rows, sl] += pv
                tail_ref[s, h] += total

    def stick_left(s):
        return jnp.max(tail_ref[s]) > F32_ZERO_LOG2

    @pl.when(step == 0)
    def _():
        cast_weights()
        for s in range(subs):
            tile(s, s, True)
            if s > 0:
                tile(s, s - 1, False)
            live_ref[s] = stick_left(s).astype(jnp.int32)

    @pl.when(step > 0)
    def _():
        cast_weights()
        for s in range(subs):
            tile(s, step * subs + s, True)
            tile(s, step * subs + s - 1, False)
            live_ref[s] = stick_left(s).astype(jnp.int32)

    for s in range(subs):
        qb = step * subs + s

        def body(carry, s=s, qb=qb):
            it, _ = carry
            tile(s, qb - 1 - it, False)
            return it + 1, stick_left(s)

        lax.while_loop(lambda c, qb=qb: jnp.logical_and(c[0] < qb, c[1]), body,
                       (jnp.int32(1), live_ref[s] > 0))


def _attention(qkv, batch, seq, n_heads, later_weights, tq=256, subs=2, heads=8):
    m = qkv.shape[0]
    rows = tq * subs
    nq = seq // rows
    ng = n_heads // heads
    width = heads * HEAD_DIM
    n_steps = batch * ng * nq
    assert seq % rows == 0 and n_heads % heads == 0
    assert all(w.shape[0] % (16 * n_steps) == 0 for w in later_weights)
    step = lambda b, g, i: ((b * ng + g) * nq + i, 0)
    slab_specs = [pl.BlockSpec((w.shape[0] // n_steps, w.shape[1]), step) for w in later_weights]
    outs = pl.pallas_call(
        functools.partial(_attn_kernel, tq=tq, subs=subs, heads=heads,
                          n_cast=len(later_weights), k_col=ng * width, v_col=2 * ng * width),
        grid=(batch, ng, nq),
        in_specs=[
            pl.BlockSpec((rows, width), lambda b, g, i: (b * nq + i, g)),
            pl.BlockSpec(memory_space=pl.ANY),
            *slab_specs,
        ],
        out_specs=[pl.BlockSpec((rows, width), lambda b, g, i: (b * nq + i, g)), *slab_specs],
        out_shape=[jax.ShapeDtypeStruct((m, n_heads * HEAD_DIM), F32),
                   *[jax.ShapeDtypeStruct(w.shape, BF16) for w in later_weights]],
        scratch_shapes=[pltpu.VMEM((subs, heads, tq, HEAD_DIM), F32),
                        pltpu.SMEM((subs,), jnp.int32),
                        pltpu.VMEM((seq, width), BF16),
                        pltpu.VMEM((seq, width), BF16),
                        pltpu.SemaphoreType.DMA((2,))],
        compiler_params=_params(3),
        name="stickbreak_attn",
    )(qkv, qkv, *later_weights)
    return outs[0], outs[1:]


def _lru_kernel(x_ref, gate_ref, cw_ref, cb_ref, wr_ref, wi_ref, br_ref, bi_ref, lam_ref,
                gn_ref, o_ref, xtail, stage, h_buf, hcarry, *, tt):
    t = pl.program_id(1)
    c = x_ref.shape[1]
    seg = tt // SUBLANES
    pitch = stage.shape[1] // SUBLANES

    @pl.when(t == 0)
    def _():
        xtail[...] = jnp.zeros((SUBLANES, c), F32)
        hcarry[...] = jnp.zeros((SUBLANES, c), F32)

    first_sublane = lax.broadcasted_iota(jnp.int32, (SUBLANES, LRU_BLOCK), 0) == 0
    log_sig_lam = _log_sigmoid(lam_ref[...])
    log2_a_coef = (0.5 * LRU_C * LOG2_E) * log_sig_lam
    neg_log_a_coef = (-0.5 * LRU_C) * log_sig_lam
    for n in range(c // LRU_BLOCK):
        sl = slice(n * LRU_BLOCK, (n + 1) * LRU_BLOCK)
        for k in range(SUBLANES):
            stage[n, k * pitch:k * pitch + seg, :] = x_ref[k * seg:(k + 1) * seg, sl]
        xs = [stage[n, pl.ds(j, SUBLANES, stride=pitch), :] for j in range(seg)]
        before = [jnp.where(first_sublane, xtail[SUBLANES - back:SUBLANES - back + 1, sl],
                            pltpu.roll(xs[seg - back], 1, 0))
                  for back in range(CONV_WIDTH - 1, 0, -1)]
        xs = before + xs
        taps = [cw_ref[k:k + 1, sl] for k in range(CONV_WIDTH)]
        bias = cb_ref[:, sl]
        conv = []
        for j in range(seg):
            acc = bias + taps[0] * xs[j]
            for k in range(1, CONV_WIDTH):
                acc = acc + taps[k] * xs[j + k]
            conv.append(acc)
        xc = jnp.concatenate(conv, axis=0)
        xcb = xc.astype(BF16)
        half_wr = (0.5 * wr_ref[n]).astype(BF16)
        half_wi = (0.5 * wi_ref[n]).astype(BF16)
        tanh_r = jnp.tanh(jnp.dot(xcb, half_wr, preferred_element_type=F32)
                          + 0.5 * br_ref[:, sl])
        tanh_i = jnp.tanh(jnp.dot(xcb, half_wi, preferred_element_type=F32)
                          + 0.5 * bi_ref[:, sl])
        two_r = tanh_r + 1.0
        a = jnp.exp2(two_r * log2_a_coef[:, sl])
        one_m_a2 = jnp.tanh(two_r * neg_log_a_coef[:, sl]) * (1.0 + a * a)
        mult = one_m_a2 * lax.rsqrt(jnp.maximum(one_m_a2, F32_TINY))
        b = mult * ((tanh_i + 1.0) * (0.5 * xc))

        a = a.reshape(seg, SUBLANES, LRU_BLOCK)
        b = b.reshape(seg, SUBLANES, LRU_BLOCK)
        hs, ps = [b[0]], [a[0]]
        for j in range(1, seg):
            hs.append(a[j] * hs[-1] + b[j])
            ps.append(a[j] * ps[-1])
        entering = [hcarry[0:1, sl]]
        for k in range(SUBLANES):
            entering.append(hs[-1][k:k + 1, :] + ps[-1][k:k + 1, :] * entering[-1])
        hcarry[:, sl] = jnp.broadcast_to(entering[-1], (SUBLANES, LRU_BLOCK))
        enter = jnp.concatenate(entering[:-1], axis=0)
        for j in range(seg):
            stage[n, pl.ds(j, SUBLANES, stride=pitch), :] = hs[j] + ps[j] * enter
        for k in range(SUBLANES):
            h_buf[k * seg:(k + 1) * seg, sl] = stage[n, k * pitch:k * pitch + seg, :]

    xtail[...] = x_ref[tt - SUBLANES:tt, :]

    g = gate_ref[...]
    inner = g * (GELU_C1 + GELU_C3 * (g * g))
    y = (h_buf[...] * (0.5 * g)) * (1.0 + jnp.tanh(inner))
    o_ref[...] = _rms(y, gn_ref[...]).astype(o_ref.dtype)


def _lru_branch(lru_in, conv_w, conv_b, w_r, w_i, b_r, b_i, lam, gain, batch, seq, tt=512):
    m = lru_in.shape[0]
    c = lru_in.shape[1] // 2
    nt = seq // tt
    n_blocks = c // LRU_BLOCK
    assert seq % tt == 0 and tt % (SUBLANES * SUBLANES) == 0 and c % LRU_BLOCK == 0
    row = lambda b, t: (0, 0)
    return pl.pallas_call(
        functools.partial(_lru_kernel, tt=tt),
        grid=(batch, nt),
        in_specs=[
            pl.BlockSpec((tt, c), lambda b, t: (b * nt + t, 0)),
            pl.BlockSpec((tt, c), lambda b, t: (b * nt + t, 1)),
            pl.BlockSpec((CONV_WIDTH, c), row),
            pl.BlockSpec((1, c), row),
            pl.BlockSpec((n_blocks, LRU_BLOCK, LRU_BLOCK), lambda b, t: (0, 0, 0)),
            pl.BlockSpec((n_blocks, LRU_BLOCK, LRU_BLOCK), lambda b, t: (0, 0, 0)),
            pl.BlockSpec((1, c), row),
            pl.BlockSpec((1, c), row),
            pl.BlockSpec((1, c), row),
            pl.BlockSpec((1, c), row),
        ],
        out_specs=pl.BlockSpec((tt, c), lambda b, t: (b * nt + t, 0)),
        out_shape=jax.ShapeDtypeStruct((m, c), BF16),
        scratch_shapes=[
            pltpu.VMEM((SUBLANES, c), F32),
            pltpu.VMEM((n_blocks, SUBLANES * (tt // SUBLANES + LRU_SEG_PAD), LRU_BLOCK), F32),
            pltpu.VMEM((tt, c), F32),
            pltpu.VMEM((SUBLANES, c), F32),
        ],
        compiler_params=_params(2),
        name="rglru",
    )(lru_in, lru_in, conv_w, conv_b, w_r, w_i, b_r, b_i, lam, gain)


def _outproj_kernel(ya_ref, yl_ref, ga_ref, w_ref, x_ref, gpost_ref, x1_ref):
    ya = _rms(ya_ref[...], ga_ref[...]).astype(BF16)
    y = jnp.concatenate([ya, yl_ref[...]], axis=1)
    mix = jnp.dot(y, w_ref[...], preferred_element_type=F32)
    x1_ref[...] = x_ref[...] + _rms(mix, gpost_ref[...])


def _outproj(y_attn, y_lru, attn_gain, w_bf, x2, post_gain, tm=512):
    m, d = x2.shape
    ca = y_attn.shape[1]
    cl = y_lru.shape[1]
    assert m % tm == 0
    row = lambda i: (0, 0)
    return pl.pallas_call(
        _outproj_kernel,
        grid=(m // tm,),
        in_specs=[
            pl.BlockSpec((tm, ca), lambda i: (i, 0)),
            pl.BlockSpec((tm, cl), lambda i: (i, 0)),
            pl.BlockSpec((1, ca), row),
            pl.BlockSpec((ca + cl, d), row, pipeline_mode=pl.Buffered(1)),
            pl.BlockSpec((tm, d), lambda i: (i, 0)),
            pl.BlockSpec((1, d), row),
        ],
        out_specs=pl.BlockSpec((tm, d), lambda i: (i, 0)),
        out_shape=jax.ShapeDtypeStruct((m, d), F32),
        compiler_params=_params(1),
        name="outproj",
    )(y_attn, y_lru, attn_gain, w_bf, x2, post_gain)


def _ffn_kernel(x1_ref, gpre_ref, wg_ref, wu_ref, wd_ref, gpost_ref, o_ref, hn_ref):
    f = pl.program_id(1)
    last = pl.num_programs(1) - 1
    half = o_ref.shape[0] // 2
    halves = [slice(0, half), slice(half, 2 * half)]

    def chunks(rows):
        return [slice(r0, r0 + NORM_ROWS) for r0 in range(rows.start, rows.stop, NORM_ROWS)]

    def pre_norm(rows):
        for r in chunks(rows):
            hn_ref[r, :] = _rms(x1_ref[r, :], gpre_ref[...]).astype(BF16)

    def post_norm_residual(rows):
        for r in chunks(rows):
            o_ref[r, :] = x1_ref[r, :] + _rms(o_ref[r, :], gpost_ref[...])

    def ffn(rows, first):
        hn = hn_ref[rows, :]
        for c0 in range(0, wg_ref.shape[1], FFN_SLAB):
            cols = slice(c0, c0 + FFN_SLAB)
            gate = jnp.dot(hn, wg_ref[:, cols], preferred_element_type=F32)
            up = jnp.dot(hn, wu_ref[:, cols], preferred_element_type=F32)
            act = (jax.nn.silu(gate) * up).astype(BF16)
            part = jnp.dot(act, wd_ref[cols, :].astype(BF16), preferred_element_type=F32)
            if first and c0 == 0:
                o_ref[rows, :] = part
            else:
                o_ref[rows, :] += part

    @pl.when(f == 0)
    def _():
        for rows in halves:
            pre_norm(rows)
            ffn(rows, True)

    @pl.when(jnp.logical_and(f > 0, f < last))
    def _():
        for rows in halves:
            ffn(rows, False)

    @pl.when(f == last)
    def _():
        for rows in halves:
            ffn(rows, False)
            post_norm_residual(rows)


def _ffn(x1, pre_gain, wg, wu, wd, post_gain, tm=1024, tf=512):
    m, d = x1.shape
    dff = wg.shape[1]
    assert m % tm == 0 and dff % tf == 0 and dff // tf >= 2 and tf % FFN_SLAB == 0
    assert tm % (2 * NORM_ROWS) == 0
    return pl.pallas_call(
        _ffn_kernel,
        grid=(m // tm, dff // tf),
        in_specs=[
            pl.BlockSpec((tm, d), lambda i, f: (i, 0)),
            pl.BlockSpec((1, d), lambda i, f: (0, 0)),
            pl.BlockSpec((d, tf), lambda i, f: (0, f)),
            pl.BlockSpec((d, tf), lambda i, f: (0, f)),
            pl.BlockSpec((tf, d), lambda i, f: (f, 0)),
            pl.BlockSpec((1, d), lambda i, f: (0, 0)),
        ],
        out_specs=pl.BlockSpec((tm, d), lambda i, f: (i, 0)),
        out_shape=jax.ShapeDtypeStruct((m, d), F32),
        scratch_shapes=[pltpu.VMEM((tm, d), BF16)],
        compiler_params=_params(2),
        name="swiglu_ffn",
    )(x1, pre_gain, wg, wu, wd, post_gain)


def _layer(x2, batch, seq, pre_mix, post_mix, pre_ffn, post_ffn, w_in, conv_w, conv_b, w_r, b_r,
           w_i, b_i, lam, attn_gain, lru_gain, w_out, w_g, w_u, w_d):
    row = lambda v: v.reshape(1, -1)
    lru_width = conv_w.shape[1]
    qkv_width = w_in.shape[1] - 2 * lru_width
    n_heads = qkv_width // (3 * HEAD_DIM)

    qkv, lru_in = _inproj(x2, row(pre_mix), w_in, qkv_width)
    y_attn, (w_out_bf, w_g_bf, w_u_bf) = _attention(qkv, batch, seq, n_heads, (w_out, w_g, w_u))
    y_lru = _lru_branch(lru_in, conv_w, row(conv_b), w_r, w_i, row(b_r), row(b_i), row(lam),
                        row(lru_gain), batch, seq)
    x1 = _outproj(y_attn, y_lru, row(attn_gain), w_out_bf, x2, row(post_mix))
    return _ffn(x1, row(pre_ffn), w_g_bf, w_u_bf, w_d, row(post_ffn))


def kernel(x, pre_mix_norm, post_mix_norm, pre_ffn_norm, post_ffn_norm, w_in, conv_w, conv_b,
           w_rgate, b_rgate, w_igate, b_igate, lru_lambda, attn_out_norm, lru_out_norm, w_out,
           w_ffn_gate, w_ffn_up, w_ffn_down):
    batch, seq, d = x.shape
    x2 = x.reshape(batch * seq, d)
    for l in range(w_in.shape[0]):
        x2 = _layer(x2, batch, seq, pre_mix_norm[l], post_mix_norm[l], pre_ffn_norm[l],
                    post_ffn_norm[l], w_in[l], conv_w[l], conv_b[l], w_rgate[l], b_rgate[l],
                    w_igate[l], b_igate[l], lru_lambda[l], attn_out_norm[l], lru_out_norm[l],
                    w_out[l], w_ffn_gate[l], w_ffn_up[l], w_ffn_down[l])
    return x2.reshape(batch, seq, d)
```

```python
import functools
import math

import jax
import jax.numpy as jnp
from jax import lax
from jax.experimental import pallas as pl
from jax.experimental.pallas import tpu as pltpu

F32 = jnp.float32
BF16 = jnp.bfloat16

HEAD_DIM = 128
LRU_BLOCK = 128
LRU_SEG_PAD = 4
CONV_WIDTH = 4
LRU_C = 8.0
RMS_EPS = 1e-6
LOG2_E = math.log2(math.e)
Q_SCALE = LOG2_E / math.sqrt(HEAD_DIM)
F32_TINY = float(jnp.finfo(jnp.float32).tiny)
GELU_C1 = math.sqrt(2.0 / math.pi)
GELU_C3 = GELU_C1 * 0.044715
F32_ZERO_LOG2 = -150.0
SUBLANES = 8
NORM_ROWS = 128
FFN_SLAB = 256
VMEM_LIMIT = 60 * 1024 * 1024


def _params(n_axes):
    return pltpu.CompilerParams(dimension_semantics=("arbitrary",) * n_axes,
                                vmem_limit_bytes=VMEM_LIMIT)


def _rms(x, gain, eps=RMS_EPS):
    var = jnp.mean(x * x, axis=-1, keepdims=True)
    return (x * lax.rsqrt(var + eps)) * gain


def _log_sigmoid(z):
    return jnp.minimum(z, 0.0) - jnp.log1p(jnp.exp(-jnp.abs(z)))


def _inproj_kernel(x_hbm, g_ref, w_ref, qkv_ref, lru_ref, hn_ref, xbuf, sem, *, n_q_tiles):
    i, j = pl.program_id(0), pl.program_id(1)
    tm = xbuf.shape[0]

    def fetch(tile):
        return pltpu.make_async_copy(x_hbm.at[pl.ds(tile * tm, tm), :], xbuf, sem)

    @pl.when(jnp.logical_and(i == 0, j == 0))
    def _():
        fetch(0).start()

    def project(rows):
        acc = jnp.dot(hn_ref[rows, :], w_ref[...].astype(BF16), preferred_element_type=F32)
        qkv_ref[rows, :] = (acc * jnp.where(j < n_q_tiles, Q_SCALE, 1.0)).astype(BF16)
        lru_ref[rows, :] = acc

    @pl.when(j == 0)
    def _():
        fetch(i).wait()
        for h0 in range(0, tm, tm // 2):
            for r0 in range(h0, h0 + tm // 2, NORM_ROWS):
                rows = slice(r0, r0 + NORM_ROWS)
                hn_ref[rows, :] = _rms(xbuf[rows, :], g_ref[...]).astype(BF16)
            project(slice(h0, h0 + tm // 2))

    @pl.when(jnp.logical_and(j == 1, i + 1 < pl.num_programs(0)))
    def _():
        fetch(i + 1).start()

    @pl.when(j > 0)
    def _():
        project(slice(0, tm))


def _inproj(x2, gain, w, qkv_width, tm=2048, tn=512):
    m, d = x2.shape
    n = w.shape[1]
    lru_width = n - qkv_width
    assert m % tm == 0 and n % tn == 0 and qkv_width % (3 * tn) == 0 and tm % (2 * NORM_ROWS) == 0
    n_qkv_tiles = qkv_width // tn
    n_q_tiles = qkv_width // (3 * tn)
    grid = (m // tm, n // tn)
    return pl.pallas_call(
        functools.partial(_inproj_kernel, n_q_tiles=n_q_tiles),
        grid=grid,
        in_specs=[
            pl.BlockSpec(memory_space=pl.ANY),
            pl.BlockSpec((1, d), lambda i, j: (0, 0)),
            pl.BlockSpec((d, tn), lambda i, j: (0, j)),
        ],
        out_specs=[
            pl.BlockSpec((tm, tn), lambda i, j: (i, jnp.minimum(j, n_qkv_tiles))),
            pl.BlockSpec((tm, tn), lambda i, j: (i, jnp.maximum(j - n_qkv_tiles, 0))),
        ],
        out_shape=[
            jax.ShapeDtypeStruct((m, qkv_width + tn), BF16),
            jax.ShapeDtypeStruct((m, lru_width), F32),
        ],
        scratch_shapes=[pltpu.VMEM((tm, d), BF16), pltpu.VMEM((tm, d), F32),
                        pltpu.SemaphoreType.DMA(())],
        compiler_params=_params(2),
        name="inproj",
    )(x2, gain, w)


def _attn_kernel(q_ref, qkv_hbm, *refs, tq, subs, heads, n_cast, k_col, v_col):
    cast_in, o_ref = refs[:n_cast], refs[n_cast]
    cast_out = refs[n_cast + 1:2 * n_cast + 1]
    tail_ref, live_ref, k_ref, v_ref, kv_sems = refs[2 * n_cast + 1:]

    def cast_weights():
        for src, dst in zip(cast_in, cast_out):
            dst[...] = src[...].astype(BF16)

    tk = tq
    step = pl.program_id(2)
    rows_per_step = subs * tq
    seq = k_ref.shape[0]
    width = k_ref.shape[1]

    def kv_fetch(chunk):
        src_rows = pl.ds(pl.program_id(0) * seq + chunk * rows_per_step, rows_per_step)
        dst_rows = pl.ds(chunk * rows_per_step, rows_per_step)
        group = pl.program_id(1)
        return [pltpu.make_async_copy(
                    qkv_hbm.at[src_rows, pl.ds(col0 + group * width, width)],
                    buf.at[dst_rows, :], kv_sems.at[which])
                for which, (col0, buf) in enumerate(((k_col, k_ref), (v_col, v_ref)))]

    @pl.when(step == 0)
    def _():
        for copy in kv_fetch(0):
            copy.start()

    for copy in kv_fetch(step):
        copy.wait()

    @pl.when(step + 1 < pl.num_programs(2))
    def _():
        for copy in kv_fetch(step + 1):
            copy.start()

    row = lax.broadcasted_iota(jnp.int32, (tk, tk), 0)
    col = lax.broadcasted_iota(jnp.int32, (tk, tk), 1)
    neg_later = jnp.where(row > col, -1.0, 0.0).astype(BF16)

    def tile(s, kj, first):
        rows = slice(s * tq, (s + 1) * tq)
        start = pl.multiple_of(kj * tk, tk)
        log_betas = []
        softplus = []
        for h in range(heads):
            sl = slice(h * HEAD_DIM, (h + 1) * HEAD_DIM)
            k_t = k_ref[pl.ds(start, tk), sl]
            z2 = lax.dot_general(q_ref[rows, sl], k_t, (((1,), (1,)), ((), ())),
                                 preferred_element_type=F32)
            pos = jnp.maximum(z2, 0.0)
            neg = z2 - pos
            log_term = jnp.log(1.0 + jnp.exp2(neg - pos)) * LOG2_E
            sp = pos + log_term
            if first:
                sp = jnp.where(col < row, sp, 0.0)
            softplus.append(sp)
            log_betas.append(neg - log_term)
        sp_all = jnp.concatenate(softplus, axis=0).astype(BF16)
        suffix_all = jnp.dot(sp_all, neg_later, preferred_element_type=F32)
        for h in range(heads):
            sl = slice(h * HEAD_DIM, (h + 1) * HEAD_DIM)
            v_t = v_ref[pl.ds(start, tk), sl]
            suffix = suffix_all[h * tq:(h + 1) * tq]
            expo = log_betas[h] + suffix
            if not first:
                expo = expo + jnp.concatenate([tail_ref[s, h]] * (tk // HEAD_DIM), axis=1)
            w = jnp.exp2(expo)
            if first:
                w = jnp.where(col < row, w, 0.0)
            pv = jnp.dot(w.astype(BF16), v_t, preferred_element_type=F32)
            total = jnp.broadcast_to(suffix[:, 0:1] - softplus[h][:, 0:1], (tq, HEAD_DIM))
            if first:
                o_ref[rows, sl] = pv
                tail_ref[s, h] = total
            else:
                o_ref[rows, sl] += pv
                tail_ref[s, h] += total

    def stick_left(s):
        return jnp.max(tail_ref[s]) > F32_ZERO_LOG2

    @pl.when(step == 0)
    def _():
        cast_weights()
        for s in range(subs):
            tile(s, s, True)
            if s > 0:
                tile(s, s - 1, False)
            live_ref[s] = stick_left(s).astype(jnp.int32)

    @pl.when(step > 0)
    def _():
        cast_weights()
        for s in range(subs):
            tile(s, step * subs + s, True)
            tile(s, step * subs + s - 1, False)
            live_ref[s] = stick_left(s).astype(jnp.int32)

    for s in range(subs):
        qb = step * subs + s

        def body(carry, s=s, qb=qb):
            it, _ = carry
            tile(s, qb - 1 - it, False)
            return it + 1, stick_left(s)

        lax.while_loop(lambda c, qb=qb: jnp.logical_and(c[0] < qb, c[1]), body,
                       (jnp.int32(1), live_ref[s] > 0))


def _attention(qkv, batch, seq, n_heads, later_weights, tq=256, subs=2, heads=8):
    m = qkv.shape[0]
    rows = tq * subs
    nq = seq // rows
    ng = n_heads // heads
    width = heads * HEAD_DIM
    n_steps = batch * ng * nq
    assert seq % rows == 0 and n_heads % heads == 0
    assert all(w.shape[0] % (16 * n_steps) == 0 for w in later_weights)
    step = lambda b, g, i: ((b * ng + g) * nq + i, 0)
    slab_specs = [pl.BlockSpec((w.shape[0] // n_steps, w.shape[1]), step) for w in later_weights]
    outs = pl.pallas_call(
        functools.partial(_attn_kernel, tq=tq, subs=subs, heads=heads,
                          n_cast=len(later_weights), k_col=ng * width, v_col=2 * ng * width),
        grid=(batch, ng, nq),
        in_specs=[
            pl.BlockSpec((rows, width), lambda b, g, i: (b * nq + i, g)),
            pl.BlockSpec(memory_space=pl.ANY),
            *slab_specs,
        ],
        out_specs=[pl.BlockSpec((rows, width), lambda b, g, i: (b * nq + i, g)), *slab_specs],
        out_shape=[jax.ShapeDtypeStruct((m, n_heads * HEAD_DIM), F32),
                   *[jax.ShapeDtypeStruct(w.shape, BF16) for w in later_weights]],
        scratch_shapes=[pltpu.VMEM((subs, heads, tq, HEAD_DIM), F32),
                        pltpu.SMEM((subs,), jnp.int32),
                        pltpu.VMEM((seq, width), BF16),
                        pltpu.VMEM((seq, width), BF16),
                        pltpu.SemaphoreType.DMA((2,))],
        compiler_params=_params(3),
        name="stickbreak_attn",
    )(qkv, qkv, *later_weights)
    return outs[0], outs[1:]


def _lru_kernel(x_ref, gate_ref, cw_ref, cb_ref, wr_ref, wi_ref, br_ref, bi_ref, lam_ref,
                gn_ref, o_ref, xtail, stage, h_buf, hcarry, *, tt):
    t = pl.program_id(1)
    c = x_ref.shape[1]
    seg = tt // SUBLANES
    pitch = stage.shape[1] // SUBLANES

    @pl.when(t == 0)
    def _():
        xtail[...] = jnp.zeros((SUBLANES, c), F32)
        hcarry[...] = jnp.zeros((SUBLANES, c), F32)

    first_sublane = lax.broadcasted_iota(jnp.int32, (SUBLANES, LRU_BLOCK), 0) == 0
    log_sig_lam = _log_sigmoid(lam_ref[...])
    log2_a_coef = (0.5 * LRU_C * LOG2_E) * log_sig_lam
    neg_log_a_coef = (-0.5 * LRU_C) * log_sig_lam
    for n in range(c // LRU_BLOCK):
        sl = slice(n * LRU_BLOCK, (n + 1) * LRU_BLOCK)
        for k in range(SUBLANES):
            stage[n, k * pitch:k * pitch + seg, :] = x_ref[k * seg:(k + 1) * seg, sl]
        xs = [stage[n, pl.ds(j, SUBLANES, stride=pitch), :] for j in range(seg)]
        before = [jnp.where(first_sublane, xtail[SUBLANES - back:SUBLANES - back + 1, sl],
                            pltpu.roll(xs[seg - back], 1, 0))
                  for back in range(CONV_WIDTH - 1, 0, -1)]
        xs = before + xs
        taps = [cw_ref[k:k + 1, sl] for k in range(CONV_WIDTH)]
        bias = cb_ref[:, sl]
        conv = []
        for j in range(seg):
            acc = bias + taps[0] * xs[j]
            for k in range(1, CONV_WIDTH):
                acc = acc + taps[k] * xs[j + k]
            conv.append(acc)
        xc = jnp.concatenate(conv, axis=0)
        xcb = xc.astype(BF16)
        half_wr = (0.5 * wr_ref[n]).astype(BF16)
        half_wi = (0.5 * wi_ref[n]).astype(BF16)
        tanh_r = jnp.tanh(jnp.dot(xcb, half_wr, preferred_element_type=F32)
                          + 0.5 * br_ref[:, sl])
        tanh_i = jnp.tanh(jnp.dot(xcb, half_wi, preferred_element_type=F32)
                          + 0.5 * bi_ref[:, sl])
        two_r = tanh_r + 1.0
        a = jnp.exp2(two_r * log2_a_coef[:, sl])
        one_m_a2 = jnp.tanh(two_r * neg_log_a_coef[:, sl]) * (1.0 + a * a)
        mult = one_m_a2 * lax.rsqrt(jnp.maximum(one_m_a2, F32_TINY))
        b = mult * ((tanh_i + 1.0) * xc)

        a = a.reshape(seg, SUBLANES, LRU_BLOCK)
        b = b.reshape(seg, SUBLANES, LRU_BLOCK)
        hs, ps = [b[0]], [a[0]]
        for j in range(1, seg):
            hs.append(a[j] * hs[-1] + b[j])
            ps.append(a[j] * ps[-1])
        entering = [hcarry[0:1, sl]]
        for k in range(SUBLANES):
            entering.append(hs[-1][k:k + 1, :] + ps[-1][k:k + 1, :] * entering[-1])
        hcarry[:, sl] = jnp.broadcast_to(entering[-1], (SUBLANES, LRU_BLOCK))
        enter = jnp.concatenate(entering[:-1], axis=0)
        for j in range(seg):
            stage[n, pl.ds(j, SUBLANES, stride=pitch), :] = hs[j] + ps[j] * enter
        for k in range(SUBLANES):
            h_buf[k * seg:(k + 1) * seg, sl] = stage[n, k * pitch:k * pitch + seg, :]

    xtail[...] = x_ref[tt - SUBLANES:tt, :]

    g = gate_ref[...]
    inner = g * (GELU_C1 + GELU_C3 * (g * g))
    y4 = (h_buf[...] * g) * (1.0 + jnp.tanh(inner))
    o_ref[...] = _rms(y4, gn_ref[...], eps=16.0 * RMS_EPS).astype(o_ref.dtype)


def _lru_branch(lru_in, conv_w, conv_b, w_r, w_i, b_r, b_i, lam, gain, batch, seq, tt=1024):
    m = lru_in.shape[0]
    c = lru_in.shape[1] // 2
    nt = seq // tt
    n_blocks = c // LRU_BLOCK
    assert seq % tt == 0 and tt % (SUBLANES * SUBLANES) == 0 and c % LRU_BLOCK == 0
    row = lambda b, t: (0, 0)
    return pl.pallas_call(
        functools.partial(_lru_kernel, tt=tt),
        grid=(batch, nt),
        in_specs=[
            pl.BlockSpec((tt, c), lambda b, t: (b * nt + t, 0)),
            pl.BlockSpec((tt, c), lambda b, t: (b * nt + t, 1)),
            pl.BlockSpec((CONV_WIDTH, c), row),
            pl.BlockSpec((1, c), row),
            pl.BlockSpec((n_blocks, LRU_BLOCK, LRU_BLOCK), lambda b, t: (0, 0, 0)),
            pl.BlockSpec((n_blocks, LRU_BLOCK, LRU_BLOCK), lambda b, t: (0, 0, 0)),
            pl.BlockSpec((1, c), row),
            pl.BlockSpec((1, c), row),
            pl.BlockSpec((1, c), row),
            pl.BlockSpec((1, c), row),
        ],
        out_specs=pl.BlockSpec((tt, c), lambda b, t: (b * nt + t, 0)),
        out_shape=jax.ShapeDtypeStruct((m, c), BF16),
        scratch_shapes=[
            pltpu.VMEM((SUBLANES, c), F32),
            pltpu.VMEM((n_blocks, SUBLANES * (tt // SUBLANES + LRU_SEG_PAD), LRU_BLOCK), F32),
            pltpu.VMEM((tt, c), F32),
            pltpu.VMEM((SUBLANES, c), F32),
        ],
        compiler_params=_params(2),
        name="rglru",
    )(lru_in, lru_in, conv_w, conv_b, w_r, w_i, b_r, b_i, lam, gain)


def _outproj_kernel(ya_ref, yl_ref, ga_ref, w_ref, x_ref, gpost_ref, x1_ref):
    ya = _rms(ya_ref[...], ga_ref[...]).astype(BF16)
    y = jnp.concatenate([ya, yl_ref[...]], axis=1)
    mix = jnp.dot(y, w_ref[...], preferred_element_type=F32)
    x1_ref[...] = x_ref[...] + _rms(mix, gpost_ref[...])


def _outproj(y_attn, y_lru, attn_gain, w_bf, x2, post_gain, tm=512):
    m, d = x2.shape
    ca = y_attn.shape[1]
    cl = y_lru.shape[1]
    assert m % tm == 0
    row = lambda i: (0, 0)
    return pl.pallas_call(
        _outproj_kernel,
        grid=(m // tm,),
        in_specs=[
            pl.BlockSpec((tm, ca), lambda i: (i, 0)),
            pl.BlockSpec((tm, cl), lambda i: (i, 0)),
            pl.BlockSpec((1, ca), row),
            pl.BlockSpec((ca + cl, d), row, pipeline_mode=pl.Buffered(1)),
            pl.BlockSpec((tm, d), lambda i: (i, 0)),
            pl.BlockSpec((1, d), row),
        ],
        out_specs=pl.BlockSpec((tm, d), lambda i: (i, 0)),
        out_shape=jax.ShapeDtypeStruct((m, d), F32),
        compiler_params=_params(1),
        name="outproj",
    )(y_attn, y_lru, attn_gain, w_bf, x2, post_gain)


def _ffn_kernel(x1_ref, gpre_ref, wg_ref, wu_ref, wd_ref, gpost_ref, o_ref, hn_ref):
    f = pl.program_id(1)
    last = pl.num_programs(1) - 1
    half = o_ref.shape[0] // 2
    halves = [slice(0, half), slice(half, 2 * half)]

    def chunks(rows):
        return [slice(r0, r0 + NORM_ROWS) for r0 in range(rows.start, rows.stop, NORM_ROWS)]

    def pre_norm(rows):
        for r in chunks(rows):
            hn_ref[r, :] = _rms(x1_ref[r, :], gpre_ref[...]).astype(BF16)

    def post_norm_residual(rows):
        for r in chunks(rows):
            o_ref[r, :] = x1_ref[r, :] + _rms(o_ref[r, :], gpost_ref[...])

    def ffn(rows, first):
        hn = hn_ref[rows, :]
        for c0 in range(0, wg_ref.shape[1], FFN_SLAB):
            cols = slice(c0, c0 + FFN_SLAB)
            gate = jnp.dot(hn, wg_ref[:, cols], preferred_element_type=F32)
            up = jnp.dot(hn, wu_ref[:, cols], preferred_element_type=F32)
            act = (jax.nn.silu(gate) * up).astype(BF16)
            part = jnp.dot(act, wd_ref[cols, :].astype(BF16), preferred_element_type=F32)
            if first and c0 == 0:
                o_ref[rows, :] = part
            else:
                o_ref[rows, :] += part

    @pl.when(f == 0)
    def _():
        for rows in halves:
            pre_norm(rows)
            ffn(rows, True)

    @pl.when(jnp.logical_and(f > 0, f < last))
    def _():
        for rows in halves:
            ffn(rows, False)

    @pl.when(f == last)
    def _():
        for rows in halves:
            ffn(rows, False)
            post_norm_residual(rows)


def _ffn(x1, pre_gain, wg, wu, wd, post_gain, tm=1024, tf=512):
    m, d = x1.shape
    dff = wg.shape[1]
    assert m % tm == 0 and dff % tf == 0 and dff // tf >= 2 and tf % FFN_SLAB == 0
    assert tm % (2 * NORM_ROWS) == 0
    return pl.pallas_call(
        _ffn_kernel,
        grid=(m // tm, dff // tf),
        in_specs=[
            pl.BlockSpec((tm, d), lambda i, f: (i, 0)),
            pl.BlockSpec((1, d), lambda i, f: (0, 0)),
            pl.BlockSpec((d, tf), lambda i, f: (0, f)),
            pl.BlockSpec((d, tf), lambda i, f: (0, f)),
            pl.BlockSpec((tf, d), lambda i, f: (f, 0)),
            pl.BlockSpec((1, d), lambda i, f: (0, 0)),
        ],
        out_specs=pl.BlockSpec((tm, d), lambda i, f: (i, 0)),
        out_shape=jax.ShapeDtypeStruct((m, d), F32),
        scratch_shapes=[pltpu.VMEM((tm, d), BF16)],
        compiler_params=_params(2),
        name="swiglu_ffn",
    )(x1, pre_gain, wg, wu, wd, post_gain)


def _layer(x2, batch, seq, pre_mix, post_mix, pre_ffn, post_ffn, w_in, conv_w, conv_b, w_r, b_r,
           w_i, b_i, lam, attn_gain, lru_gain, w_out, w_g, w_u, w_d):
    row = lambda v: v.reshape(1, -1)
    lru_width = conv_w.shape[1]
    qkv_width = w_in.shape[1] - 2 * lru_width
    n_heads = qkv_width // (3 * HEAD_DIM)

    qkv, lru_in = _inproj(x2, row(pre_mix), w_in, qkv_width)
    y_attn, (w_out_bf, w_g_bf, w_u_bf) = _attention(qkv, batch, seq, n_heads, (w_out, w_g, w_u))
    y_lru = _lru_branch(lru_in, conv_w, row(conv_b), w_r, w_i, row(b_r), row(b_i), row(lam),
                        row(lru_gain), batch, seq)
    x1 = _outproj(y_attn, y_lru, row(attn_gain), w_out_bf, x2, row(post_mix))
    return _ffn(x1, row(pre_ffn), w_g_bf, w_u_bf, w_d, row(post_ffn))


def kernel(x, pre_mix_norm, post_mix_norm, pre_ffn_norm, post_ffn_norm, w_in, conv_w, conv_b,
           w_rgate, b_rgate, w_igate, b_igate, lru_lambda, attn_out_norm, lru_out_norm, w_out,
           w_ffn_gate, w_ffn_up, w_ffn_down):
    batch, seq, d = x.shape
    x2 = x.reshape(batch * seq, d)
    for l in range(w_in.shape[0]):
        x2 = _layer(x2, batch, seq, pre_mix_norm[l], post_mix_norm[l], pre_ffn_norm[l],
                    post_ffn_norm[l], w_in[l], conv_w[l], conv_b[l], w_rgate[l], b_rgate[l],
                    w_igate[l], b_igate[l], lru_lambda[l], attn_out_norm[l], lru_out_norm[l],
                    w_out[l], w_ffn_gate[l], w_ffn_up[l], w_ffn_down[l])
    return x2.reshape(batch, seq, d)
```

```python
import functools
import math

import jax
import jax.numpy as jnp
from jax import lax
from jax.experimental import pallas as pl
from jax.experimental.pallas import tpu as pltpu

F32 = jnp.float32
BF16 = jnp.bfloat16

HEAD_DIM = 128
LRU_BLOCK = 128
LRU_SEG_PAD = 4
CONV_WIDTH = 4
LRU_C = 8.0
RMS_EPS = 1e-6
LOG2_E = math.log2(math.e)
Q_SCALE = LOG2_E / math.sqrt(HEAD_DIM)
F32_TINY = float(jnp.finfo(jnp.float32).tiny)
GELU_C1 = math.sqrt(2.0 / math.pi)
GELU_C3 = GELU_C1 * 0.044715
F32_ZERO_LOG2 = -150.0
SUBLANES = 8
NORM_ROWS = 128
FFN_SLAB = 256
VMEM_LIMIT = 60 * 1024 * 1024


def _params(n_axes):
    return pltpu.CompilerParams(dimension_semantics=("arbitrary",) * n_axes,
                                vmem_limit_bytes=VMEM_LIMIT)


def _rms(x, gain, eps=RMS_EPS):
    var = jnp.mean(x * x, axis=-1, keepdims=True)
    return (x * lax.rsqrt(var + eps)) * gain


def _log_sigmoid(z):
    return jnp.minimum(z, 0.0) - jnp.log1p(jnp.exp(-jnp.abs(z)))


def _inproj_kernel(x_hbm, g_ref, w_ref, qkv_ref, lru_ref, hn_ref, xbuf, sem, *, n_q_tiles):
    i, j = pl.program_id(0), pl.program_id(1)
    tm = xbuf.shape[0]

    def fetch(tile):
        return pltpu.make_async_copy(x_hbm.at[pl.ds(tile * tm, tm), :], xbuf, sem)

    @pl.when(jnp.logical_and(i == 0, j == 0))
    def _():
        fetch(0).start()

    def project(rows):
        acc = jnp.dot(hn_ref[rows, :], w_ref[...].astype(BF16), preferred_element_type=F32)
        qkv_ref[rows, :] = (acc * jnp.where(j < n_q_tiles, Q_SCALE, 1.0)).astype(BF16)
        lru_ref[rows, :] = acc

    @pl.when(j == 0)
    def _():
        fetch(i).wait()
        for h0 in range(0, tm, tm // 2):
            for r0 in range(h0, h0 + tm // 2, NORM_ROWS):
                rows = slice(r0, r0 + NORM_ROWS)
                hn_ref[rows, :] = _rms(xbuf[rows, :], g_ref[...]).astype(BF16)
            project(slice(h0, h0 + tm // 2))

    @pl.when(jnp.logical_and(j == 1, i + 1 < pl.num_programs(0)))
    def _():
        fetch(i + 1).start()

    @pl.when(j > 0)
    def _():
        project(slice(0, tm))


def _inproj(x2, gain, w, qkv_width, tm=2048, tn=512):
    m, d = x2.shape
    n = w.shape[1]
    lru_width = n - qkv_width
    assert m % tm == 0 and n % tn == 0 and qkv_width % (3 * tn) == 0 and tm % (2 * NORM_ROWS) == 0
    n_qkv_tiles = qkv_width // tn
    n_q_tiles = qkv_width // (3 * tn)
    grid = (m // tm, n // tn)
    return pl.pallas_call(
        functools.partial(_inproj_kernel, n_q_tiles=n_q_tiles),
        grid=grid,
        in_specs=[
            pl.BlockSpec(memory_space=pl.ANY),
            pl.BlockSpec((1, d), lambda i, j: (0, 0)),
            pl.BlockSpec((d, tn), lambda i, j: (0, j)),
        ],
        out_specs=[
            pl.BlockSpec((tm, tn), lambda i, j: (i, jnp.minimum(j, n_qkv_tiles))),
            pl.BlockSpec((tm, tn), lambda i, j: (i, jnp.maximum(j - n_qkv_tiles, 0))),
        ],
        out_shape=[
            jax.ShapeDtypeStruct((m, qkv_width + tn), BF16),
            jax.ShapeDtypeStruct((m, lru_width), F32),
        ],
        scratch_shapes=[pltpu.VMEM((tm, d), BF16), pltpu.VMEM((tm, d), F32),
                        pltpu.SemaphoreType.DMA(())],
        compiler_params=_params(2),
        name="inproj",
    )(x2, gain, w)


def _attn_kernel(q_ref, qkv_hbm, *refs, tq, subs, heads, n_cast, k_col, v_col):
    cast_in, o_ref = refs[:n_cast], refs[n_cast]
    cast_out = refs[n_cast + 1:2 * n_cast + 1]
    tail_ref, live_ref, k_ref, v_ref, kv_sems = refs[2 * n_cast + 1:]

    def cast_weights():
        for src, dst in zip(cast_in, cast_out):
            dst[...] = src[...].astype(BF16)

    tk = tq
    step = pl.program_id(2)
    rows_per_step = subs * tq
    seq = k_ref.shape[0]
    width = k_ref.shape[1]

    def kv_fetch(chunk):
        src_rows = pl.ds(pl.program_id(0) * seq + chunk * rows_per_step, rows_per_step)
        dst_rows = pl.ds(chunk * rows_per_step, rows_per_step)
        group = pl.program_id(1)
        return [pltpu.make_async_copy(
                    qkv_hbm.at[src_rows, pl.ds(col0 + group * width, width)],
                    buf.at[dst_rows, :], kv_sems.at[which])
                for which, (col0, buf) in enumerate(((k_col, k_ref), (v_col, v_ref)))]

    @pl.when(step == 0)
    def _():
        for copy in kv_fetch(0):
            copy.start()

    for copy in kv_fetch(step):
        copy.wait()

    @pl.when(step + 1 < pl.num_programs(2))
    def _():
        for copy in kv_fetch(step + 1):
            copy.start()

    row = lax.broadcasted_iota(jnp.int32, (tk, tk), 0)
    col = lax.broadcasted_iota(jnp.int32, (tk, tk), 1)
    neg_later = jnp.where(row > col, -1.0, 0.0).astype(BF16)

    def tile(s, kj, first):
        rows = slice(s * tq, (s + 1) * tq)
        start = pl.multiple_of(kj * tk, tk)
        log_betas = []
        softplus = []
        for h in range(heads):
            sl = slice(h * HEAD_DIM, (h + 1) * HEAD_DIM)
            k_t = k_ref[pl.ds(start, tk), sl]
            z2 = lax.dot_general(q_ref[rows, sl], k_t, (((1,), (1,)), ((), ())),
                                 preferred_element_type=F32)
            pos = jnp.maximum(z2, 0.0)
            neg = z2 - pos
            log_term = jnp.log(1.0 + jnp.exp2(neg - pos)) * LOG2_E
            sp = pos + log_term
            if first:
                sp = jnp.where(col < row, sp, 0.0)
            softplus.append(sp)
            log_betas.append(neg - log_term)
        sp_all = jnp.concatenate(softplus, axis=0).astype(BF16)
        suffix_all = jnp.dot(sp_all, neg_later, preferred_element_type=F32)
        for h in range(heads):
            sl = slice(h * HEAD_DIM, (h + 1) * HEAD_DIM)
            v_t = v_ref[pl.ds(start, tk), sl]
            suffix = suffix_all[h * tq:(h + 1) * tq]
            expo = log_betas[h] + suffix
            if not first:
                expo = expo + jnp.concatenate([tail_ref[s, h]] * (tk // HEAD_DIM), axis=1)
            w = jnp.exp2(expo)
            if first:
                w = jnp.where(col < row, w, 0.0)
            pv = jnp.dot(w.astype(BF16), v_t, preferred_element_type=F32)
            total = jnp.broadcast_to(suffix[:, 0:1] - softplus[h][:, 0:1], (tq, HEAD_DIM))
            if first:
                o_ref[rows, sl] = pv
                tail_ref[s, h] = total
            else:
                o_ref[rows, sl] += pv
                tail_ref[s, h] += total

    def stick_left(s):
        return jnp.max(tail_ref[s]) > F32_ZERO_LOG2

    @pl.when(step == 0)
    def _():
        cast_weights()
        for s in range(subs):
            tile(s, s, True)
            if s > 0:
                tile(s, s - 1, False)
            live_ref[s] = stick_left(s).astype(jnp.int32)

    @pl.when(step > 0)
    def _():
        cast_weights()
        for s in range(subs):
            tile(s, step * subs + s, True)
            tile(s, step * subs + s - 1, False)
            live_ref[s] = stick_left(s).astype(jnp.int32)

    for s in range(subs):
        qb = step * subs + s

        def body(carry, s=s, qb=qb):
            it, _ = carry
            tile(s, qb - 1 - it, False)
            return it + 1, stick_left(s)

        lax.while_loop(lambda c, qb=qb: jnp.logical_and(c[0] < qb, c[1]), body,
                       (jnp.int32(1), live_ref[s] > 0))


def _attention(qkv, batch, seq, n_heads, later_weights, tq=256, subs=2, heads=8):
    m = qkv.shape[0]
    rows = tq * subs
    nq = seq // rows
    ng = n_heads // heads
    width = heads * HEAD_DIM
    n_steps = batch * ng * nq
    assert seq % rows == 0 and n_heads % heads == 0
    assert all(w.shape[0] % (16 * n_steps) == 0 for w in later_weights)
    step = lambda b, g, i: ((b * ng + g) * nq + i, 0)
    slab_specs = [pl.BlockSpec((w.shape[0] // n_steps, w.shape[1]), step) for w in later_weights]
    outs = pl.pallas_call(
        functools.partial(_attn_kernel, tq=tq, subs=subs, heads=heads,
                          n_cast=len(later_weights), k_col=ng * width, v_col=2 * ng * width),
        grid=(batch, ng, nq),
        in_specs=[
            pl.BlockSpec((rows, width), lambda b, g, i: (b * nq + i, g)),
            pl.BlockSpec(memory_space=pl.ANY),
            *slab_specs,
        ],
        out_specs=[pl.BlockSpec((rows, width), lambda b, g, i: (b * nq + i, g)), *slab_specs],
        out_shape=[jax.ShapeDtypeStruct((m, n_heads * HEAD_DIM), F32),
                   *[jax.ShapeDtypeStruct(w.shape, BF16) for w in later_weights]],
        scratch_shapes=[pltpu.VMEM((subs, heads, tq, HEAD_DIM), F32),
                        pltpu.SMEM((subs,), jnp.int32),
                        pltpu.VMEM((seq, width), BF16),
                        pltpu.VMEM((seq, width), BF16),
                        pltpu.SemaphoreType.DMA((2,))],
        compiler_params=_params(3),
        name="stickbreak_attn",
    )(qkv, qkv, *later_weights)
    return outs[0], outs[1:]


def _lru_kernel(x_ref, gate_ref, cw_ref, cb_ref, wr_ref, wi_ref, br_ref, bi_ref, lam_ref,
                gn_ref, later_w_ref, o_ref, later_w_bf_ref, xtail, stage, h_buf, hcarry, *, tt):
    t = pl.program_id(1)
    c = x_ref.shape[1]
    seg = tt // SUBLANES
    pitch = stage.shape[1] // SUBLANES

    @pl.when(t == 0)
    def _():
        xtail[...] = jnp.zeros((SUBLANES, c), F32)
        hcarry[...] = jnp.zeros((SUBLANES, c), F32)

    first_sublane = lax.broadcasted_iota(jnp.int32, (SUBLANES, LRU_BLOCK), 0) == 0
    log_sig_lam = _log_sigmoid(lam_ref[...])
    log2_a_coef = (0.5 * LRU_C * LOG2_E) * log_sig_lam
    neg_log_a_coef = (-0.5 * LRU_C) * log_sig_lam
    for n in range(c // LRU_BLOCK):
        sl = slice(n * LRU_BLOCK, (n + 1) * LRU_BLOCK)
        for k in range(SUBLANES):
            stage[n, k * pitch:k * pitch + seg, :] = x_ref[k * seg:(k + 1) * seg, sl]
        xs = [stage[n, pl.ds(j, SUBLANES, stride=pitch), :] for j in range(seg)]
        before = [jnp.where(first_sublane, xtail[SUBLANES - back:SUBLANES - back + 1, sl],
                            pltpu.roll(xs[seg - back], 1, 0))
                  for back in range(CONV_WIDTH - 1, 0, -1)]
        xs = before + xs
        taps = [cw_ref[k:k + 1, sl] for k in range(CONV_WIDTH)]
        bias = cb_ref[:, sl]
        conv = []
        for j in range(seg):
            acc = bias + taps[0] * xs[j]
            for k in range(1, CONV_WIDTH):
                acc = acc + taps[k] * xs[j + k]
            conv.append(acc)
        xc = jnp.concatenate(conv, axis=0)
        xcb = xc.astype(BF16)
        half_wr = (0.5 * wr_ref[n]).astype(BF16)
        half_wi = (0.5 * wi_ref[n]).astype(BF16)
        tanh_r = jnp.tanh(jnp.dot(xcb, half_wr, preferred_element_type=F32)
                          + 0.5 * br_ref[:, sl])
        tanh_i = jnp.tanh(jnp.dot(xcb, half_wi, preferred_element_type=F32)
                          + 0.5 * bi_ref[:, sl])
        two_r = tanh_r + 1.0
        a = jnp.exp2(two_r * log2_a_coef[:, sl])
        one_m_a2 = jnp.tanh(two_r * neg_log_a_coef[:, sl]) * (1.0 + a * a)
        mult = one_m_a2 * lax.rsqrt(jnp.maximum(one_m_a2, F32_TINY))
        b = mult * ((tanh_i + 1.0) * xc)

        a = a.reshape(seg, SUBLANES, LRU_BLOCK)
        b = b.reshape(seg, SUBLANES, LRU_BLOCK)
        hs, ps = [b[0]], [a[0]]
        for j in range(1, seg):
            hs.append(a[j] * hs[-1] + b[j])
            ps.append(a[j] * ps[-1])
        entering = [hcarry[0:1, sl]]
        for k in range(SUBLANES):
            entering.append(hs[-1][k:k + 1, :] + ps[-1][k:k + 1, :] * entering[-1])
        hcarry[:, sl] = jnp.broadcast_to(entering[-1], (SUBLANES, LRU_BLOCK))
        enter = jnp.concatenate(entering[:-1], axis=0)
        for j in range(seg):
            stage[n, pl.ds(j, SUBLANES, stride=pitch), :] = hs[j] + ps[j] * enter
        for k in range(SUBLANES):
            h_buf[k * seg:(k + 1) * seg, sl] = stage[n, k * pitch:k * pitch + seg, :]

    xtail[...] = x_ref[tt - SUBLANES:tt, :]
    later_w_bf_ref[...] = later_w_ref[...].astype(BF16)

    g = gate_ref[...]
    inner = g * (GELU_C1 + GELU_C3 * (g * g))
    y4 = (h_buf[...] * g) * (1.0 + jnp.tanh(inner))
    o_ref[...] = _rms(y4, gn_ref[...], eps=16.0 * RMS_EPS).astype(o_ref.dtype)


def _lru_branch(lru_in, conv_w, conv_b, w_r, w_i, b_r, b_i, lam, gain, later_w, batch, seq,
                tt=1024):
    m = lru_in.shape[0]
    c = lru_in.shape[1] // 2
    nt = seq // tt
    n_blocks = c // LRU_BLOCK
    assert seq % tt == 0 and tt % (SUBLANES * SUBLANES) == 0 and c % LRU_BLOCK == 0
    assert later_w.shape[0] % (16 * batch * nt) == 0
    row = lambda b, t: (0, 0)
    slab = pl.BlockSpec((later_w.shape[0] // (batch * nt), later_w.shape[1]),
                        lambda b, t: (b * nt + t, 0))
    return pl.pallas_call(
        functools.partial(_lru_kernel, tt=tt),
        grid=(batch, nt),
        in_specs=[
            pl.BlockSpec((tt, c), lambda b, t: (b * nt + t, 0)),
            pl.BlockSpec((tt, c), lambda b, t: (b * nt + t, 1)),
            pl.BlockSpec((CONV_WIDTH, c), row),
            pl.BlockSpec((1, c), row),
            pl.BlockSpec((n_blocks, LRU_BLOCK, LRU_BLOCK), lambda b, t: (0, 0, 0)),
            pl.BlockSpec((n_blocks, LRU_BLOCK, LRU_BLOCK), lambda b, t: (0, 0, 0)),
            pl.BlockSpec((1, c), row),
            pl.BlockSpec((1, c), row),
            pl.BlockSpec((1, c), row),
            pl.BlockSpec((1, c), row),
            slab,
        ],
        out_specs=[pl.BlockSpec((tt, c), lambda b, t: (b * nt + t, 0)), slab],
        out_shape=[jax.ShapeDtypeStruct((m, c), BF16),
                   jax.ShapeDtypeStruct(later_w.shape, BF16)],
        scratch_shapes=[
            pltpu.VMEM((SUBLANES, c), F32),
            pltpu.VMEM((n_blocks, SUBLANES * (tt // SUBLANES + LRU_SEG_PAD), LRU_BLOCK), F32),
            pltpu.VMEM((tt, c), F32),
            pltpu.VMEM((SUBLANES, c), F32),
        ],
        compiler_params=_params(2),
        name="rglru",
    )(lru_in, lru_in, conv_w, conv_b, w_r, w_i, b_r, b_i, lam, gain, later_w)


def _outproj_kernel(ya_ref, yl_ref, ga_ref, w_ref, x_ref, gpost_ref, x1_ref):
    ya = _rms(ya_ref[...], ga_ref[...]).astype(BF16)
    y = jnp.concatenate([ya, yl_ref[...]], axis=1)
    mix = jnp.dot(y, w_ref[...], preferred_element_type=F32)
    x1_ref[...] = x_ref[...] + _rms(mix, gpost_ref[...])


def _outproj(y_attn, y_lru, attn_gain, w_bf, x2, post_gain, tm=512):
    m, d = x2.shape
    ca = y_attn.shape[1]
    cl = y_lru.shape[1]
    assert m % tm == 0
    row = lambda i: (0, 0)
    return pl.pallas_call(
        _outproj_kernel,
        grid=(m // tm,),
        in_specs=[
            pl.BlockSpec((tm, ca), lambda i: (i, 0)),
            pl.BlockSpec((tm, cl), lambda i: (i, 0)),
            pl.BlockSpec((1, ca), row),
            pl.BlockSpec((ca + cl, d), row, pipeline_mode=pl.Buffered(1)),
            pl.BlockSpec((tm, d), lambda i: (i, 0)),
            pl.BlockSpec((1, d), row),
        ],
        out_specs=pl.BlockSpec((tm, d), lambda i: (i, 0)),
        out_shape=jax.ShapeDtypeStruct((m, d), F32),
        compiler_params=_params(1),
        name="outproj",
    )(y_attn, y_lru, attn_gain, w_bf, x2, post_gain)


def _ffn_kernel(x1_ref, gpre_ref, wg_ref, wu_ref, wd_ref, gpost_ref, o_ref, hn_ref):
    f = pl.program_id(1)
    last = pl.num_programs(1) - 1
    half = o_ref.shape[0] // 2
    halves = [slice(0, half), slice(half, 2 * half)]

    def chunks(rows):
        return [slice(r0, r0 + NORM_ROWS) for r0 in range(rows.start, rows.stop, NORM_ROWS)]

    def pre_norm(rows):
        for r in chunks(rows):
            hn_ref[r, :] = _rms(x1_ref[r, :], gpre_ref[...]).astype(BF16)

    def post_norm_residual(rows):
        for r in chunks(rows):
            o_ref[r, :] = x1_ref[r, :] + _rms(o_ref[r, :], gpost_ref[...])

    def ffn(rows, first):
        hn = hn_ref[rows, :]
        for c0 in range(0, wg_ref.shape[1], FFN_SLAB):
            cols = slice(c0, c0 + FFN_SLAB)
            gate = jnp.dot(hn, wg_ref[:, cols], preferred_element_type=F32)
            up = jnp.dot(hn, wu_ref[:, cols], preferred_element_type=F32)
            act = (jax.nn.silu(gate) * up).astype(BF16)
            part = jnp.dot(act, wd_ref[cols, :].astype(BF16), preferred_element_type=F32)
            if first and c0 == 0:
                o_ref[rows, :] = part
            else:
                o_ref[rows, :] += part

    @pl.when(f == 0)
    def _():
        for rows in halves:
            pre_norm(rows)
            ffn(rows, True)

    @pl.when(jnp.logical_and(f > 0, f < last))
    def _():
        for rows in halves:
            ffn(rows, False)

    @pl.when(f == last)
    def _():
        for rows in halves:
            ffn(rows, False)
            post_norm_residual(rows)


def _ffn(x1, pre_gain, wg, wu, wd, post_gain, tm=1024, tf=512):
    m, d = x1.shape
    dff = wg.shape[1]
    assert m % tm == 0 and dff % tf == 0 and dff // tf >= 2 and tf % FFN_SLAB == 0
    assert tm % (2 * NORM_ROWS) == 0
    return pl.pallas_call(
        _ffn_kernel,
        grid=(m // tm, dff // tf),
        in_specs=[
            pl.BlockSpec((tm, d), lambda i, f: (i, 0)),
            pl.BlockSpec((1, d), lambda i, f: (0, 0)),
            pl.BlockSpec((d, tf), lambda i, f: (0, f)),
            pl.BlockSpec((d, tf), lambda i, f: (0, f)),
            pl.BlockSpec((tf, d), lambda i, f: (f, 0)),
            pl.BlockSpec((1, d), lambda i, f: (0, 0)),
        ],
        out_specs=pl.BlockSpec((tm, d), lambda i, f: (i, 0)),
        out_shape=jax.ShapeDtypeStruct((m, d), F32),
        scratch_shapes=[pltpu.VMEM((tm, d), BF16)],
        compiler_params=_params(2),
        name="swiglu_ffn",
    )(x1, pre_gain, wg, wu, wd, post_gain)


def _layer(x2, batch, seq, pre_mix, post_mix, pre_ffn, post_ffn, w_in, conv_w, conv_b, w_r, b_r,
           w_i, b_i, lam, attn_gain, lru_gain, w_out, w_g, w_u, w_d):
    row = lambda v: v.reshape(1, -1)
    lru_width = conv_w.shape[1]
    qkv_width = w_in.shape[1] - 2 * lru_width
    n_heads = qkv_width // (3 * HEAD_DIM)

    qkv, lru_in = _inproj(x2, row(pre_mix), w_in, qkv_width)
    y_attn, (w_out_bf, w_g_bf) = _attention(qkv, batch, seq, n_heads, (w_out, w_g))
    y_lru, w_u_bf = _lru_branch(lru_in, conv_w, row(conv_b), w_r, w_i, row(b_r), row(b_i),
                                row(lam), row(lru_gain), w_u, batch, seq)
    x1 = _outproj(y_attn, y_lru, row(attn_gain), w_out_bf, x2, row(post_mix))
    return _ffn(x1, row(pre_ffn), w_g_bf, w_u_bf, w_d, row(post_ffn))


def kernel(x, pre_mix_norm, post_mix_norm, pre_ffn_norm, post_ffn_norm, w_in, conv_w, conv_b,
           w_rgate, b_rgate, w_igate, b_igate, lru_lambda, attn_out_norm, lru_out_norm, w_out,
           w_ffn_gate, w_ffn_up, w_ffn_down):
    batch, seq, d = x.shape
    x2 = x.reshape(batch * seq, d)
    for l in range(w_in.shape[0]):
        x2 = _layer(x2, batch, seq, pre_mix_norm[l], post_mix_norm[l], pre_ffn_norm[l],
                    post_ffn_norm[l], w_in[l], conv_w[l], conv_b[l], w_rgate[l], b_rgate[l],
                    w_igate[l], b_igate[l], lru_lambda[l], attn_out_norm[l], lru_out_norm[l],
                    w_out[l], w_ffn_gate[l], w_ffn_up[l], w_ffn_down[l])
    return x2.reshape(batch, seq, d)
```

```python
import functools
import math

import jax
import jax.numpy as jnp
from jax import lax
from jax.experimental import pallas as pl
from jax.experimental.pallas import tpu as pltpu

F32 = jnp.float32
BF16 = jnp.bfloat16

HEAD_DIM = 128
LRU_BLOCK = 128
LRU_SEG_PAD = 4
CONV_WIDTH = 4
LRU_C = 8.0
RMS_EPS = 1e-6
LOG2_E = math.log2(math.e)
Q_SCALE = LOG2_E / math.sqrt(HEAD_DIM)
F32_TINY = float(jnp.finfo(jnp.float32).tiny)
GELU_C1 = math.sqrt(2.0 / math.pi)
GELU_C3 = GELU_C1 * 0.044715
F32_ZERO_LOG2 = -150.0
SUBLANES = 8
NORM_ROWS = 128
FFN_SLAB = 256
VMEM_LIMIT = 60 * 1024 * 1024


def _params(n_axes):
    return pltpu.CompilerParams(dimension_semantics=("arbitrary",) * n_axes,
                                vmem_limit_bytes=VMEM_LIMIT)


def _rms(x, gain, eps=RMS_EPS):
    var = jnp.mean(x * x, axis=-1, keepdims=True)
    return (x * lax.rsqrt(var + eps)) * gain


def _log_sigmoid(z):
    return jnp.minimum(z, 0.0) - jnp.log1p(jnp.exp(-jnp.abs(z)))


def _inproj_kernel(x_hbm, g_ref, w_ref, qkv_ref, lru_ref, hn_ref, xbuf, sem, *, n_q_tiles):
    i, j = pl.program_id(0), pl.program_id(1)
    tm = xbuf.shape[0]

    def fetch(tile):
        return pltpu.make_async_copy(x_hbm.at[pl.ds(tile * tm, tm), :], xbuf, sem)

    @pl.when(jnp.logical_and(i == 0, j == 0))
    def _():
        fetch(0).start()

    def project(rows):
        acc = jnp.dot(hn_ref[rows, :], w_ref[...].astype(BF16), preferred_element_type=F32)
        qkv_ref[rows, :] = (acc * jnp.where(j < n_q_tiles, Q_SCALE, 1.0)).astype(BF16)
        lru_ref[rows, :] = acc

    @pl.when(j == 0)
    def _():
        fetch(i).wait()
        for h0 in range(0, tm, tm // 2):
            for r0 in range(h0, h0 + tm // 2, NORM_ROWS):
                rows = slice(r0, r0 + NORM_ROWS)
                hn_ref[rows, :] = _rms(xbuf[rows, :], g_ref[...]).astype(BF16)
            project(slice(h0, h0 + tm // 2))

    @pl.when(jnp.logical_and(j == 1, i + 1 < pl.num_programs(0)))
    def _():
        fetch(i + 1).start()

    @pl.when(j > 0)
    def _():
        project(slice(0, tm))


def _inproj(x2, gain, w, qkv_width, tm=2048, tn=512):
    m, d = x2.shape
    n = w.shape[1]
    lru_width = n - qkv_width
    assert m % tm == 0 and n % tn == 0 and qkv_width % (3 * tn) == 0 and tm % (2 * NORM_ROWS) == 0
    n_qkv_tiles = qkv_width // tn
    n_q_tiles = qkv_width // (3 * tn)
    grid = (m // tm, n // tn)
    return pl.pallas_call(
        functools.partial(_inproj_kernel, n_q_tiles=n_q_tiles),
        grid=grid,
        in_specs=[
            pl.BlockSpec(memory_space=pl.ANY),
            pl.BlockSpec((1, d), lambda i, j: (0, 0)),
            pl.BlockSpec((d, tn), lambda i, j: (0, j)),
        ],
        out_specs=[
            pl.BlockSpec((tm, tn), lambda i, j: (i, jnp.minimum(j, n_qkv_tiles))),
            pl.BlockSpec((tm, tn), lambda i, j: (i, jnp.maximum(j - n_qkv_tiles, 0))),
        ],
        out_shape=[
            jax.ShapeDtypeStruct((m, qkv_width + tn), BF16),
            jax.ShapeDtypeStruct((m, lru_width), F32),
        ],
        scratch_shapes=[pltpu.VMEM((tm, d), BF16), pltpu.VMEM((tm, d), F32),
                        pltpu.SemaphoreType.DMA(())],
        compiler_params=_params(2),
        name="inproj",
    )(x2, gain, w)


def _attn_kernel(q_ref, qkv_hbm, *refs, tq, subs, heads, n_cast, k_col, v_col):
    cast_in, o_ref = refs[:n_cast], refs[n_cast]
    cast_out = refs[n_cast + 1:2 * n_cast + 1]
    tail_ref, live_ref, k_ref, v_ref, kv_sems = refs[2 * n_cast + 1:]

    def cast_weights():
        for src, dst in zip(cast_in, cast_out):
            dst[...] = src[...].astype(BF16)

    tk = tq
    step = pl.program_id(2)
    rows_per_step = subs * tq
    seq = k_ref.shape[0]
    width = k_ref.shape[1]

    def kv_fetch(chunk):
        src_rows = pl.ds(pl.program_id(0) * seq + chunk * rows_per_step, rows_per_step)
        dst_rows = pl.ds(chunk * rows_per_step, rows_per_step)
        group = pl.program_id(1)
        return [pltpu.make_async_copy(
                    qkv_hbm.at[src_rows, pl.ds(col0 + group * width, width)],
                    buf.at[dst_rows, :], kv_sems.at[which])
                for which, (col0, buf) in enumerate(((k_col, k_ref), (v_col, v_ref)))]

    @pl.when(step == 0)
    def _():
        for copy in kv_fetch(0):
            copy.start()

    for copy in kv_fetch(step):
        copy.wait()

    @pl.when(step + 1 < pl.num_programs(2))
    def _():
        for copy in kv_fetch(step + 1):
            copy.start()

    row = lax.broadcasted_iota(jnp.int32, (tk, tk), 0)
    col = lax.broadcasted_iota(jnp.int32, (tk, tk), 1)
    neg_later = jnp.where(row > col, -1.0, 0.0).astype(BF16)

    def tile(s, kj, first):
        rows = slice(s * tq, (s + 1) * tq)
        start = pl.multiple_of(kj * tk, tk)
        log_betas = []
        softplus = []
        for h in range(heads):
            sl = slice(h * HEAD_DIM, (h + 1) * HEAD_DIM)
            k_t = k_ref[pl.ds(start, tk), sl]
            z2 = lax.dot_general(q_ref[rows, sl], k_t, (((1,), (1,)), ((), ())),
                                 preferred_element_type=F32)
            pos = jnp.maximum(z2, 0.0)
            neg = z2 - pos
            log_term = jnp.log(1.0 + jnp.exp2(neg - pos)) * LOG2_E
            sp = pos + log_term
            if first:
                sp = jnp.where(col < row, sp, 0.0)
            softplus.append(sp)
            log_betas.append(neg - log_term)
        sp_all = jnp.concatenate(softplus, axis=0).astype(BF16)
        suffix_all = jnp.dot(sp_all, neg_later, preferred_element_type=F32)
        for h in range(heads):
            sl = slice(h * HEAD_DIM, (h + 1) * HEAD_DIM)
            v_t = v_ref[pl.ds(start, tk), sl]
            suffix = suffix_all[h * tq:(h + 1) * tq]
            expo = log_betas[h] + suffix
            if not first:
                expo = expo + jnp.concatenate([tail_ref[s, h]] * (tk // HEAD_DIM), axis=1)
            w = jnp.exp2(expo)
            if first:
                w = jnp.where(col < row, w, 0.0)
            pv = jnp.dot(w.astype(BF16), v_t, preferred_element_type=F32)
            total = jnp.broadcast_to(suffix[:, 0:1] - softplus[h][:, 0:1], (tq, HEAD_DIM))
            if first:
                o_ref[rows, sl] = pv
                tail_ref[s, h] = total
            else:
                o_ref[rows, sl] += pv
                tail_ref[s, h] += total

    def stick_left(s):
        return jnp.max(tail_ref[s]) > F32_ZERO_LOG2

    @pl.when(step == 0)
    def _():
        cast_weights()
        for s in range(subs):
            tile(s, s, True)
            if s > 0:
                tile(s, s - 1, False)
            live_ref[s] = stick_left(s).astype(jnp.int32)

    @pl.when(step > 0)
    def _():
        cast_weights()
        for s in range(subs):
            tile(s, step * subs + s, True)
            tile(s, step * subs + s - 1, False)
            live_ref[s] = stick_left(s).astype(jnp.int32)

    for s in range(subs):
        qb = step * subs + s

        def body(carry, s=s, qb=qb):
            it, _ = carry
            tile(s, qb - 1 - it, False)
            return it + 1, stick_left(s)

        lax.while_loop(lambda c, qb=qb: jnp.logical_and(c[0] < qb, c[1]), body,
                       (jnp.int32(1), live_ref[s] > 0))


def _attention(qkv, batch, seq, n_heads, later_weights, tq=256, subs=2, heads=8):
    m = qkv.shape[0]
    rows = tq * subs
    nq = seq // rows
    ng = n_heads // heads
    width = heads * HEAD_DIM
    n_steps = batch * ng * nq
    assert seq % rows == 0 and n_heads % heads == 0
    assert all(w.shape[0] % (16 * n_steps) == 0 for w in later_weights)
    step = lambda b, g, i: ((b * ng + g) * nq + i, 0)
    slab_specs = [pl.BlockSpec((w.shape[0] // n_steps, w.shape[1]), step) for w in later_weights]
    outs = pl.pallas_call(
        functools.partial(_attn_kernel, tq=tq, subs=subs, heads=heads,
                          n_cast=len(later_weights), k_col=ng * width, v_col=2 * ng * width),
        grid=(batch, ng, nq),
        in_specs=[
            pl.BlockSpec((rows, width), lambda b, g, i: (b * nq + i, g)),
            pl.BlockSpec(memory_space=pl.ANY),
            *slab_specs,
        ],
        out_specs=[pl.BlockSpec((rows, width), lambda b, g, i: (b * nq + i, g)), *slab_specs],
        out_shape=[jax.ShapeDtypeStruct((m, n_heads * HEAD_DIM), F32),
                   *[jax.ShapeDtypeStruct(w.shape, BF16) for w in later_weights]],
        scratch_shapes=[pltpu.VMEM((subs, heads, tq, HEAD_DIM), F32),
                        pltpu.SMEM((subs,), jnp.int32),
                        pltpu.VMEM((seq, width), BF16),
                        pltpu.VMEM((seq, width), BF16),
                        pltpu.SemaphoreType.DMA((2,))],
        compiler_params=_params(3),
        name="stickbreak_attn",
    )(qkv, qkv, *later_weights)
    return outs[0], outs[1:]


def _lru_kernel(x_ref, gate_ref, cw_ref, cb_ref, wr_ref, wi_ref, br_ref, bi_ref, lam_ref,
                gn_ref, later_w_ref, o_ref, later_w_bf_ref, xtail, stage, h_buf, hcarry, *, tt):
    t = pl.program_id(1)
    c = x_ref.shape[1]
    seg = tt // SUBLANES
    pitch = stage.shape[1] // SUBLANES

    @pl.when(t == 0)
    def _():
        xtail[...] = jnp.zeros((SUBLANES, c), F32)
        hcarry[...] = jnp.zeros((SUBLANES, c), F32)

    first_sublane = lax.broadcasted_iota(jnp.int32, (SUBLANES, LRU_BLOCK), 0) == 0
    log_sig_lam = _log_sigmoid(lam_ref[...])
    log2_a_coef = (0.5 * LRU_C * LOG2_E) * log_sig_lam
    neg_log_a_coef = (-0.5 * LRU_C) * log_sig_lam
    for n in range(c // LRU_BLOCK):
        sl = slice(n * LRU_BLOCK, (n + 1) * LRU_BLOCK)
        for k in range(SUBLANES):
            stage[n, k * pitch:k * pitch + seg, :] = x_ref[k * seg:(k + 1) * seg, sl]
        xs = [stage[n, pl.ds(j, SUBLANES, stride=pitch), :] for j in range(seg)]
        before = [jnp.where(first_sublane, xtail[SUBLANES - back:SUBLANES - back + 1, sl],
                            pltpu.roll(xs[seg - back], 1, 0))
                  for back in range(CONV_WIDTH - 1, 0, -1)]
        xs = before + xs
        taps = [cw_ref[k:k + 1, sl] for k in range(CONV_WIDTH)]
        bias = cb_ref[:, sl]
        conv = []
        for j in range(seg):
            acc = bias + taps[0] * xs[j]
            for k in range(1, CONV_WIDTH):
                acc = acc + taps[k] * xs[j + k]
            conv.append(acc)
        xc = jnp.concatenate(conv, axis=0)
        xcb = xc.astype(BF16)
        half_wr = (0.5 * wr_ref[n]).astype(BF16)
        half_wi = (0.5 * wi_ref[n]).astype(BF16)
        tanh_r = jnp.tanh(jnp.dot(xcb, half_wr, preferred_element_type=F32)
                          + 0.5 * br_ref[:, sl])
        tanh_i = jnp.tanh(jnp.dot(xcb, half_wi, preferred_element_type=F32)
                          + 0.5 * bi_ref[:, sl])
        two_r = tanh_r + 1.0
        a = jnp.exp2(two_r * log2_a_coef[:, sl])
        one_m_a2 = jnp.tanh(two_r * neg_log_a_coef[:, sl]) * (1.0 + a * a)
        mult = one_m_a2 * lax.rsqrt(jnp.maximum(one_m_a2, F32_TINY))
        b = mult * ((tanh_i + 1.0) * xc)

        a = a.reshape(seg, SUBLANES, LRU_BLOCK)
        b = b.reshape(seg, SUBLANES, LRU_BLOCK)
        hs, ps = [b[0]], [a[0]]
        for j in range(1, seg):
            hs.append(a[j] * hs[-1] + b[j])
            ps.append(a[j] * ps[-1])
        entering = [hcarry[0:1, sl]]
        for k in range(SUBLANES):
            entering.append(hs[-1][k:k + 1, :] + ps[-1][k:k + 1, :] * entering[-1])
        hcarry[:, sl] = jnp.broadcast_to(entering[-1], (SUBLANES, LRU_BLOCK))
        enter = jnp.concatenate(entering[:-1], axis=0)
        for j in range(seg):
            stage[n, pl.ds(j, SUBLANES, stride=pitch), :] = hs[j] + ps[j] * enter
        for k in range(SUBLANES):
            h_buf[k * seg:(k + 1) * seg, sl] = stage[n, k * pitch:k * pitch + seg, :]

    xtail[...] = x_ref[tt - SUBLANES:tt, :]
    later_w_bf_ref[...] = later_w_ref[...].astype(BF16)

    g = gate_ref[...]
    inner = g * (GELU_C1 + GELU_C3 * (g * g))
    y4 = (h_buf[...] * g) * (1.0 + jnp.tanh(inner))
    o_ref[...] = _rms(y4, gn_ref[...], eps=16.0 * RMS_EPS).astype(o_ref.dtype)


def _lru_branch(lru_in, conv_w, conv_b, w_r, w_i, b_r, b_i, lam, gain, later_w, batch, seq,
                tt=1024):
    m = lru_in.shape[0]
    c = lru_in.shape[1] // 2
    nt = seq // tt
    n_blocks = c // LRU_BLOCK
    assert seq % tt == 0 and tt % (SUBLANES * SUBLANES) == 0 and c % LRU_BLOCK == 0
    assert later_w.shape[0] % (16 * batch * nt) == 0
    row = lambda b, t: (0, 0)
    slab = pl.BlockSpec((later_w.shape[0] // (batch * nt), later_w.shape[1]),
                        lambda b, t: (b * nt + t, 0))
    return pl.pallas_call(
        functools.partial(_lru_kernel, tt=tt),
        grid=(batch, nt),
        in_specs=[
            pl.BlockSpec((tt, c), lambda b, t: (b * nt + t, 0)),
            pl.BlockSpec((tt, c), lambda b, t: (b * nt + t, 1)),
            pl.BlockSpec((CONV_WIDTH, c), row),
            pl.BlockSpec((1, c), row),
            pl.BlockSpec((n_blocks, LRU_BLOCK, LRU_BLOCK), lambda b, t: (0, 0, 0)),
            pl.BlockSpec((n_blocks, LRU_BLOCK, LRU_BLOCK), lambda b, t: (0, 0, 0)),
            pl.BlockSpec((1, c), row),
            pl.BlockSpec((1, c), row),
            pl.BlockSpec((1, c), row),
            pl.BlockSpec((1, c), row),
            slab,
        ],
        out_specs=[pl.BlockSpec((tt, c), lambda b, t: (b * nt + t, 0)), slab],
        out_shape=[jax.ShapeDtypeStruct((m, c), BF16),
                   jax.ShapeDtypeStruct(later_w.shape, BF16)],
        scratch_shapes=[
            pltpu.VMEM((SUBLANES, c), F32),
            pltpu.VMEM((n_blocks, SUBLANES * (tt // SUBLANES + LRU_SEG_PAD), LRU_BLOCK), F32),
            pltpu.VMEM((tt, c), F32),
            pltpu.VMEM((SUBLANES, c), F32),
        ],
        compiler_params=_params(2),
        name="rglru",
    )(lru_in, lru_in, conv_w, conv_b, w_r, w_i, b_r, b_i, lam, gain, later_w)


def _outproj_kernel(ya_ref, yl_ref, ga_ref, w_ref, x_ref, gpost_ref, x1_ref):
    ya = _rms(ya_ref[...], ga_ref[...]).astype(BF16)
    y = jnp.concatenate([ya, yl_ref[...]], axis=1)
    mix = jnp.dot(y, w_ref[...], preferred_element_type=F32)
    x1_ref[...] = x_ref[...] + _rms(mix, gpost_ref[...])


def _outproj(y_attn, y_lru, attn_gain, w_bf, x2, post_gain, tm=512):
    m, d = x2.shape
    ca = y_attn.shape[1]
    cl = y_lru.shape[1]
    assert m % tm == 0
    row = lambda i: (0, 0)
    return pl.pallas_call(
        _outproj_kernel,
        grid=(m // tm,),
        in_specs=[
            pl.BlockSpec((tm, ca), lambda i: (i, 0)),
            pl.BlockSpec((tm, cl), lambda i: (i, 0)),
            pl.BlockSpec((1, ca), row),
            pl.BlockSpec((ca + cl, d), row, pipeline_mode=pl.Buffered(1)),
            pl.BlockSpec((tm, d), lambda i: (i, 0)),
            pl.BlockSpec((1, d), row),
        ],
        out_specs=pl.BlockSpec((tm, d), lambda i: (i, 0)),
        out_shape=jax.ShapeDtypeStruct((m, d), F32),
        compiler_params=_params(1),
        name="outproj",
    )(y_attn, y_lru, attn_gain, w_bf, x2, post_gain)


def _ffn_kernel(x1_ref, gpre_ref, wg_hbm, wu_hbm, wd_hbm, gpost_ref, o_ref,
                hn_ref, wg_buf, wu_buf, wd_buf, sems, *, tf):
    i = pl.program_id(0)
    n_f = wg_hbm.shape[1] // tf
    half = o_ref.shape[0] // 2
    halves = [slice(0, half), slice(half, 2 * half)]

    def weight_copies(f, slot):
        cols = pl.ds(f * tf, tf)
        return (pltpu.make_async_copy(wg_hbm.at[:, cols], wg_buf.at[slot], sems.at[0, slot]),
                pltpu.make_async_copy(wu_hbm.at[:, cols], wu_buf.at[slot], sems.at[1, slot]),
                pltpu.make_async_copy(wd_hbm.at[cols, :], wd_buf.at[slot], sems.at[2, slot]))

    def start(f, slot):
        for copy in weight_copies(f, slot):
            copy.start()

    def wait(f, slot):
        for copy in weight_copies(f, slot):
            copy.wait()

    def slot_of(f):
        return (i * n_f + f) % 2

    def chunks(rows):
        return [slice(r0, r0 + NORM_ROWS) for r0 in range(rows.start, rows.stop, NORM_ROWS)]

    def pre_norm(rows):
        for r in chunks(rows):
            hn_ref[r, :] = _rms(x1_ref[r, :], gpre_ref[...]).astype(BF16)

    def post_norm_residual(rows):
        for r in chunks(rows):
            o_ref[r, :] = x1_ref[r, :] + _rms(o_ref[r, :], gpost_ref[...])

    def ffn(rows, slot, first):
        hn = hn_ref[rows, :]
        for c0 in range(0, tf, FFN_SLAB):
            cols = slice(c0, c0 + FFN_SLAB)
            gate = jnp.dot(hn, wg_buf[slot, :, cols], preferred_element_type=F32)
            up = jnp.dot(hn, wu_buf[slot, :, cols], preferred_element_type=F32)
            act = (jax.nn.silu(gate) * up).astype(BF16)
            part = jnp.dot(act, wd_buf[slot, cols, :].astype(BF16), preferred_element_type=F32)
            if first and c0 == 0:
                o_ref[rows, :] = part
            else:
                o_ref[rows, :] += part

    @pl.when(i == 0)
    def _():
        start(0, slot_of(0))

    slot = slot_of(0)
    wait(0, slot)
    start(1, 1 - slot)
    for rows in halves:
        pre_norm(rows)
        ffn(rows, slot, True)

    def steady(f, carry):
        slot = slot_of(f)
        wait(f, slot)
        start(f + 1, 1 - slot)
        for rows in halves:
            ffn(rows, slot, False)
        return carry

    lax.fori_loop(1, n_f - 1, steady, 0)

    slot = slot_of(n_f - 1)
    wait(n_f - 1, slot)

    @pl.when(i + 1 < pl.num_programs(0))
    def _():
        start(0, 1 - slot)

    for rows in halves:
        ffn(rows, slot, False)
        post_norm_residual(rows)


def _ffn(x1, pre_gain, wg, wu, wd, post_gain, tm=1024, tf=512):
    m, d = x1.shape
    dff = wg.shape[1]
    assert m % tm == 0 and dff % tf == 0 and dff // tf >= 2 and tf % FFN_SLAB == 0
    assert tm % (2 * NORM_ROWS) == 0
    hbm = pl.BlockSpec(memory_space=pl.ANY)
    return pl.pallas_call(
        functools.partial(_ffn_kernel, tf=tf),
        grid=(m // tm,),
        in_specs=[
            pl.BlockSpec((tm, d), lambda i: (i, 0)),
            pl.BlockSpec((1, d), lambda i: (0, 0)),
            hbm, hbm, hbm,
            pl.BlockSpec((1, d), lambda i: (0, 0)),
        ],
        out_specs=pl.BlockSpec((tm, d), lambda i: (i, 0)),
        out_shape=jax.ShapeDtypeStruct((m, d), F32),
        scratch_shapes=[
            pltpu.VMEM((tm, d), BF16),
            pltpu.VMEM((2, d, tf), wg.dtype),
            pltpu.VMEM((2, d, tf), wu.dtype),
            pltpu.VMEM((2, tf, d), wd.dtype),
            pltpu.SemaphoreType.DMA((3, 2)),
        ],
        compiler_params=_params(1),
        name="swiglu_ffn",
    )(x1, pre_gain, wg, wu, wd, post_gain)


def _layer(x2, batch, seq, pre_mix, post_mix, pre_ffn, post_ffn, w_in, conv_w, conv_b, w_r, b_r,
           w_i, b_i, lam, attn_gain, lru_gain, w_out, w_g, w_u, w_d):
    row = lambda v: v.reshape(1, -1)
    lru_width = conv_w.shape[1]
    qkv_width = w_in.shape[1] - 2 * lru_width
    n_heads = qkv_width // (3 * HEAD_DIM)

    qkv, lru_in = _inproj(x2, row(pre_mix), w_in, qkv_width)
    y_attn, (w_out_bf, w_g_bf) = _attention(qkv, batch, seq, n_heads, (w_out, w_g))
    y_lru, w_u_bf = _lru_branch(lru_in, conv_w, row(conv_b), w_r, w_i, row(b_r), row(b_i),
                                row(lam), row(lru_gain), w_u, batch, seq)
    x1 = _outproj(y_attn, y_lru, row(attn_gain), w_out_bf, x2, row(post_mix))
    return _ffn(x1, row(pre_ffn), w_g_bf, w_u_bf, w_d, row(post_ffn))


def kernel(x, pre_mix_norm, post_mix_norm, pre_ffn_norm, post_ffn_norm, w_in, conv_w, conv_b,
           w_rgate, b_rgate, w_igate, b_igate, lru_lambda, attn_out_norm, lru_out_norm, w_out,
           w_ffn_gate, w_ffn_up, w_ffn_down):
    batch, seq, d = x.shape
    x2 = x.reshape(batch * seq, d)
    for l in range(w_in.shape[0]):
        x2 = _layer(x2, batch, seq, pre_mix_norm[l], post_mix_norm[l], pre_ffn_norm[l],
                    post_ffn_norm[l], w_in[l], conv_w[l], conv_b[l], w_rgate[l], b_rgate[l],
                    w_igate[l], b_igate[l], lru_lambda[l], attn_out_norm[l], lru_out_norm[l],
                    w_out[l], w_ffn_gate[l], w_ffn_up[l], w_ffn_down[l])
    return x2.reshape(batch, seq, d)
```

```python
import functools
import math

import jax
import jax.numpy as jnp
from jax import lax
from jax.experimental import pallas as pl
from jax.experimental.pallas import tpu as pltpu

F32 = jnp.float32
BF16 = jnp.bfloat16

HEAD_DIM = 128
ATTN_STACK = 4
LRU_BLOCK = 128
LRU_SEG_PAD = 4
CONV_WIDTH = 4
LRU_C = 8.0
RMS_EPS = 1e-6
LOG2_E = math.log2(math.e)
Q_SCALE = LOG2_E / math.sqrt(HEAD_DIM)
F32_TINY = float(jnp.finfo(jnp.float32).tiny)
GELU_C1 = math.sqrt(2.0 / math.pi)
GELU_C3 = GELU_C1 * 0.044715
F32_ZERO_LOG2 = -150.0
SUBLANES = 8
NORM_ROWS = 128
FFN_SLAB = 256
VMEM_LIMIT = 60 * 1024 * 1024


def _params(n_axes):
    return pltpu.CompilerParams(dimension_semantics=("arbitrary",) * n_axes,
                                vmem_limit_bytes=VMEM_LIMIT)


def _rms(x, gain, eps=RMS_EPS):
    var = jnp.mean(x * x, axis=-1, keepdims=True)
    return (x * lax.rsqrt(var + eps)) * gain


def _log_sigmoid(z):
    return jnp.minimum(z, 0.0) - jnp.log1p(jnp.exp(-jnp.abs(z)))


def _inproj_kernel(x_hbm, g_ref, w_ref, qkv_ref, lru_ref, hn_ref, xbuf, sem, *, n_q_tiles):
    i, j = pl.program_id(0), pl.program_id(1)
    tm = xbuf.shape[0]

    def fetch(tile):
        return pltpu.make_async_copy(x_hbm.at[pl.ds(tile * tm, tm), :], xbuf, sem)

    @pl.when(jnp.logical_and(i == 0, j == 0))
    def _():
        fetch(0).start()

    def project(rows):
        acc = jnp.dot(hn_ref[rows, :], w_ref[...].astype(BF16), preferred_element_type=F32)
        qkv_ref[rows, :] = (acc * jnp.where(j < n_q_tiles, Q_SCALE, 1.0)).astype(BF16)
        lru_ref[rows, :] = acc

    @pl.when(j == 0)
    def _():
        fetch(i).wait()
        for h0 in range(0, tm, tm // 2):
            for r0 in range(h0, h0 + tm // 2, NORM_ROWS):
                rows = slice(r0, r0 + NORM_ROWS)
                hn_ref[rows, :] = _rms(xbuf[rows, :], g_ref[...]).astype(BF16)
            project(slice(h0, h0 + tm // 2))

    @pl.when(jnp.logical_and(j == 1, i + 1 < pl.num_programs(0)))
    def _():
        fetch(i + 1).start()

    @pl.when(j > 0)
    def _():
        project(slice(0, tm))


def _inproj(x2, gain, w, qkv_width, tm=2048, tn=512):
    m, d = x2.shape
    n = w.shape[1]
    lru_width = n - qkv_width
    assert m % tm == 0 and n % tn == 0 and qkv_width % (3 * tn) == 0 and tm % (2 * NORM_ROWS) == 0
    n_qkv_tiles = qkv_width // tn
    n_q_tiles = qkv_width // (3 * tn)
    grid = (m // tm, n // tn)
    return pl.pallas_call(
        functools.partial(_inproj_kernel, n_q_tiles=n_q_tiles),
        grid=grid,
        in_specs=[
            pl.BlockSpec(memory_space=pl.ANY),
            pl.BlockSpec((1, d), lambda i, j: (0, 0)),
            pl.BlockSpec((d, tn), lambda i, j: (0, j)),
        ],
        out_specs=[
            pl.BlockSpec((tm, tn), lambda i, j: (i, jnp.minimum(j, n_qkv_tiles))),
            pl.BlockSpec((tm, tn), lambda i, j: (i, jnp.maximum(j - n_qkv_tiles, 0))),
        ],
        out_shape=[
            jax.ShapeDtypeStruct((m, qkv_width + tn), BF16),
            jax.ShapeDtypeStruct((m, lru_width), F32),
        ],
        scratch_shapes=[pltpu.VMEM((tm, d), BF16), pltpu.VMEM((tm, d), F32),
                        pltpu.SemaphoreType.DMA(())],
        compiler_params=_params(2),
        name="inproj",
    )(x2, gain, w)


def _attn_kernel(q_ref, qkv_hbm, *refs, tq, subs, heads, n_cast, k_col, v_col):
    cast_in, o_ref = refs[:n_cast], refs[n_cast]
    cast_out = refs[n_cast + 1:2 * n_cast + 1]
    tail_ref, live_ref, k_ref, v_ref, kv_sems = refs[2 * n_cast + 1:]

    def cast_weights():
        for src, dst in zip(cast_in, cast_out):
            dst[...] = src[...].astype(BF16)

    tk = tq
    step = pl.program_id(2)
    rows_per_step = subs * tq
    seq = k_ref.shape[0]
    width = k_ref.shape[1]

    def kv_fetch(chunk):
        src_rows = pl.ds(pl.program_id(0) * seq + chunk * rows_per_step, rows_per_step)
        dst_rows = pl.ds(chunk * rows_per_step, rows_per_step)
        group = pl.program_id(1)
        return [pltpu.make_async_copy(
                    qkv_hbm.at[src_rows, pl.ds(col0 + group * width, width)],
                    buf.at[dst_rows, :], kv_sems.at[which])
                for which, (col0, buf) in enumerate(((k_col, k_ref), (v_col, v_ref)))]

    @pl.when(step == 0)
    def _():
        for copy in kv_fetch(0):
            copy.start()

    for copy in kv_fetch(step):
        copy.wait()

    @pl.when(step + 1 < pl.num_programs(2))
    def _():
        for copy in kv_fetch(step + 1):
            copy.start()

    row = lax.broadcasted_iota(jnp.int32, (tk, tk), 0)
    col = lax.broadcasted_iota(jnp.int32, (tk, tk), 1)
    neg_later = jnp.where(row > col, -1.0, 0.0).astype(BF16)

    def tile(s, kj, first):
        rows = slice(s * tq, (s + 1) * tq)
        start = pl.multiple_of(kj * tk, tk)
        for h0 in range(0, heads, ATTN_STACK):
            head_group(s, rows, start, range(h0, min(h0 + ATTN_STACK, heads)), first)

    def head_group(s, rows, start, group, first):
        log_betas = []
        softplus = []
        for h in group:
            sl = slice(h * HEAD_DIM, (h + 1) * HEAD_DIM)
            k_t = k_ref[pl.ds(start, tk), sl]
            z2 = lax.dot_general(q_ref[rows, sl], k_t, (((1,), (1,)), ((), ())),
                                 preferred_element_type=F32)
            pos = jnp.maximum(z2, 0.0)
            neg = z2 - pos
            log_term = jnp.log(1.0 + jnp.exp2(neg - pos)) * LOG2_E
            sp = pos + log_term
            if first:
                sp = jnp.where(col < row, sp, 0.0)
            softplus.append(sp)
            log_betas.append(neg - log_term)
        sp_all = jnp.concatenate(softplus, axis=0).astype(BF16)
        suffix_all = jnp.dot(sp_all, neg_later, preferred_element_type=F32)
        for n, h in enumerate(group):
            sl = slice(h * HEAD_DIM, (h + 1) * HEAD_DIM)
            v_t = v_ref[pl.ds(start, tk), sl]
            suffix = suffix_all[n * tq:(n + 1) * tq]
            expo = log_betas[n] + suffix
            if not first:
                expo = expo + jnp.concatenate([tail_ref[s, h]] * (tk // HEAD_DIM), axis=1)
            w = jnp.exp2(expo)
            if first:
                w = jnp.where(col < row, w, 0.0)
            pv = jnp.dot(w.astype(BF16), v_t, preferred_element_type=F32)
            total = jnp.broadcast_to(suffix[:, 0:1] - softplus[n][:, 0:1], (tq, HEAD_DIM))
            if first:
                o_ref[rows, sl] = pv
                tail_ref[s, h] = total
            else:
                o_ref[rows, sl] += pv
                tail_ref[s, h] += total

    def stick_left(s):
        return jnp.max(tail_ref[s]) > F32_ZERO_LOG2

    @pl.when(step == 0)
    def _():
        cast_weights()
        for s in range(subs):
            tile(s, s, True)
            if s > 0:
                tile(s, s - 1, False)
            live_ref[s] = stick_left(s).astype(jnp.int32)

    @pl.when(step > 0)
    def _():
        cast_weights()
        for s in range(subs):
            tile(s, step * subs + s, True)
            tile(s, step * subs + s - 1, False)
            live_ref[s] = stick_left(s).astype(jnp.int32)

    for s in range(subs):
        qb = step * subs + s

        def body(carry, s=s, qb=qb):
            it, _ = carry
            tile(s, qb - 1 - it, False)
            return it + 1, stick_left(s)

        lax.while_loop(lambda c, qb=qb: jnp.logical_and(c[0] < qb, c[1]), body,
                       (jnp.int32(1), live_ref[s] > 0))


def _attention(qkv, batch, seq, n_heads, later_weights, tq=256, subs=2, heads=8):
    m = qkv.shape[0]
    rows = tq * subs
    nq = seq // rows
    ng = n_heads // heads
    width = heads * HEAD_DIM
    n_steps = batch * ng * nq
    assert seq % rows == 0 and n_heads % heads == 0
    assert all(w.shape[0] % (16 * n_steps) == 0 for w in later_weights)
    step = lambda b, g, i: ((b * ng + g) * nq + i, 0)
    slab_specs = [pl.BlockSpec((w.shape[0] // n_steps, w.shape[1]), step) for w in later_weights]
    outs = pl.pallas_call(
        functools.partial(_attn_kernel, tq=tq, subs=subs, heads=heads,
                          n_cast=len(later_weights), k_col=ng * width, v_col=2 * ng * width),
        grid=(batch, ng, nq),
        in_specs=[
            pl.BlockSpec((rows, width), lambda b, g, i: (b * nq + i, g)),
            pl.BlockSpec(memory_space=pl.ANY),
            *slab_specs,
        ],
        out_specs=[pl.BlockSpec((rows, width), lambda b, g, i: (b * nq + i, g)), *slab_specs],
        out_shape=[jax.ShapeDtypeStruct((m, n_heads * HEAD_DIM), F32),
                   *[jax.ShapeDtypeStruct(w.shape, BF16) for w in later_weights]],
        scratch_shapes=[pltpu.VMEM((subs, heads, tq, HEAD_DIM), F32),
                        pltpu.SMEM((subs,), jnp.int32),
                        pltpu.VMEM((seq, width), BF16),
                        pltpu.VMEM((seq, width), BF16),
                        pltpu.SemaphoreType.DMA((2,))],
        compiler_params=_params(3),
        name="stickbreak_attn",
    )(qkv, qkv, *later_weights)
    return outs[0], outs[1:]


def _lru_kernel(x_ref, gate_ref, cw_ref, cb_ref, wr_ref, wi_ref, br_ref, bi_ref, lam_ref,
                gn_ref, later_w_ref, o_ref, later_w_bf_ref, xtail, stage, h_buf, hcarry, *, tt):
    t = pl.program_id(1)
    c = x_ref.shape[1]
    seg = tt // SUBLANES
    pitch = stage.shape[1] // SUBLANES

    @pl.when(t == 0)
    def _():
        xtail[...] = jnp.zeros((SUBLANES, c), F32)
        hcarry[...] = jnp.zeros((SUBLANES, c), F32)

    first_sublane = lax.broadcasted_iota(jnp.int32, (SUBLANES, LRU_BLOCK), 0) == 0
    log_sig_lam = _log_sigmoid(lam_ref[...])
    log2_a_coef = (0.5 * LRU_C * LOG2_E) * log_sig_lam
    neg_log_a_coef = (-0.5 * LRU_C) * log_sig_lam
    for n in range(c // LRU_BLOCK):
        sl = slice(n * LRU_BLOCK, (n + 1) * LRU_BLOCK)
        for k in range(SUBLANES):
            stage[n, k * pitch:k * pitch + seg, :] = x_ref[k * seg:(k + 1) * seg, sl]
        xs = [stage[n, pl.ds(j, SUBLANES, stride=pitch), :] for j in range(seg)]
        before = [jnp.where(first_sublane, xtail[SUBLANES - back:SUBLANES - back + 1, sl],
                            pltpu.roll(xs[seg - back], 1, 0))
                  for back in range(CONV_WIDTH - 1, 0, -1)]
        xs = before + xs
        taps = [cw_ref[k:k + 1, sl] for k in range(CONV_WIDTH)]
        bias = cb_ref[:, sl]
        conv = []
        for j in range(seg):
            acc = bias + taps[0] * xs[j]
            for k in range(1, CONV_WIDTH):
                acc = acc + taps[k] * xs[j + k]
            conv.append(acc)
        xc = jnp.concatenate(conv, axis=0)
        xcb = xc.astype(BF16)
        half_wr = (0.5 * wr_ref[n]).astype(BF16)
        half_wi = (0.5 * wi_ref[n]).astype(BF16)
        tanh_r = jnp.tanh(jnp.dot(xcb, half_wr, preferred_element_type=F32)
                          + 0.5 * br_ref[:, sl])
        tanh_i = jnp.tanh(jnp.dot(xcb, half_wi, preferred_element_type=F32)
                          + 0.5 * bi_ref[:, sl])
        two_r = tanh_r + 1.0
        a = jnp.exp2(two_r * log2_a_coef[:, sl])
        one_m_a2 = jnp.tanh(two_r * neg_log_a_coef[:, sl]) * (1.0 + a * a)
        mult = one_m_a2 * lax.rsqrt(jnp.maximum(one_m_a2, F32_TINY))
        b = mult * ((tanh_i + 1.0) * xc)

        a = a.reshape(seg, SUBLANES, LRU_BLOCK)
        b = b.reshape(seg, SUBLANES, LRU_BLOCK)
        hs, ps = [b[0]], [a[0]]
        for j in range(1, seg):
            hs.append(a[j] * hs[-1] + b[j])
            ps.append(a[j] * ps[-1])
        entering = [hcarry[0:1, sl]]
        for k in range(SUBLANES):
            entering.append(hs[-1][k:k + 1, :] + ps[-1][k:k + 1, :] * entering[-1])
        hcarry[:, sl] = jnp.broadcast_to(entering[-1], (SUBLANES, LRU_BLOCK))
        enter = jnp.concatenate(entering[:-1], axis=0)
        for j in range(seg):
            stage[n, pl.ds(j, SUBLANES, stride=pitch), :] = hs[j] + ps[j] * enter
        for k in range(SUBLANES):
            h_buf[k * seg:(k + 1) * seg, sl] = stage[n, k * pitch:k * pitch + seg, :]

    xtail[...] = x_ref[tt - SUBLANES:tt, :]
    later_w_bf_ref[...] = later_w_ref[...].astype(BF16)

    g = gate_ref[...]
    inner = g * (GELU_C1 + GELU_C3 * (g * g))
    y4 = (h_buf[...] * g) * (1.0 + jnp.tanh(inner))
    o_ref[...] = _rms(y4, gn_ref[...], eps=16.0 * RMS_EPS).astype(o_ref.dtype)


def _lru_branch(lru_in, conv_w, conv_b, w_r, w_i, b_r, b_i, lam, gain, later_w, batch, seq,
                tt=1024):
    m = lru_in.shape[0]
    c = lru_in.shape[1] // 2
    nt = seq // tt
    n_blocks = c // LRU_BLOCK
    assert seq % tt == 0 and tt % (SUBLANES * SUBLANES) == 0 and c % LRU_BLOCK == 0
    assert later_w.shape[0] % (16 * batch * nt) == 0
    row = lambda b, t: (0, 0)
    slab = pl.BlockSpec((later_w.shape[0] // (batch * nt), later_w.shape[1]),
                        lambda b, t: (b * nt + t, 0))
    return pl.pallas_call(
        functools.partial(_lru_kernel, tt=tt),
        grid=(batch, nt),
        in_specs=[
            pl.BlockSpec((tt, c), lambda b, t: (b * nt + t, 0)),
            pl.BlockSpec((tt, c), lambda b, t: (b * nt + t, 1)),
            pl.BlockSpec((CONV_WIDTH, c), row),
            pl.BlockSpec((1, c), row),
            pl.BlockSpec((n_blocks, LRU_BLOCK, LRU_BLOCK), lambda b, t: (0, 0, 0)),
            pl.BlockSpec((n_blocks, LRU_BLOCK, LRU_BLOCK), lambda b, t: (0, 0, 0)),
            pl.BlockSpec((1, c), row),
            pl.BlockSpec((1, c), row),
            pl.BlockSpec((1, c), row),
            pl.BlockSpec((1, c), row),
            slab,
        ],
        out_specs=[pl.BlockSpec((tt, c), lambda b, t: (b * nt + t, 0)), slab],
        out_shape=[jax.ShapeDtypeStruct((m, c), BF16),
                   jax.ShapeDtypeStruct(later_w.shape, BF16)],
        scratch_shapes=[
            pltpu.VMEM((SUBLANES, c), F32),
            pltpu.VMEM((n_blocks, SUBLANES * (tt // SUBLANES + LRU_SEG_PAD), LRU_BLOCK), F32),
            pltpu.VMEM((tt, c), F32),
            pltpu.VMEM((SUBLANES, c), F32),
        ],
        compiler_params=_params(2),
        name="rglru",
    )(lru_in, lru_in, conv_w, conv_b, w_r, w_i, b_r, b_i, lam, gain, later_w)


def _outproj_kernel(ya_ref, yl_ref, ga_ref, w_ref, x_ref, gpost_ref, x1_ref):
    ya = _rms(ya_ref[...], ga_ref[...]).astype(BF16)
    y = jnp.concatenate([ya, yl_ref[...]], axis=1)
    mix = jnp.dot(y, w_ref[...], preferred_element_type=F32)
    x1_ref[...] = x_ref[...] + _rms(mix, gpost_ref[...])


def _outproj(y_attn, y_lru, attn_gain, w_bf, x2, post_gain, tm=512):
    m, d = x2.shape
    ca = y_attn.shape[1]
    cl = y_lru.shape[1]
    assert m % tm == 0
    row = lambda i: (0, 0)
    return pl.pallas_call(
        _outproj_kernel,
        grid=(m // tm,),
        in_specs=[
            pl.BlockSpec((tm, ca), lambda i: (i, 0)),
            pl.BlockSpec((tm, cl), lambda i: (i, 0)),
            pl.BlockSpec((1, ca), row),
            pl.BlockSpec((ca + cl, d), row, pipeline_mode=pl.Buffered(1)),
            pl.BlockSpec((tm, d), lambda i: (i, 0)),
            pl.BlockSpec((1, d), row),
        ],
        out_specs=pl.BlockSpec((tm, d), lambda i: (i, 0)),
        out_shape=jax.ShapeDtypeStruct((m, d), F32),
        compiler_params=_params(1),
        name="outproj",
    )(y_attn, y_lru, attn_gain, w_bf, x2, post_gain)


def _ffn_kernel(x1_ref, gpre_ref, wg_ref, wu_ref, wd_ref, gpost_ref, o_ref, hn_ref):
    f = pl.program_id(1)
    last = pl.num_programs(1) - 1
    half = o_ref.shape[0] // 2
    halves = [slice(0, half), slice(half, 2 * half)]

    def chunks(rows):
        return [slice(r0, r0 + NORM_ROWS) for r0 in range(rows.start, rows.stop, NORM_ROWS)]

    def pre_norm(rows):
        for r in chunks(rows):
            hn_ref[r, :] = _rms(x1_ref[r, :], gpre_ref[...]).astype(BF16)

    def post_norm_residual(rows):
        for r in chunks(rows):
            o_ref[r, :] = x1_ref[r, :] + _rms(o_ref[r, :], gpost_ref[...])

    def ffn(rows, first):
        hn = hn_ref[rows, :]
        for c0 in range(0, wg_ref.shape[1], FFN_SLAB):
            cols = slice(c0, c0 + FFN_SLAB)
            gate = jnp.dot(hn, wg_ref[:, cols], preferred_element_type=F32)
            up = jnp.dot(hn, wu_ref[:, cols], preferred_element_type=F32)
            act = (jax.nn.silu(gate) * up).astype(BF16)
            part = jnp.dot(act, wd_ref[cols, :].astype(BF16), preferred_element_type=F32)
            if first and c0 == 0:
                o_ref[rows, :] = part
            else:
                o_ref[rows, :] += part

    @pl.when(f == 0)
    def _():
        for rows in halves:
            pre_norm(rows)
            ffn(rows, True)

    @pl.when(jnp.logical_and(f > 0, f < last))
    def _():
        for rows in halves:
            ffn(rows, False)

    @pl.when(f == last)
    def _():
        for rows in halves:
            ffn(rows, False)
            post_norm_residual(rows)


def _ffn(x1, pre_gain, wg, wu, wd, post_gain, tm=1024, tf=512):
    m, d = x1.shape
    dff = wg.shape[1]
    assert m % tm == 0 and dff % tf == 0 and dff // tf >= 2 and tf % FFN_SLAB == 0
    assert tm % (2 * NORM_ROWS) == 0
    return pl.pallas_call(
        _ffn_kernel,
        grid=(m // tm, dff // tf),
        in_specs=[
            pl.BlockSpec((tm, d), lambda i, f: (i, 0)),
            pl.BlockSpec((1, d), lambda i, f: (0, 0)),
            pl.BlockSpec((d, tf), lambda i, f: (0, f)),
            pl.BlockSpec((d, tf), lambda i, f: (0, f)),
            pl.BlockSpec((tf, d), lambda i, f: (f, 0)),
            pl.BlockSpec((1, d), lambda i, f: (0, 0)),
        ],
        out_specs=pl.BlockSpec((tm, d), lambda i, f: (i, 0)),
        out_shape=jax.ShapeDtypeStruct((m, d), F32),
        scratch_shapes=[pltpu.VMEM((tm, d), BF16)],
        compiler_params=_params(2),
        name="swiglu_ffn",
    )(x1, pre_gain, wg, wu, wd, post_gain)


def _layer(x2, batch, seq, pre_mix, post_mix, pre_ffn, post_ffn, w_in, conv_w, conv_b, w_r, b_r,
           w_i, b_i, lam, attn_gain, lru_gain, w_out, w_g, w_u, w_d):
    row = lambda v: v.reshape(1, -1)
    lru_width = conv_w.shape[1]
    qkv_width = w_in.shape[1] - 2 * lru_width
    n_heads = qkv_width // (3 * HEAD_DIM)

    qkv, lru_in = _inproj(x2, row(pre_mix), w_in, qkv_width)
    y_attn, (w_out_bf, w_g_bf) = _attention(qkv, batch, seq, n_heads, (w_out, w_g))
    y_lru, w_u_bf = _lru_branch(lru_in, conv_w, row(conv_b), w_r, w_i, row(b_r), row(b_i),
                                row(lam), row(lru_gain), w_u, batch, seq)
    x1 = _outproj(y_attn, y_lru, row(attn_gain), w_out_bf, x2, row(post_mix))
    return _ffn(x1, row(pre_ffn), w_g_bf, w_u_bf, w_d, row(post_ffn))


def kernel(x, pre_mix_norm, post_mix_norm, pre_ffn_norm, post_ffn_norm, w_in, conv_w, conv_b,
           w_rgate, b_rgate, w_igate, b_igate, lru_lambda, attn_out_norm, lru_out_norm, w_out,
           w_ffn_gate, w_ffn_up, w_ffn_down):
    batch, seq, d = x.shape
    x2 = x.reshape(batch * seq, d)
    for l in range(w_in.shape[0]):
        x2 = _layer(x2, batch, seq, pre_mix_norm[l], post_mix_norm[l], pre_ffn_norm[l],
                    post_ffn_norm[l], w_in[l], conv_w[l], conv_b[l], w_rgate[l], b_rgate[l],
                    w_igate[l], b_igate[l], lru_lambda[l], attn_out_norm[l], lru_out_norm[l],
                    w_out[l], w_ffn_gate[l], w_ffn_up[l], w_ffn_down[l])
    return x2.reshape(batch, seq, d)
```

```python
import functools
import math

import jax
import jax.numpy as jnp
from jax import lax
from jax.experimental import pallas as pl
from jax.experimental.pallas import tpu as pltpu

F32 = jnp.float32
BF16 = jnp.bfloat16

HEAD_DIM = 128
ATTN_STACK = 4
LRU_BLOCK = 128
LRU_SEG_PAD = 4
CONV_WIDTH = 4
LRU_C = 8.0
RMS_EPS = 1e-6
LOG2_E = math.log2(math.e)
Q_SCALE = LOG2_E / math.sqrt(HEAD_DIM)
F32_TINY = float(jnp.finfo(jnp.float32).tiny)
GELU_C1 = math.sqrt(2.0 / math.pi)
GELU_C3 = GELU_C1 * 0.044715
F32_ZERO_LOG2 = -150.0
SUBLANES = 8
NORM_ROWS = 128
FFN_SLAB = 256
VMEM_LIMIT = 60 * 1024 * 1024


def _params(n_axes):
    return pltpu.CompilerParams(dimension_semantics=("arbitrary",) * n_axes,
                                vmem_limit_bytes=VMEM_LIMIT)


def _rms(x, gain, eps=RMS_EPS):
    var = jnp.mean(x * x, axis=-1, keepdims=True)
    return (x * lax.rsqrt(var + eps)) * gain


def _log_sigmoid(z):
    return jnp.minimum(z, 0.0) - jnp.log1p(jnp.exp(-jnp.abs(z)))


def _inproj_kernel(x_hbm, g_ref, w_ref, qkv_ref, lru_ref, hn_ref, xbuf, sem, *, n_q_tiles):
    i, j = pl.program_id(0), pl.program_id(1)
    tm = xbuf.shape[0]

    def fetch(tile):
        return pltpu.make_async_copy(x_hbm.at[pl.ds(tile * tm, tm), :], xbuf, sem)

    @pl.when(jnp.logical_and(i == 0, j == 0))
    def _():
        fetch(0).start()

    def project(rows):
        acc = jnp.dot(hn_ref[rows, :], w_ref[...].astype(BF16), preferred_element_type=F32)
        qkv_ref[rows, :] = (acc * jnp.where(j < n_q_tiles, Q_SCALE, 1.0)).astype(BF16)
        lru_ref[rows, :] = acc

    @pl.when(j == 0)
    def _():
        fetch(i).wait()
        for h0 in range(0, tm, tm // 2):
            for r0 in range(h0, h0 + tm // 2, NORM_ROWS):
                rows = slice(r0, r0 + NORM_ROWS)
                hn_ref[rows, :] = _rms(xbuf[rows, :], g_ref[...]).astype(BF16)
            project(slice(h0, h0 + tm // 2))

    @pl.when(jnp.logical_and(j == 1, i + 1 < pl.num_programs(0)))
    def _():
        fetch(i + 1).start()

    @pl.when(j > 0)
    def _():
        project(slice(0, tm))


def _inproj(x2, gain, w, qkv_width, tm=2048, tn=512):
    m, d = x2.shape
    n = w.shape[1]
    lru_width = n - qkv_width
    assert m % tm == 0 and n % tn == 0 and qkv_width % (3 * tn) == 0 and tm % (2 * NORM_ROWS) == 0
    n_qkv_tiles = qkv_width // tn
    n_q_tiles = qkv_width // (3 * tn)
    grid = (m // tm, n // tn)
    return pl.pallas_call(
        functools.partial(_inproj_kernel, n_q_tiles=n_q_tiles),
        grid=grid,
        in_specs=[
            pl.BlockSpec(memory_space=pl.ANY),
            pl.BlockSpec((1, d), lambda i, j: (0, 0)),
            pl.BlockSpec((d, tn), lambda i, j: (0, j)),
        ],
        out_specs=[
            pl.BlockSpec((tm, tn), lambda i, j: (i, jnp.minimum(j, n_qkv_tiles))),
            pl.BlockSpec((tm, tn), lambda i, j: (i, jnp.maximum(j - n_qkv_tiles, 0))),
        ],
        out_shape=[
            jax.ShapeDtypeStruct((m, qkv_width + tn), BF16),
            jax.ShapeDtypeStruct((m, lru_width), F32),
        ],
        scratch_shapes=[pltpu.VMEM((tm, d), BF16), pltpu.VMEM((tm, d), F32),
                        pltpu.SemaphoreType.DMA(())],
        compiler_params=_params(2),
        name="inproj",
    )(x2, gain, w)


def _attn_kernel(q_ref, qkv_hbm, ga_ref, *refs, tq, subs, heads, n_cast, k_col, v_col):
    cast_in, y_ref = refs[:n_cast], refs[n_cast]
    cast_out = refs[n_cast + 1:2 * n_cast + 1]
    tail_ref, live_ref, k_ref, v_ref, kv_sems, o_ref = refs[2 * n_cast + 1:]

    def cast_weights():
        for src, dst in zip(cast_in, cast_out):
            dst[...] = src[...].astype(BF16)

    tk = tq
    step = pl.program_id(2)
    rows_per_step = subs * tq
    seq = k_ref.shape[0]
    width = k_ref.shape[1]

    def kv_fetch(chunk):
        src_rows = pl.ds(pl.program_id(0) * seq + chunk * rows_per_step, rows_per_step)
        dst_rows = pl.ds(chunk * rows_per_step, rows_per_step)
        group = pl.program_id(1)
        return [pltpu.make_async_copy(
                    qkv_hbm.at[src_rows, pl.ds(col0 + group * width, width)],
                    buf.at[dst_rows, :], kv_sems.at[which])
                for which, (col0, buf) in enumerate(((k_col, k_ref), (v_col, v_ref)))]

    @pl.when(step == 0)
    def _():
        for copy in kv_fetch(0):
            copy.start()

    for copy in kv_fetch(step):
        copy.wait()

    @pl.when(step + 1 < pl.num_programs(2))
    def _():
        for copy in kv_fetch(step + 1):
            copy.start()

    row = lax.broadcasted_iota(jnp.int32, (tk, tk), 0)
    col = lax.broadcasted_iota(jnp.int32, (tk, tk), 1)
    neg_later = jnp.where(row > col, -1.0, 0.0).astype(BF16)

    def tile(s, kj, first):
        rows = slice(s * tq, (s + 1) * tq)
        start = pl.multiple_of(kj * tk, tk)
        for h0 in range(0, heads, ATTN_STACK):
            head_group(s, rows, start, range(h0, min(h0 + ATTN_STACK, heads)), first)

    def head_group(s, rows, start, group, first):
        log_betas = []
        softplus = []
        for h in group:
            sl = slice(h * HEAD_DIM, (h + 1) * HEAD_DIM)
            k_t = k_ref[pl.ds(start, tk), sl]
            z2 = lax.dot_general(q_ref[rows, sl], k_t, (((1,), (1,)), ((), ())),
                                 preferred_element_type=F32)
            pos = jnp.maximum(z2, 0.0)
            neg = z2 - pos
            log_term = jnp.log(1.0 + jnp.exp2(neg - pos)) * LOG2_E
            sp = pos + log_term
            if first:
                sp = jnp.where(col < row, sp, 0.0)
            softplus.append(sp)
            log_betas.append(neg - log_term)
        sp_all = jnp.concatenate(softplus, axis=0).astype(BF16)
        suffix_all = jnp.dot(sp_all, neg_later, preferred_element_type=F32)
        for n, h in enumerate(group):
            sl = slice(h * HEAD_DIM, (h + 1) * HEAD_DIM)
            v_t = v_ref[pl.ds(start, tk), sl]
            suffix = suffix_all[n * tq:(n + 1) * tq]
            expo = log_betas[n] + suffix
            if not first:
                expo = expo + jnp.concatenate([tail_ref[s, h]] * (tk // HEAD_DIM), axis=1)
            w = jnp.exp2(expo)
            if first:
                w = jnp.where(col < row, w, 0.0)
            pv = jnp.dot(w.astype(BF16), v_t, preferred_element_type=F32)
            total = jnp.broadcast_to(suffix[:, 0:1] - softplus[n][:, 0:1], (tq, HEAD_DIM))
            if first:
                o_ref[rows, sl] = pv
                tail_ref[s, h] = total
            else:
                o_ref[rows, sl] += pv
                tail_ref[s, h] += total

    def stick_left(s):
        return jnp.max(tail_ref[s]) > F32_ZERO_LOG2

    @pl.when(step == 0)
    def _():
        cast_weights()
        for s in range(subs):
            tile(s, s, True)
            if s > 0:
                tile(s, s - 1, False)
            live_ref[s] = stick_left(s).astype(jnp.int32)

    @pl.when(step > 0)
    def _():
        cast_weights()
        for s in range(subs):
            tile(s, step * subs + s, True)
            tile(s, step * subs + s - 1, False)
            live_ref[s] = stick_left(s).astype(jnp.int32)

    for s in range(subs):
        qb = step * subs + s

        def body(carry, s=s, qb=qb):
            it, _ = carry
            tile(s, qb - 1 - it, False)
            return it + 1, stick_left(s)

        lax.while_loop(lambda c, qb=qb: jnp.logical_and(c[0] < qb, c[1]), body,
                       (jnp.int32(1), live_ref[s] > 0))

    for r0 in range(0, rows_per_step, NORM_ROWS):
        r = slice(r0, r0 + NORM_ROWS)
        y_ref[r, :] = _rms(o_ref[r, :], ga_ref[...]).astype(BF16)


def _attention(qkv, attn_gain, batch, seq, n_heads, later_weights, tq=256, subs=2, heads=8):
    assert heads == n_heads
    m = qkv.shape[0]
    rows = tq * subs
    nq = seq // rows
    ng = n_heads // heads
    width = heads * HEAD_DIM
    n_steps = batch * ng * nq
    assert seq % rows == 0 and n_heads % heads == 0
    assert all(w.shape[0] % (16 * n_steps) == 0 for w in later_weights)
    step = lambda b, g, i: ((b * ng + g) * nq + i, 0)
    slab_specs = [pl.BlockSpec((w.shape[0] // n_steps, w.shape[1]), step) for w in later_weights]
    outs = pl.pallas_call(
        functools.partial(_attn_kernel, tq=tq, subs=subs, heads=heads,
                          n_cast=len(later_weights), k_col=ng * width, v_col=2 * ng * width),
        grid=(batch, ng, nq),
        in_specs=[
            pl.BlockSpec((rows, width), lambda b, g, i: (b * nq + i, g)),
            pl.BlockSpec(memory_space=pl.ANY),
            pl.BlockSpec((1, width), lambda b, g, i: (0, 0)),
            *slab_specs,
        ],
        out_specs=[pl.BlockSpec((rows, width), lambda b, g, i: (b * nq + i, g)), *slab_specs],
        out_shape=[jax.ShapeDtypeStruct((m, n_heads * HEAD_DIM), BF16),
                   *[jax.ShapeDtypeStruct(w.shape, BF16) for w in later_weights]],
        scratch_shapes=[pltpu.VMEM((subs, heads, tq, HEAD_DIM), F32),
                        pltpu.SMEM((subs,), jnp.int32),
                        pltpu.VMEM((seq, width), BF16),
                        pltpu.VMEM((seq, width), BF16),
                        pltpu.SemaphoreType.DMA((2,)),
                        pltpu.VMEM((rows, width), F32)],
        compiler_params=_params(3),
        name="stickbreak_attn",
    )(qkv, qkv, attn_gain, *later_weights)
    return outs[0], outs[1:]


def _lru_kernel(x_ref, gate_ref, cw_ref, cb_ref, wr_ref, wi_ref, br_ref, bi_ref, lam_ref,
                gn_ref, later_w_ref, o_ref, later_w_bf_ref, xtail, stage, h_buf, hcarry, *, tt):
    t = pl.program_id(1)
    c = x_ref.shape[1]
    seg = tt // SUBLANES
    pitch = stage.shape[1] // SUBLANES

    @pl.when(t == 0)
    def _():
        xtail[...] = jnp.zeros((SUBLANES, c), F32)
        hcarry[...] = jnp.zeros((SUBLANES, c), F32)

    first_sublane = lax.broadcasted_iota(jnp.int32, (SUBLANES, LRU_BLOCK), 0) == 0
    log_sig_lam = _log_sigmoid(lam_ref[...])
    log2_a_coef = (0.5 * LRU_C * LOG2_E) * log_sig_lam
    neg_log_a_coef = (-0.5 * LRU_C) * log_sig_lam
    for n in range(c // LRU_BLOCK):
        sl = slice(n * LRU_BLOCK, (n + 1) * LRU_BLOCK)
        for k in range(SUBLANES):
            stage[n, k * pitch:k * pitch + seg, :] = x_ref[k * seg:(k + 1) * seg, sl]
        xs = [stage[n, pl.ds(j, SUBLANES, stride=pitch), :] for j in range(seg)]
        before = [jnp.where(first_sublane, xtail[SUBLANES - back:SUBLANES - back + 1, sl],
                            pltpu.roll(xs[seg - back], 1, 0))
                  for back in range(CONV_WIDTH - 1, 0, -1)]
        xs = before + xs
        taps = [cw_ref[k:k + 1, sl] for k in range(CONV_WIDTH)]
        bias = cb_ref[:, sl]
        conv = []
        for j in range(seg):
            acc = bias + taps[0] * xs[j]
            for k in range(1, CONV_WIDTH):
                acc = acc + taps[k] * xs[j + k]
            conv.append(acc)
        xc = jnp.concatenate(conv, axis=0)
        xcb = xc.astype(BF16)
        half_wr = (0.5 * wr_ref[n]).astype(BF16)
        half_wi = (0.5 * wi_ref[n]).astype(BF16)
        tanh_r = jnp.tanh(jnp.dot(xcb, half_wr, preferred_element_type=F32)
                          + 0.5 * br_ref[:, sl])
        tanh_i = jnp.tanh(jnp.dot(xcb, half_wi, preferred_element_type=F32)
                          + 0.5 * bi_ref[:, sl])
        two_r = tanh_r + 1.0
        a = jnp.exp2(two_r * log2_a_coef[:, sl])
        one_m_a2 = jnp.tanh(two_r * neg_log_a_coef[:, sl]) * (1.0 + a * a)
        mult = one_m_a2 * lax.rsqrt(jnp.maximum(one_m_a2, F32_TINY))
        b = mult * ((tanh_i + 1.0) * xc)

        a = a.reshape(seg, SUBLANES, LRU_BLOCK)
        b = b.reshape(seg, SUBLANES, LRU_BLOCK)
        hs, ps = [b[0]], [a[0]]
        for j in range(1, seg):
            hs.append(a[j] * hs[-1] + b[j])
            ps.append(a[j] * ps[-1])
        entering = [hcarry[0:1, sl]]
        for k in range(SUBLANES):
            entering.append(hs[-1][k:k + 1, :] + ps[-1][k:k + 1, :] * entering[-1])
        hcarry[:, sl] = jnp.broadcast_to(entering[-1], (SUBLANES, LRU_BLOCK))
        enter = jnp.concatenate(entering[:-1], axis=0)
        for j in range(seg):
            stage[n, pl.ds(j, SUBLANES, stride=pitch), :] = hs[j] + ps[j] * enter
        for k in range(SUBLANES):
            h_buf[k * seg:(k + 1) * seg, sl] = stage[n, k * pitch:k * pitch + seg, :]

    xtail[...] = x_ref[tt - SUBLANES:tt, :]
    later_w_bf_ref[...] = later_w_ref[...].astype(BF16)

    g = gate_ref[...]
    inner = g * (GELU_C1 + GELU_C3 * (g * g))
    y4 = (h_buf[...] * g) * (1.0 + jnp.tanh(inner))
    o_ref[...] = _rms(y4, gn_ref[...], eps=16.0 * RMS_EPS).astype(o_ref.dtype)


def _lru_branch(lru_in, conv_w, conv_b, w_r, w_i, b_r, b_i, lam, gain, later_w, batch, seq,
                tt=1024):
    m = lru_in.shape[0]
    c = lru_in.shape[1] // 2
    nt = seq // tt
    n_blocks = c // LRU_BLOCK
    assert seq % tt == 0 and tt % (SUBLANES * SUBLANES) == 0 and c % LRU_BLOCK == 0
    assert later_w.shape[0] % (16 * batch * nt) == 0
    row = lambda b, t: (0, 0)
    slab = pl.BlockSpec((later_w.shape[0] // (batch * nt), later_w.shape[1]),
                        lambda b, t: (b * nt + t, 0))
    return pl.pallas_call(
        functools.partial(_lru_kernel, tt=tt),
        grid=(batch, nt),
        in_specs=[
            pl.BlockSpec((tt, c), lambda b, t: (b * nt + t, 0)),
            pl.BlockSpec((tt, c), lambda b, t: (b * nt + t, 1)),
            pl.BlockSpec((CONV_WIDTH, c), row),
            pl.BlockSpec((1, c), row),
            pl.BlockSpec((n_blocks, LRU_BLOCK, LRU_BLOCK), lambda b, t: (0, 0, 0)),
            pl.BlockSpec((n_blocks, LRU_BLOCK, LRU_BLOCK), lambda b, t: (0, 0, 0)),
            pl.BlockSpec((1, c), row),
            pl.BlockSpec((1, c), row),
            pl.BlockSpec((1, c), row),
            pl.BlockSpec((1, c), row),
            slab,
        ],
        out_specs=[pl.BlockSpec((tt, c), lambda b, t: (b * nt + t, 0)), slab],
        out_shape=[jax.ShapeDtypeStruct((m, c), BF16),
                   jax.ShapeDtypeStruct(later_w.shape, BF16)],
        scratch_shapes=[
            pltpu.VMEM((SUBLANES, c), F32),
            pltpu.VMEM((n_blocks, SUBLANES * (tt // SUBLANES + LRU_SEG_PAD), LRU_BLOCK), F32),
            pltpu.VMEM((tt, c), F32),
            pltpu.VMEM((SUBLANES, c), F32),
        ],
        compiler_params=_params(2),
        name="rglru",
    )(lru_in, lru_in, conv_w, conv_b, w_r, w_i, b_r, b_i, lam, gain, later_w)


def _outproj_kernel(ya_ref, yl_ref, w_ref, x_ref, gpost_ref, x1_ref):
    y = jnp.concatenate([ya_ref[...], yl_ref[...]], axis=1)
    mix = jnp.dot(y, w_ref[...], preferred_element_type=F32)
    x1_ref[...] = x_ref[...] + _rms(mix, gpost_ref[...])


def _outproj(y_attn, y_lru, w_bf, x2, post_gain, tm=512):
    m, d = x2.shape
    ca = y_attn.shape[1]
    cl = y_lru.shape[1]
    assert m % tm == 0
    row = lambda i: (0, 0)
    return pl.pallas_call(
        _outproj_kernel,
        grid=(m // tm,),
        in_specs=[
            pl.BlockSpec((tm, ca), lambda i: (i, 0)),
            pl.BlockSpec((tm, cl), lambda i: (i, 0)),
            pl.BlockSpec((ca + cl, d), row, pipeline_mode=pl.Buffered(1)),
            pl.BlockSpec((tm, d), lambda i: (i, 0)),
            pl.BlockSpec((1, d), row),
        ],
        out_specs=pl.BlockSpec((tm, d), lambda i: (i, 0)),
        out_shape=jax.ShapeDtypeStruct((m, d), F32),
        compiler_params=_params(1),
        name="outproj",
    )(y_attn, y_lru, w_bf, x2, post_gain)


def _ffn_kernel(x1_ref, gpre_ref, wg_ref, wu_ref, wd_ref, gpost_ref, o_ref, hn_ref):
    f = pl.program_id(1)
    last = pl.num_programs(1) - 1
    half = o_ref.shape[0] // 2
    halves = [slice(0, half), slice(half, 2 * half)]

    def chunks(rows):
        return [slice(r0, r0 + NORM_ROWS) for r0 in range(rows.start, rows.stop, NORM_ROWS)]

    def pre_norm(rows):
        for r in chunks(rows):
            hn_ref[r, :] = _rms(x1_ref[r, :], gpre_ref[...]).astype(BF16)

    def post_norm_residual(rows):
        for r in chunks(rows):
            o_ref[r, :] = x1_ref[r, :] + _rms(o_ref[r, :], gpost_ref[...])

    def ffn(rows, first):
        hn = hn_ref[rows, :]
        for c0 in range(0, wg_ref.shape[1], FFN_SLAB):
            cols = slice(c0, c0 + FFN_SLAB)
            gate = jnp.dot(hn, wg_ref[:, cols], preferred_element_type=F32)
            up = jnp.dot(hn, wu_ref[:, cols], preferred_element_type=F32)
            act = (jax.nn.silu(gate) * up).astype(BF16)
            part = jnp.dot(act, wd_ref[cols, :].astype(BF16), preferred_element_type=F32)
            if first and c0 == 0:
                o_ref[rows, :] = part
            else:
                o_ref[rows, :] += part

    @pl.when(f == 0)
    def _():
        for rows in halves:
            pre_norm(rows)
            ffn(rows, True)

    @pl.when(jnp.logical_and(f > 0, f < last))
    def _():
        for rows in halves:
            ffn(rows, False)

    @pl.when(f == last)
    def _():
        for rows in halves:
            ffn(rows, False)
            post_norm_residual(rows)


def _ffn(x1, pre_gain, wg, wu, wd, post_gain, tm=1024, tf=512):
    m, d = x1.shape
    dff = wg.shape[1]
    assert m % tm == 0 and dff % tf == 0 and dff // tf >= 2 and tf % FFN_SLAB == 0
    assert tm % (2 * NORM_ROWS) == 0
    return pl.pallas_call(
        _ffn_kernel,
        grid=(m // tm, dff // tf),
        in_specs=[
            pl.BlockSpec((tm, d), lambda i, f: (i, 0)),
            pl.BlockSpec((1, d), lambda i, f: (0, 0)),
            pl.BlockSpec((d, tf), lambda i, f: (0, f)),
            pl.BlockSpec((d, tf), lambda i, f: (0, f)),
            pl.BlockSpec((tf, d), lambda i, f: (f, 0)),
            pl.BlockSpec((1, d), lambda i, f: (0, 0)),
        ],
        out_specs=pl.BlockSpec((tm, d), lambda i, f: (i, 0)),
        out_shape=jax.ShapeDtypeStruct((m, d), F32),
        scratch_shapes=[pltpu.VMEM((tm, d), BF16)],
        compiler_params=_params(2),
        name="swiglu_ffn",
    )(x1, pre_gain, wg, wu, wd, post_gain)


def _layer(x2, batch, seq, pre_mix, post_mix, pre_ffn, post_ffn, w_in, conv_w, conv_b, w_r, b_r,
           w_i, b_i, lam, attn_gain, lru_gain, w_out, w_g, w_u, w_d):
    row = lambda v: v.reshape(1, -1)
    lru_width = conv_w.shape[1]
    qkv_width = w_in.shape[1] - 2 * lru_width
    n_heads = qkv_width // (3 * HEAD_DIM)

    qkv, lru_in = _inproj(x2, row(pre_mix), w_in, qkv_width)
    y_attn, (w_out_bf, w_g_bf) = _attention(qkv, row(attn_gain), batch, seq, n_heads,
                                            (w_out, w_g))
    y_lru, w_u_bf = _lru_branch(lru_in, conv_w, row(conv_b), w_r, w_i, row(b_r), row(b_i),
                                row(lam), row(lru_gain), w_u, batch, seq)
    x1 = _outproj(y_attn, y_lru, w_out_bf, x2, row(post_mix))
    return _ffn(x1, row(pre_ffn), w_g_bf, w_u_bf, w_d, row(post_ffn))


def kernel(x, pre_mix_norm, post_mix_norm, pre_ffn_norm, post_ffn_norm, w_in, conv_w, conv_b,
           w_rgate, b_rgate, w_igate, b_igate, lru_lambda, attn_out_norm, lru_out_norm, w_out,
           w_ffn_gate, w_ffn_up, w_ffn_down):
    batch, seq, d = x.shape
    x2 = x.reshape(batch * seq, d)
    for l in range(w_in.shape[0]):
        x2 = _layer(x2, batch, seq, pre_mix_norm[l], post_mix_norm[l], pre_ffn_norm[l],
                    post_ffn_norm[l], w_in[l], conv_w[l], conv_b[l], w_rgate[l], b_rgate[l],
                    w_igate[l], b_igate[l], lru_lambda[l], attn_out_norm[l], lru_out_norm[l],
                    w_out[l], w_ffn_gate[l], w_ffn_up[l], w_ffn_down[l])
    return x2.reshape(batch, seq, d)
```
